```python
import math
import jax, jax.numpy as jnp
from jax import lax
import numpy as np

D_MODEL = 2048
BATCH = 2
SEQ = 8192
DEPTH = 2

GRID_W = 64
CTX_LEN = 256
N_BRANCH = 4
BRANCH_W = 512
BLOCK = 128
ROPE_BASE = 10000.0
EPS = 1e-6
D_FF = 4 * D_MODEL
DA_HEADS = 4
DA_DIM = 64
SSM_HEADS = 8
SSM_HEAD_DIM = 64
SSM_INNER = SSM_HEADS * SSM_HEAD_DIM
SSM_GROUPS = 2
SSM_STATE = 128
SSM_XBC = SSM_INNER + 2 * SSM_GROUPS * SSM_STATE
SSM_CONV = 5
SSM_CHUNK = 128
SW_HEADS = 8
SW_KV_HEADS = 2
SW_DIM = 64
SW_WINDOW = 128
NA_HEADS = 8
NA_DIM = 64
NA_ROWS = 8
NA_COLS = 16
IN_SPLITS = (DA_HEADS * 2 * DA_DIM, DA_HEADS * 2 * DA_DIM, DA_HEADS * 2 * DA_DIM,
             SSM_INNER, SSM_XBC, 2 * SSM_HEADS,
             SW_HEADS * SW_DIM, SW_KV_HEADS * SW_DIM, SW_KV_HEADS * SW_DIM,
             NA_HEADS * NA_DIM, NA_HEADS * NA_DIM, NA_HEADS * NA_DIM,
             N_BRANCH * D_MODEL)
IN_COLS = sum(IN_SPLITS)

kernel_name = "hybrid_diff_ssd_swa_na_dit_block"


def rmsnorm(t, g):
    tf = t.astype(jnp.float32)
    y = tf * lax.rsqrt(jnp.mean(tf * tf, -1, keepdims=True) + EPS)
    return (y * g.astype(jnp.float32)).astype(t.dtype)


def modulate(t, shift, scale):
    return t * (1 + scale) + shift


def split_cols(p):
    idx, acc = [], 0
    for w in IN_SPLITS[:-1]:
        acc += w
        idx.append(acc)
    return jnp.split(p, idx, axis=-1)


def axial_rope(n, d):
    t = jnp.arange(n)
    row = (t // GRID_W).astype(jnp.float32)
    col = (t % GRID_W).astype(jnp.float32)
    nf = d // 4
    inv = ROPE_BASE ** (-jnp.arange(nf, dtype=jnp.float32) / nf)
    ang = jnp.concatenate([row[:, None] * inv, col[:, None] * inv], -1)
    return jnp.cos(ang), jnp.sin(ang)


def apply_rope(t, cos, sin):
    shape = (t.shape[1],) + (1,) * (t.ndim - 3) + (cos.shape[-1],)
    cos = cos.reshape(shape)
    sin = sin.reshape(shape)
    t1, t2 = jnp.split(t.astype(jnp.float32), 2, -1)
    return jnp.concatenate([t1 * cos - t2 * sin, t2 * cos + t1 * sin], -1).astype(t.dtype)


def ctx_attend(q, k, v, sink=None):
    B, L, H, d = q.shape
    KV = k.shape[2]
    G = H // KV
    s = jnp.einsum('bqhgd,bkhd->bhgqk', q.reshape(B, L, KV, G, d), k).astype(jnp.float32) * d ** -0.5
    if sink is not None:
        s_sink = jnp.broadcast_to(sink.astype(jnp.float32).reshape(1, KV, G, 1, 1), s.shape[:-1] + (1,))
        s = jnp.concatenate([s, s_sink], -1)
    p = jax.nn.softmax(s, -1)[..., :k.shape[1]]
    o = jnp.einsum('bhgqk,bkhd->bqhgd', p.astype(v.dtype), v)
    return o.reshape(B, L, H * d)


def diff_core(q, k, v, lam):
    s = jnp.einsum('bqhmd,bkhmd->bhmqk', q, k).astype(jnp.float32) * q.shape[-1] ** -0.5
    p = jax.nn.softmax(s, -1)
    a = p[:, :, 0] - lam * p[:, :, 1]
    return jnp.einsum('bhqk,bkhe->bqhe', a.astype(v.dtype), v)


def diff_attn_latent(q, k_all, v_all, lam):
    B, N = q.shape[:2]
    nb = N // BLOCK
    qb = jnp.moveaxis(q.reshape((B, nb, BLOCK) + q.shape[2:]), 1, 0)
    o = lax.map(lambda qq: diff_core(qq, k_all, v_all, lam), qb)
    return jnp.moveaxis(o, 0, 1).reshape((B, N) + o.shape[3:])


def window_attn_latent(q, k, v, k_ctx, v_ctx, sink):
    B, N, H, d = q.shape
    KV = k.shape[2]
    G = H // KV
    nb = N // BLOCK
    qb = q.reshape(B, nb, BLOCK, KV, G, d)
    pad = ((0, 0), (BLOCK, BLOCK), (0, 0), (0, 0))
    kp = jnp.pad(k, pad).reshape(B, nb + 2, BLOCK, KV, d)
    vp = jnp.pad(v, pad).reshape(B, nb + 2, BLOCK, KV, d)
    k_band = jnp.concatenate([kp[:, :-2], kp[:, 1:-1], kp[:, 2:]], axis=2)
    v_band = jnp.concatenate([vp[:, :-2], vp[:, 1:-1], vp[:, 2:]], axis=2)
    scale = d ** -0.5
    s_win = jnp.einsum('bnqhgd,bnjhd->bnhgqj', qb, k_band).astype(jnp.float32) * scale
    blk = jnp.arange(nb)[:, None]
    q_pos = blk * BLOCK + jnp.arange(BLOCK)[None, :]
    k_pos = (blk - 1) * BLOCK + jnp.arange(3 * BLOCK)[None, :]
    kq = k_pos[:, None, :]
    valid = (jnp.abs(kq - q_pos[:, :, None]) <= SW_WINDOW) & (kq >= 0) & (kq < N)
    s_win = jnp.where(valid[None, :, None, None], s_win, -jnp.inf)
    s_ctx = jnp.einsum('bnqhgd,bjhd->bnhgqj', qb, k_ctx).astype(jnp.float32) * scale
    s_sink = jnp.broadcast_to(sink.astype(jnp.float32).reshape(1, 1, KV, G, 1, 1), s_ctx.shape[:-1] + (1,))
    p = jax.nn.softmax(jnp.concatenate([s_win, s_ctx, s_sink], -1), -1)
    nw = 3 * BLOCK
    L = k_ctx.shape[1]
    o = (jnp.einsum('bnhgqj,bnjhd->bnqhgd', p[..., :nw].astype(v.dtype), v_band)
         + jnp.einsum('bnhgqj,bjhd->bnqhgd', p[..., nw:nw + L].astype(v.dtype), v_ctx))
    return o.reshape(B, N, H * d)


def neighbourhood_attn_latent(q, k, v, k_ctx, v_ctx, rpb):
    B, N, H, d = q.shape
    W = GRID_W
    R = N // W
    KR = min(NA_ROWS, R)
    KC = NA_COLS
    qg = q.reshape(B, R, W, H, d)
    kg = k.reshape(B, R, W, H, d)
    vg = v.reshape(B, R, W, H, d)
    r = jnp.arange(R)
    row_idx = jnp.clip(r - KR // 2, 0, R - KR)[:, None] + jnp.arange(KR)[None, :]
    cc = jnp.arange(W)
    col_idx = jnp.clip(cc - KC // 2, 0, W - KC)[:, None] + jnp.arange(KC)[None, :]
    k_rows = kg[:, row_idx]
    v_rows = vg[:, row_idx]
    scale = d ** -0.5
    s_rows = jnp.einsum('brqhd,brjkhd->brhqjk', qg, k_rows)
    idx = jnp.broadcast_to(col_idx[:, None, :], (B, R, H, W, KR, KC))
    s_nb = jnp.take_along_axis(s_rows, idx, axis=-1).astype(jnp.float32) * scale
    ro = row_idx - r[:, None] + NA_ROWS - 1
    co = col_idx - cc[:, None] + NA_COLS - 1
    bias = rpb[:, ro[:, None, :, None], co[None, :, None, :]]
    s_nb = s_nb + jnp.moveaxis(bias, 0, 1)[None].astype(jnp.float32)
    s_ctx = jnp.einsum('brqhd,bjhd->brhqj', qg, k_ctx).astype(jnp.float32) * scale
    nk = KR * KC
    p = jax.nn.softmax(jnp.concatenate([s_nb.reshape(B, R, H, W, nk), s_ctx], -1), -1)
    p_nb = p[..., :nk].reshape(B, R, H, W, KR, KC).astype(v.dtype)
    onehot = (col_idx[:, :, None] == cc[None, None, :]).astype(v.dtype)
    p_rows = jnp.einsum('brhqjm,qmk->brhqjk', p_nb, onehot)
    o = (jnp.einsum('brhqjk,brjkhd->brqhd', p_rows, v_rows)
         + jnp.einsum('brhqj,bjhd->brqhd', p[..., nk:].astype(v.dtype), v_ctx))
    return o.reshape(B, N, H * d)


def dwconv(t, w, b):
    K = w.shape[0]
    y = lax.conv_general_dilated(t, w[:, None, :].astype(t.dtype), window_strides=(1,),
                                 padding=[(K // 2, K // 2)], dimension_numbers=('NWC', 'WIO', 'NWC'),
                                 feature_group_count=t.shape[-1])
    return y + b.astype(t.dtype)


def ssd_scan(x, dt, a, bm, cm, h0, with_y=True):
    Bn, n, H, P = x.shape
    G, S = bm.shape[2], bm.shape[3]
    R = H // G
    Lc = SSM_CHUNK
    nc = n // Lc
    xg = (x.astype(jnp.float32) * dt[..., None]).reshape(Bn, nc, Lc, G, R, P)
    bc = bm.astype(jnp.float32).reshape(Bn, nc, Lc, G, S)
    cc = cm.astype(jnp.float32).reshape(Bn, nc, Lc, G, S)
    acum = jnp.cumsum((dt * a).reshape(Bn, nc, Lc, G, R), axis=2)
    decay_states = jnp.exp(acum[:, :, -1:] - acum)
    states = jnp.einsum('bclgn,bclgr,bclgrp->bcgrpn', bc, decay_states, xg)
    chunk_decay = jnp.exp(acum[:, :, -1])

    def step(h, inp):
        dec, st = inp
        return h * dec[..., None, None] + st, h

    h_final, h_prev = lax.scan(step, h0, (jnp.moveaxis(chunk_decay, 1, 0), jnp.moveaxis(states, 1, 0)))
    if not with_y:
        return None, h_final
    seg = acum[:, :, :, None] - acum[:, :, None, :]
    tril = (jnp.arange(Lc)[:, None] >= jnp.arange(Lc)[None, :])[None, None, :, :, None, None]
    ldec = jnp.exp(jnp.where(tril, seg, -jnp.inf))
    cb = jnp.einsum('bclgn,bcsgn->bclsg', cc, bc)
    y_diag = jnp.einsum('bclsgr,bcsgrp->bclgrp', cb[..., None] * ldec, xg)
    y_off = jnp.einsum('bclgn,cbgrpn,bclgr->bclgrp', cc, h_prev, jnp.exp(acum))
    y = (y_diag + y_off).reshape(Bn, n, H, P).astype(x.dtype)
    return y, h_final


def ssm_branch(z, xbc, dtr, zc, xbcc, dtrc, conv_w, conv_b, dt_bias, a_log, d_skip, g_ssm, need_ctx):
    A = -jnp.exp(a_log.astype(jnp.float32))

    def prep(xbc_, dtr_):
        Bn, n = xbc_.shape[:2]
        u = jax.nn.silu(dwconv(xbc_, conv_w, conv_b))
        xs, bm, cm = jnp.split(u, [SSM_INNER, SSM_INNER + SSM_GROUPS * SSM_STATE], -1)
        xs = xs.reshape(Bn, n, SSM_HEADS, SSM_HEAD_DIM)
        bm = bm.reshape(Bn, n, SSM_GROUPS, SSM_STATE)
        cm = cm.reshape(Bn, n, SSM_GROUPS, SSM_STATE)
        dt = jax.nn.softplus(dtr_.reshape(Bn, n, 2, SSM_HEADS).astype(jnp.float32) + dt_bias.astype(jnp.float32))
        return xs, bm, cm, dt

    xs, bm, cm, dt = prep(xbc, dtr)
    xsc, bmc, cmc, dtc = prep(xbcc, dtrc)
    h0 = jnp.zeros((xs.shape[0], SSM_GROUPS, SSM_HEADS // SSM_GROUPS, SSM_HEAD_DIM, SSM_STATE), jnp.float32)
    fl = lambda t: jnp.flip(t, 1)
    yc_f, hc_f = ssd_scan(xsc, dtc[:, :, 0], A[0], bmc, cmc, h0, need_ctx)
    y_f, _ = ssd_scan(xs, dt[:, :, 0], A[0], bm, cm, hc_f)
    yc_b, hc_b = ssd_scan(fl(xsc), fl(dtc[:, :, 1]), A[1], fl(bmc), fl(cmc), h0, need_ctx)
    y_b, _ = ssd_scan(fl(xs), fl(dt[:, :, 1]), A[1], fl(bm), fl(cm), hc_b)

    def finish(yf, yb_rev, xs_, z_):
        y = yf + fl(yb_rev) + d_skip[:, None].astype(xs_.dtype) * xs_
        return rmsnorm(y.reshape(z_.shape) * jax.nn.silu(z_), g_ssm)

    y = finish(y_f, y_b, xs, z)
    if not need_ctx:
        return y, None
    return y, finish(yc_f, yc_b, xsc, zc)


def merge(ys, gate, w_branch, w_out):
    D = w_out.shape[0]
    g = jax.nn.sigmoid(gate).reshape(gate.shape[:-1] + (N_BRANCH, D))
    m = g[..., 0, :] * (ys[0] @ w_branch[0])
    for i in range(1, N_BRANCH):
        m = m + g[..., i, :] * (ys[i] @ w_branch[i])
    return m @ w_out


def mixer(h, hc, cos, sin, lam_init, need_ctx, w_in, lam_q1, lam_k1, lam_q2, lam_k2, g_subln,
          conv_w, conv_b, dt_bias, a_log, d_skip, g_ssm, sink, rpb, w_branch, w_out):
    B, N, _ = h.shape
    L = hc.shape[1]
    aq, ak, av, bz, bx, bdt, sq, sk, sv, nq, nk, nv, gate = split_cols(h @ w_in)
    aqc, akc, avc, bzc, bxc, bdtc, sqc, skc, svc, nqc, nkc, nvc, gatec = split_cols(hc @ w_in)
    lam = (jnp.exp(jnp.sum(lam_q1.astype(jnp.float32) * lam_k1.astype(jnp.float32)))
           - jnp.exp(jnp.sum(lam_q2.astype(jnp.float32) * lam_k2.astype(jnp.float32))) + lam_init)
    qa = apply_rope(aq.reshape(B, N, DA_HEADS, 2, DA_DIM), cos, sin)
    ka = apply_rope(ak.reshape(B, N, DA_HEADS, 2, DA_DIM), cos, sin)
    va = av.reshape(B, N, DA_HEADS, 2 * DA_DIM)
    kac = akc.reshape(B, L, DA_HEADS, 2, DA_DIM)
    vac = avc.reshape(B, L, DA_HEADS, 2 * DA_DIM)
    oa = diff_attn_latent(qa, jnp.concatenate([ka, kac], 1), jnp.concatenate([va, vac], 1), lam)
    ya = (rmsnorm(oa, g_subln) * (1.0 - lam_init)).reshape(B, N, BRANCH_W)
    yb, yb_c = ssm_branch(bz, bx, bdt, bzc, bxc, bdtc, conv_w, conv_b, dt_bias, a_log, d_skip, g_ssm, need_ctx)
    qs = apply_rope(sq.reshape(B, N, SW_HEADS, SW_DIM), cos, sin)
    ks = apply_rope(sk.reshape(B, N, SW_KV_HEADS, SW_DIM), cos, sin)
    vs = sv.reshape(B, N, SW_KV_HEADS, SW_DIM)
    ksc = skc.reshape(B, L, SW_KV_HEADS, SW_DIM)
    vsc = svc.reshape(B, L, SW_KV_HEADS, SW_DIM)
    ys = window_attn_latent(qs, ks, vs, ksc, vsc, sink)
    knc = nkc.reshape(B, L, NA_HEADS, NA_DIM)
    vnc = nvc.reshape(B, L, NA_HEADS, NA_DIM)
    yn = neighbourhood_attn_latent(nq.reshape(B, N, NA_HEADS, NA_DIM), nk.reshape(B, N, NA_HEADS, NA_DIM),
                                   nv.reshape(B, N, NA_HEADS, NA_DIM), knc, vnc, rpb)
    y = merge([ya, yb, ys, yn], gate, w_branch, w_out)
    if not need_ctx:
        return y, None
    ya_c = (rmsnorm(diff_core(aqc.reshape(B, L, DA_HEADS, 2, DA_DIM), kac, vac, lam), g_subln)
            * (1.0 - lam_init)).reshape(B, L, BRANCH_W)
    ys_c = ctx_attend(sqc.reshape(B, L, SW_HEADS, SW_DIM), ksc, vsc, sink)
    yn_c = ctx_attend(nqc.reshape(B, L, NA_HEADS, NA_DIM), knc, vnc)
    y_c = merge([ya_c, yb_c, ys_c, yn_c], gatec, w_branch, w_out)
    return y, y_c


def ffn(t, w1, w2):
    return jnp.square(jax.nn.relu(t @ w1)) @ w2


def setup_inputs(seed: int = 0) -> dict:
    key = jax.random.key(seed)
    ks = iter(jax.random.split(key, 40))
    nrm = lambda shape, s: jax.random.normal(next(ks), shape, jnp.float32) * s
    Lr, D = DEPTH, D_MODEL
    dt0 = jnp.exp(jax.random.uniform(next(ks), (Lr, 2, SSM_HEADS), jnp.float32)
                  * (math.log(0.1) - math.log(0.001)) + math.log(0.001))
    return {
        'x': nrm((BATCH, SEQ, D), 1.0),
        'c': nrm((BATCH, D), 1.0),
        'ctx': nrm((BATCH, CTX_LEN, D), 1.0),
        'c_ctx': nrm((D,), 1.0),
        'w_mod': nrm((Lr, D, 6 * D), 0.5 * D ** -0.5),
        'b_mod': nrm((Lr, 6 * D), 0.02),
        'g_pre_mix': 1.0 + nrm((Lr, D), 0.05),
        'g_post_mix': 1.0 + nrm((Lr, D), 0.05),
        'g_pre_mlp': 1.0 + nrm((Lr, D), 0.05),
        'g_post_mlp': 1.0 + nrm((Lr, D), 0.05),
        'w_in': nrm((Lr, D, IN_COLS), D ** -0.5),
        'lam_q1': nrm((Lr, DA_DIM), 0.1),
        'lam_k1': nrm((Lr, DA_DIM), 0.1),
        'lam_q2': nrm((Lr, DA_DIM), 0.1),
        'lam_k2': nrm((Lr, DA_DIM), 0.1),
        'g_subln': 1.0 + nrm((Lr, 2 * DA_DIM), 0.05),
        'conv_w': nrm((Lr, SSM_CONV, SSM_XBC), SSM_CONV ** -0.5),
        'conv_b': nrm((Lr, SSM_XBC), 0.02),
        'dt_bias': dt0 + jnp.log(-jnp.expm1(-dt0)),
        'a_log': jnp.log(jax.random.uniform(next(ks), (Lr, 2, SSM_HEADS), jnp.float32, 1.0, 16.0)),
        'd_skip': 1.0 + nrm((Lr, SSM_HEADS), 0.1),
        'g_ssm': 1.0 + nrm((Lr, SSM_INNER), 0.05),
        'sink': nrm((Lr, SW_HEADS), 0.5),
        'rpb': nrm((Lr, NA_HEADS, 2 * NA_ROWS - 1, 2 * NA_COLS - 1), 0.1),
        'w_branch': nrm((Lr, N_BRANCH, BRANCH_W, D), BRANCH_W ** -0.5),
        'w_out': nrm((Lr, D, D), D ** -0.5),
        'w_ff1': nrm((Lr, D, D_FF), D ** -0.5),
        'w_ff2': nrm((Lr, D_FF, D), D_FF ** -0.5),
    }


def reference(x, c, ctx, c_ctx, w_mod, b_mod, g_pre_mix, g_post_mix, g_pre_mlp, g_post_mlp, w_in,
              lam_q1, lam_k1, lam_q2, lam_k2, g_subln, conv_w, conv_b, dt_bias, a_log, d_skip, g_ssm,
              sink, rpb, w_branch, w_out, w_ff1, w_ff2):
    B, N, D = x.shape
    cos, sin = axial_rope(N, DA_DIM)
    cx = ctx
    for l in range(DEPTH):
        need_ctx = l < DEPTH - 1
        lam_init = 0.8 - 0.6 * math.exp(-0.3 * l)
        mod = (jax.nn.silu(c) @ w_mod[l] + b_mod[l]).reshape(B, 6, 1, D)
        modc = (jax.nn.silu(c_ctx) @ w_mod[l] + b_mod[l]).reshape(6, D)
        h = modulate(rmsnorm(x, g_pre_mix[l]), mod[:, 0], mod[:, 1])
        hc = modulate(rmsnorm(cx, g_pre_mix[l]), modc[0], modc[1])
        y, y_c = mixer(h, hc, cos, sin, lam_init, need_ctx, w_in[l], lam_q1[l], lam_k1[l], lam_q2[l], lam_k2[l],
                       g_subln[l], conv_w[l], conv_b[l], dt_bias[l], a_log[l], d_skip[l], g_ssm[l], sink[l],
                       rpb[l], w_branch[l], w_out[l])
        x = x + mod[:, 2] * rmsnorm(y, g_post_mix[l])
        h = modulate(rmsnorm(x, g_pre_mlp[l]), mod[:, 3], mod[:, 4])
        x = x + mod[:, 5] * rmsnorm(ffn(h, w_ff1[l], w_ff2[l]), g_post_mlp[l])
        if need_ctx:
            cx = cx + modc[2] * rmsnorm(y_c, g_post_mix[l])
            hc = modulate(rmsnorm(cx, g_pre_mlp[l]), modc[3], modc[4])
            cx = cx + modc[5] * rmsnorm(ffn(hc, w_ff1[l], w_ff2[l]), g_post_mlp[l])
    return x
```

```python
import functools
import math

import numpy as np
import jax
import jax.numpy as jnp
from jax import lax
from jax.experimental import pallas as pl
from jax.experimental.pallas import tpu as pltpu

F32 = jnp.float32
BF16 = jnp.bfloat16
HIGHEST = lax.Precision.HIGHEST

LANES = 128
VMEM_LIMIT = 56 * 1024 * 1024
NEG = -1e30
EPS = 1e-6
GRID_W = 64
ROPE_BASE = 10000.0
HEAD_DIM = 64
QSCALE = HEAD_DIM ** -0.5
DA_HEADS = 4
SSM_HEADS = 8
SSM_INNER = 512
SSM_GROUPS = 2
SSM_STATE = 128
SSM_XBC = SSM_INNER + 2 * SSM_GROUPS * SSM_STATE
SSM_CONV = 5
CHUNK = 128
SW_HEADS = 8
SW_KV = 2
SW_BLOCK = 128
NA_HEADS = 8
NA_ROWS = 8
NA_COLS = 16
N_BRANCH = 4
BRANCH_W = 512

TN = 512
T_AQ, T_AK, T_SQ, T_MISC, T_NQ, T_NK, T_NV, T_AV, T_BZ, T_BX = 0, 1, 2, 3, 4, 5, 6, 7, 8, 9
T_GATE = 11
N_TILES = T_GATE + N_BRANCH * 4


def _cparams(sem):
    return pltpu.CompilerParams(dimension_semantics=sem, vmem_limit_bytes=VMEM_LIMIT)


def _rms(t, g):
    return t * lax.rsqrt(jnp.mean(t * t, -1, keepdims=True) + EPS) * g


def _dot(a, b):
    return jnp.dot(a, b, preferred_element_type=F32)


def _dot_nt(a, b):
    return lax.dot_general(a, b, (((1,), (1,)), ((), ())), preferred_element_type=F32)


def _dot_tn(a, b):
    return lax.dot_general(a, b, (((0,), (0,)), ((), ())), preferred_element_type=F32)


def _half_mask(hh):
    lane = lax.broadcasted_iota(jnp.int32, (1, LANES), 1)
    return (lane // HEAD_DIM) == hh


def _mod_kernel(c_ref, w_ref, b_ref, o_ref):
    c = c_ref[...]
    s = (c * jax.nn.sigmoid(c)).astype(BF16)
    o_ref[0] = _dot(s, w_ref[0].astype(BF16)) + b_ref[0]


def _modulation(cvec, w_mod, b_mod):
    depth, D, C = w_mod.shape
    tn = 1024
    return pl.pallas_call(
        _mod_kernel,
        grid=(depth, C // tn),
        in_specs=[pl.BlockSpec((8, D), lambda l, j: (0, 0)),
                  pl.BlockSpec((1, D, tn), lambda l, j: (l, 0, j)),
                  pl.BlockSpec((1, 1, tn), lambda l, j: (l, 0, j))],
        out_specs=pl.BlockSpec((1, 8, tn), lambda l, j: (l, 0, j)),
        out_shape=jax.ShapeDtypeStruct((depth, 8, C), F32),
        compiler_params=_cparams(("arbitrary", "arbitrary")),
        name="modulation",
    )(cvec, w_mod, b_mod.reshape(depth, 1, C))


def _rope(t, cos, s1, s2):
    w = t.shape[1]
    reps = w // LANES
    tile = lambda a: a if reps == 1 else jnp.concatenate([a] * reps, axis=1)
    return (t * tile(cos) + pltpu.roll(t, w - HEAD_DIM // 2, 1) * tile(s1)
            + pltpu.roll(t, HEAD_DIM // 2, 1) * tile(s2))


def _win_kernel(x_ref, g_ref, sh_ref, sc_ref, cos_ref, s1_ref, s2_ref, w_ref, o_ref, odt_ref, h_scr, *, rope):
    j = pl.program_id(2)

    @pl.when(j == 0)
    def _():
        h = _rms(x_ref[0], g_ref[...]) * (1.0 + sc_ref[0]) + sh_ref[0]
        h_scr[...] = h.astype(BF16)

    acc = _dot(h_scr[...], w_ref[...])
    rot = (lambda t: _rope(t, cos_ref[...], s1_ref[...], s2_ref[...])) if rope else (lambda t: t)

    @pl.when((j == T_AQ) | (j == T_SQ))
    def _():
        o_ref[0] = (rot(acc) * QSCALE).astype(BF16)

    @pl.when(j == T_AK)
    def _():
        o_ref[0] = rot(acc).astype(BF16)

    @pl.when(j == T_MISC)
    def _():
        o_ref[0] = jnp.concatenate([rot(acc[:, :LANES]), acc[:, LANES:]], axis=1).astype(BF16)
        odt_ref[0] = acc[:, 2 * LANES:]

    @pl.when(j == T_NQ)
    def _():
        o_ref[0] = (acc * QSCALE).astype(BF16)

    @pl.when((j > T_NQ) & (j < T_GATE))
    def _():
        o_ref[0] = acc.astype(BF16)

    @pl.when(j >= T_GATE)
    def _():
        o_ref[0] = jax.nn.sigmoid(acc).astype(BF16)


def _in_proj(x, g, shift, scale, rope_tabs, w, *, rope, tm):
    B, T, D = x.shape
    C = w.shape[1]
    tm = min(tm, T)
    cos, s1, s2 = rope_tabs
    tab_spec = pl.BlockSpec((tm, LANES), lambda b, i, j: (i, 0))
    mod_spec = pl.BlockSpec((1, 1, D), lambda b, i, j: (b, 0, 0))
    return pl.pallas_call(
        functools.partial(_win_kernel, rope=rope),
        grid=(B, T // tm, C // TN),
        in_specs=[pl.BlockSpec((1, tm, D), lambda b, i, j: (b, i, 0)),
                  pl.BlockSpec((1, D), lambda b, i, j: (0, 0)),
                  mod_spec, mod_spec, tab_spec, tab_spec, tab_spec,
                  pl.BlockSpec((D, TN), lambda b, i, j: (0, j))],
        out_specs=[pl.BlockSpec((1, tm, TN), lambda b, i, j: (b, i, j)),
                   pl.BlockSpec((1, tm, 2 * LANES), lambda b, i, j: (b, i, 0))],
        out_shape=[jax.ShapeDtypeStruct((B, T, C), BF16),
                   jax.ShapeDtypeStruct((B, T, 2 * LANES), F32)],
        scratch_shapes=[pltpu.VMEM((tm, D), BF16)],
        compiler_params=_cparams(("arbitrary", "arbitrary", "arbitrary")),
        name="in_proj_rope" if rope else "in_proj_ctx",
    )(x, g, shift, scale, cos, s1, s2, w)


def _diff_attn_kernel(lam_ref, g_ref, q_ref, k_ref, v_ref, o_ref, m_scr, l_scr, acc_scr, *, lam_init):
    kk = pl.program_id(3)

    @pl.when(kk == 0)
    def _():
        m_scr[...] = jnp.full(m_scr.shape, -jnp.inf, F32)
        l_scr[...] = jnp.zeros(l_scr.shape, F32)
        acc_scr[...] = jnp.zeros(acc_scr.shape, F32)

    q = q_ref[0]
    k = k_ref[0]
    v = v_ref[0]
    reps = k.shape[0] // LANES
    for m in range(2):
        s = _dot_nt(jnp.where(_half_mask(m), q, jnp.zeros_like(q)), k)
        m_prev = m_scr[m]
        m_next = jnp.maximum(m_prev, jnp.max(s, -1, keepdims=True))
        alpha = jnp.exp(m_prev - m_next)
        p = jnp.exp(s - jnp.concatenate([m_next] * reps, axis=1))
        l_scr[m] = alpha * l_scr[m] + jnp.sum(p, -1, keepdims=True)
        acc_scr[m] = alpha * acc_scr[m] + _dot(p.astype(BF16), v)
        m_scr[m] = m_next

    @pl.when(kk == pl.num_programs(3) - 1)
    def _():
        lv = lam_ref[...]
        lam = (jnp.exp(jnp.sum(lv[0:1] * lv[1:2], -1, keepdims=True))
               - jnp.exp(jnp.sum(lv[2:3] * lv[3:4], -1, keepdims=True)) + lam_init)
        o = acc_scr[0] / l_scr[0] - lam * (acc_scr[1] / l_scr[1])
        o_ref[0] = (_rms(o, g_ref[...]) * (1.0 - lam_init)).astype(BF16)


def _largest_divisor(n, cands):
    for c in cands:
        if n % c == 0:
            return c
    raise ValueError(f"no tile in {cands} divides {n}")


def _diff_attn(q_arr, q_blk0, k_all, v_all, lamv, g_subln, lam_init, *, tq):
    B, Tq = q_arr.shape[:2]
    Tk = k_all.shape[1]
    tq = min(tq, Tq)
    tk = _largest_divisor(Tk, (768, 640, 512, 384, 256, 128))
    return pl.pallas_call(
        functools.partial(_diff_attn_kernel, lam_init=lam_init),
        grid=(B, DA_HEADS, Tq // tq, Tk // tk),
        in_specs=[pl.BlockSpec((4, HEAD_DIM), lambda b, h, i, kk: (0, 0)),
                  pl.BlockSpec((1, LANES), lambda b, h, i, kk: (0, 0)),
                  pl.BlockSpec((1, tq, LANES), lambda b, h, i, kk: (b, i, q_blk0 + h)),
                  pl.BlockSpec((1, tk, LANES), lambda b, h, i, kk: (b, kk, h)),
                  pl.BlockSpec((1, tk, LANES), lambda b, h, i, kk: (b, kk, h))],
        out_specs=pl.BlockSpec((1, tq, LANES), lambda b, h, i, kk: (b, i, h)),
        out_shape=jax.ShapeDtypeStruct((B, Tq, DA_HEADS * LANES), BF16),
        scratch_shapes=[pltpu.VMEM((2, tq, LANES), F32), pltpu.VMEM((2, tq, LANES), F32),
                        pltpu.VMEM((2, tq, LANES), F32)],
        compiler_params=_cparams(("arbitrary",) * 4),
        name="diff_attn",
    )(lamv, g_subln, q_arr, k_all, v_all)


def _swa_kernel(sink_ref, q_ref, kp_ref, kc_ref, kn_ref, vp_ref, vc_ref, vn_ref, kx_ref, vx_ref, o_ref):
    i = pl.program_id(1)
    nb = pl.num_programs(1)
    blk = SW_BLOCK
    kband = jnp.concatenate([kp_ref[0], kc_ref[0], kn_ref[0], kx_ref[0]], axis=0)
    vband = jnp.concatenate([vp_ref[0], vc_ref[0], vn_ref[0], vx_ref[0]], axis=0)
    nkeys = kband.shape[0]
    row = lax.broadcasted_iota(jnp.int32, (blk, nkeys), 0)
    col = lax.broadcasted_iota(jnp.int32, (blk, nkeys), 1)
    big = 4 * blk
    off_prev = jnp.where(i > 0, 0, big)
    off_next = jnp.where(i < nb - 1, 0, big)
    ok_prev = jnp.where(col >= row + off_prev, 0.0, NEG)
    ok_next = jnp.where(col - 2 * blk <= row - off_next, 0.0, NEG)
    bias = jnp.where(col < blk, ok_prev, jnp.where((col >= 2 * blk) & (col < 3 * blk), ok_next, 0.0))
    bias = jnp.concatenate([bias] * 4, axis=0)
    q = q_ref[0]
    outs = [None] * 4
    for kv in range(SW_KV):
        hm = _half_mask(kv)
        qs = jnp.concatenate([jnp.where(hm, q[:, g * LANES:(g + 1) * LANES], jnp.zeros((blk, LANES), BF16))
                              for g in range(4)], axis=0)
        s = _dot_nt(qs, kband) + bias
        sink = jnp.concatenate([jnp.full((blk, 1), sink_ref[kv * 4 + g], F32) for g in range(4)], axis=0)
        m = jnp.maximum(jnp.max(s, -1, keepdims=True), sink)
        p = jnp.exp(s - m)
        denom = jnp.sum(p, -1, keepdims=True) + jnp.exp(sink - m)
        o = _dot(p.astype(BF16), jnp.where(hm, vband, jnp.zeros_like(vband))) / denom
        for g in range(4):
            part = o[g * blk:(g + 1) * blk]
            outs[g] = part if outs[g] is None else outs[g] + part
    o_ref[0] = jnp.concatenate(outs, axis=1).astype(BF16)


def _swa(p_lat, p_ctx, sink):
    B, N = p_lat.shape[:2]
    L = p_ctx.shape[1]
    nb = N // SW_BLOCK
    kcol, vcol = T_MISC * 4, T_MISC * 4 + 1
    kv_spec = lambda colblk, off: pl.BlockSpec(
        (1, SW_BLOCK, LANES), lambda b, i: (b, jnp.clip(i + off, 0, nb - 1), colblk))
    return pl.pallas_call(
        _swa_kernel,
        grid=(B, nb),
        in_specs=[pl.BlockSpec(memory_space=pltpu.SMEM),
                  pl.BlockSpec((1, SW_BLOCK, TN), lambda b, i: (b, i, T_SQ)),
                  kv_spec(kcol, -1), kv_spec(kcol, 0), kv_spec(kcol, 1),
                  kv_spec(vcol, -1), kv_spec(vcol, 0), kv_spec(vcol, 1),
                  pl.BlockSpec((1, L, LANES), lambda b, i: (b, 0, kcol)),
                  pl.BlockSpec((1, L, LANES), lambda b, i: (b, 0, vcol))],
        out_specs=pl.BlockSpec((1, SW_BLOCK, TN), lambda b, i: (b, i, 0)),
        out_shape=jax.ShapeDtypeStruct((B, N, TN), BF16),
        compiler_params=_cparams(("arbitrary", "arbitrary")),
        name="window_attn",
    )(sink, p_lat, p_lat, p_lat, p_lat, p_lat, p_lat, p_lat, p_ctx, p_ctx)


NA_QROWS = 2
NA_KBLKS = 5


def _na_kernel(bias_ref, q_ref, k0, k1, k2, k3, k4, v0, v1, v2, v3, v4, kx_ref, vx_ref, o_ref):
    kall = jnp.concatenate([k0[0], k1[0], k2[0], k3[0], k4[0], kx_ref[0]], axis=0)
    vall = jnp.concatenate([v0[0], v1[0], v2[0], v3[0], v4[0], vx_ref[0]], axis=0)
    q = q_ref[0]
    outs = []
    for pb in range(4):
        sl = slice(pb * LANES, (pb + 1) * LANES)
        qb, kb, vb = q[:, sl], kall[:, sl], vall[:, sl]
        ob = None
        for hh in range(2):
            hm = _half_mask(hh)
            s = _dot_nt(jnp.where(hm, qb, jnp.zeros_like(qb)), kb) + bias_ref[0, pb * 2 + hh]
            m = jnp.max(s, -1, keepdims=True)
            p = jnp.exp(s - m)
            o = _dot(p.astype(BF16), jnp.where(hm, vb, jnp.zeros_like(vb))) / jnp.sum(p, -1, keepdims=True)
            ob = o if ob is None else ob + o
        outs.append(ob)
    o_ref[0] = jnp.concatenate(outs, axis=1).astype(BF16)


def _na_bias_index(R, L):
    W = GRID_W
    nblk = R // NA_QROWS
    nq, nk = NA_QROWS * W, NA_KBLKS * NA_QROWS * W
    qq = np.arange(nq)
    kx = np.arange(nk)
    t, c = qq // W, qq % W
    s, kc = kx // W, kx % W
    cb = np.clip(c - NA_COLS // 2, 0, W - NA_COLS)
    ro_all, co_all, ok_all = [], [], []
    for i in (0, 1, 2, nblk - 2, nblk - 1):
        base_blk = np.clip(i - 2, 0, nblk - NA_KBLKS)
        rq = NA_QROWS * i + t
        rk = NA_QROWS * base_blk + s
        base = np.clip(rq - NA_ROWS // 2, 0, R - NA_ROWS)
        dr = rk[None, :] - base[:, None]
        dc = kc[None, :] - cb[:, None]
        ok = (dr >= 0) & (dr < NA_ROWS) & (dc >= 0) & (dc < NA_COLS)
        ro = rk[None, :] - rq[:, None] + NA_ROWS - 1
        co = kc[None, :] - c[:, None] + NA_COLS - 1
        ro_all.append(np.where(ok, ro, 0))
        co_all.append(np.where(ok, co, 0))
        ok_all.append(ok)
    return np.stack(ro_all), np.stack(co_all), np.stack(ok_all)


def _na_bias(rpb, R, L):
    ro, co, ok = _na_bias_index(R, L)
    b = jnp.where(ok[:, None], rpb.astype(F32)[:, ro, co].transpose(1, 0, 2, 3), NEG)
    return jnp.concatenate([b, jnp.zeros(b.shape[:3] + (L,), F32)], axis=-1)


def _na(p_lat, p_ctx, bias):
    B, N = p_lat.shape[:2]
    L = p_ctx.shape[1]
    rows = NA_QROWS * GRID_W
    nblk = N // rows
    assert nblk >= NA_KBLKS + 2
    cls = lambda i: jnp.where(i < 2, i, jnp.where(i >= nblk - 2, i - (nblk - 5), 2))
    kv_spec = lambda tile, s: pl.BlockSpec(
        (1, rows, TN), lambda b, i: (b, jnp.clip(i - 2, 0, nblk - NA_KBLKS) + s, tile))
    nkeys = NA_KBLKS * rows + L
    return pl.pallas_call(
        _na_kernel,
        grid=(B, nblk),
        in_specs=[pl.BlockSpec((1, NA_HEADS, rows, nkeys), lambda b, i: (cls(i), 0, 0, 0)),
                  pl.BlockSpec((1, rows, TN), lambda b, i: (b, i, T_NQ))]
                 + [kv_spec(T_NK, s) for s in range(NA_KBLKS)]
                 + [kv_spec(T_NV, s) for s in range(NA_KBLKS)]
                 + [pl.BlockSpec((1, L, TN), lambda b, i: (b, 0, T_NK)),
                    pl.BlockSpec((1, L, TN), lambda b, i: (b, 0, T_NV))],
        out_specs=pl.BlockSpec((1, rows, TN), lambda b, i: (b, i, 0)),
        out_shape=jax.ShapeDtypeStruct((B, N, TN), BF16),
        compiler_params=_cparams(("arbitrary", "arbitrary")),
        name="neighbourhood_attn",
    )(bias, p_lat, *([p_lat] * (2 * NA_KBLKS)), p_ctx, p_ctx)


def _ctx_attend_kernel(sink_ref, q_ref, k_ref, v_ref, o_ref, *, gqa):
    q = q_ref[0]
    k = k_ref[0]
    v = v_ref[0]
    outs = []
    for blk in range(4):
        sl = slice(blk * LANES, (blk + 1) * LANES)
        qb = q[:, sl]
        kb, vb = (k, v) if gqa else (k[:, sl], v[:, sl])
        ob = None
        for hh in range(2):
            hm = _half_mask(hh)
            head = hh * 4 + blk if gqa else blk * 2 + hh
            s = _dot_nt(jnp.where(hm, qb, jnp.zeros_like(qb)), kb)
            sink = sink_ref[head]
            m = jnp.maximum(jnp.max(s, -1, keepdims=True), sink)
            p = jnp.exp(s - m)
            denom = jnp.sum(p, -1, keepdims=True) + jnp.exp(sink - m)
            o = _dot(p.astype(BF16), jnp.where(hm, vb, jnp.zeros_like(vb))) / denom
            ob = o if ob is None else ob + o
        outs.append(ob)
    o_ref[0] = jnp.concatenate(outs, axis=1).astype(BF16)


def _ctx_attend(p_ctx, sink, q_tile, k_blk, v_blk, *, gqa):
    B, L = p_ctx.shape[:2]
    kw = LANES if gqa else TN
    return pl.pallas_call(
        functools.partial(_ctx_attend_kernel, gqa=gqa),
        grid=(B,),
        in_specs=[pl.BlockSpec(memory_space=pltpu.SMEM),
                  pl.BlockSpec((1, L, TN), lambda b: (b, 0, q_tile)),
                  pl.BlockSpec((1, L, kw), lambda b: (b, 0, k_blk)),
                  pl.BlockSpec((1, L, kw), lambda b: (b, 0, v_blk))],
        out_specs=pl.BlockSpec((1, L, TN), lambda b: (b, 0, 0)),
        out_shape=jax.ShapeDtypeStruct((B, L, TN), BF16),
        compiler_params=_cparams(("arbitrary",)),
        name="ctx_attend_gqa" if gqa else "ctx_attend",
    )(sink, p_ctx, p_ctx, p_ctx)


CONV_BLK = 256
HALO = 16


def _conv_kernel(prev_ref, cur_ref, next_ref, w_ref, b_ref, o_ref, *, ctx_blocks):
    i = pl.program_id(1)
    last = pl.num_programs(1) - 1
    has_prev = (i != 0) & (i != ctx_blocks)
    has_next = (i != ctx_blocks - 1) & (i != last)
    prev = prev_ref[0].astype(F32) * has_prev.astype(F32)
    nxt = next_ref[0].astype(F32) * has_next.astype(F32)
    ext = jnp.concatenate([prev, cur_ref[0].astype(F32), nxt], axis=0)
    n = ext.shape[0]
    w = w_ref[...]
    acc = None
    for k in range(SSM_CONV):
        shift = (SSM_CONV // 2 - k) % n
        t = (ext if shift == 0 else pltpu.roll(ext, shift, 0))[HALO:HALO + CONV_BLK] * w[k:k + 1]
        acc = t if acc is None else acc + t
    acc = acc + b_ref[...]
    o_ref[0] = (acc * jax.nn.sigmoid(acc)).astype(BF16)


def _conv_silu(xbc, conv_w, conv_b, L):
    B, T, C = xbc.shape
    assert L % CONV_BLK == 0 and T % CONV_BLK == 0
    per = CONV_BLK // HALO
    nh = T // HALO
    w = jnp.zeros((8, C), F32).at[:SSM_CONV].set(conv_w.astype(F32))
    return pl.pallas_call(
        functools.partial(_conv_kernel, ctx_blocks=L // CONV_BLK),
        grid=(B, T // CONV_BLK),
        in_specs=[pl.BlockSpec((1, HALO, C), lambda b, i: (b, jnp.maximum(i * per - 1, 0), 0)),
                  pl.BlockSpec((1, CONV_BLK, C), lambda b, i: (b, i, 0)),
                  pl.BlockSpec((1, HALO, C), lambda b, i: (b, jnp.minimum((i + 1) * per, nh - 1), 0)),
                  pl.BlockSpec((8, C), lambda b, i: (0, 0)),
                  pl.BlockSpec((1, C), lambda b, i: (0, 0))],
        out_specs=pl.BlockSpec((1, CONV_BLK, C), lambda b, i: (b, i, 0)),
        out_shape=jax.ShapeDtypeStruct((B, T, C), BF16),
        compiler_params=_cparams(("arbitrary", "arbitrary")),
        name="ssm_conv",
    )(xbc, xbc, xbc, w, conv_b.reshape(1, C).astype(F32))


def _ssd_kernel(u_ref, dtr_ref, dtb_ref, alog_ref, tri_ref, exp_ref, y_ref, h_scr, *, fwd):
    @pl.when(pl.program_id(1) == 0)
    def _():
        h_scr[...] = jnp.zeros(h_scr.shape, F32)

    last = CHUNK - 1 if fwd else 0
    u = u_ref[0]
    xs = u[:, :SSM_INNER].astype(F32)
    bm = u[:, SSM_INNER:SSM_INNER + SSM_GROUPS * SSM_STATE]
    cm = u[:, SSM_INNER + SSM_GROUPS * SSM_STATE:]
    dt = jax.nn.softplus(dtr_ref[0] + dtb_ref[...])
    a = dt * (-jnp.exp(alog_ref[...]))
    acum = jnp.dot(tri_ref[...], a, precision=HIGHEST, preferred_element_type=F32)
    eac = jnp.exp(acum)
    dst = jnp.exp(acum[last:last + 1] - acum)
    expand = lambda t: jnp.dot(t, exp_ref[...], precision=HIGHEST, preferred_element_type=F32)
    dt_e, eac_e, dst_e = expand(dt), expand(eac), expand(dst)
    xg = xs * dt_e
    xg_b = xg.astype(BF16)
    xgd_b = (xg * dst_e).astype(BF16)
    acum_t = acum.T
    row = lax.broadcasted_iota(jnp.int32, (CHUNK, CHUNK), 0)
    col = lax.broadcasted_iota(jnp.int32, (CHUNK, CHUNK), 1)
    keep = (row >= col) if fwd else (row <= col)
    state = h_scr[...]
    state_b = state.astype(BF16)
    gw = SSM_INNER // SSM_GROUPS
    ys, st_new = [], []
    for g in range(SSM_GROUPS):
        bmg = bm[:, g * SSM_STATE:(g + 1) * SSM_STATE]
        cmg = cm[:, g * SSM_STATE:(g + 1) * SSM_STATE]
        cb = _dot_nt(cmg, bmg)
        yoff = _dot(cmg, state_b[:, g * gw:(g + 1) * gw])
        st_new.append(_dot_tn(bmg, xgd_b[:, g * gw:(g + 1) * gw]))
        for pb in range(gw // LANES):
            blk = g * (gw // LANES) + pb
            sl = slice(blk * LANES, (blk + 1) * LANES)
            xgb = xg_b[:, sl]
            yd = None
            for hh in range(2):
                h = blk * 2 + hh
                seg = acum[:, h:h + 1] - acum_t[h:h + 1, :]
                ld = jnp.exp(jnp.where(keep, seg, -jnp.inf))
                t = _dot((cb * ld).astype(BF16), jnp.where(_half_mask(hh), xgb, jnp.zeros_like(xgb)))
                yd = t if yd is None else yd + t
            ys.append(yd + yoff[:, pb * LANES:(pb + 1) * LANES] * eac_e[:, sl])
    y_ref[0] = jnp.concatenate(ys, axis=1)
    h_scr[...] = state * eac_e[last:last + 1] + jnp.concatenate(st_new, axis=1)


def _ssd(u, dtr, dt_bias, a_log, L, *, fwd):
    B, T, _ = u.shape
    nc, ncx = T // CHUNK, L // CHUNK
    d = 0 if fwd else 1
    chunk = (lambda c: c) if fwd else (lambda c: jnp.where(c < ncx, ncx - 1 - c, nc + ncx - 1 - c))
    pad = lambda t: jnp.zeros((1, LANES), F32).at[0, :SSM_HEADS].set(t[d].astype(F32))
    li = np.arange(CHUNK)
    tri = (li[:, None] >= li[None, :]) if fwd else (li[:, None] <= li[None, :])
    expm = np.zeros((LANES, SSM_INNER), np.float32)
    for h in range(SSM_HEADS):
        expm[h, h * 64:(h + 1) * 64] = 1.0
    return pl.pallas_call(
        functools.partial(_ssd_kernel, fwd=fwd),
        grid=(B, nc),
        in_specs=[pl.BlockSpec((1, CHUNK, SSM_XBC), lambda b, c: (b, chunk(c), 0)),
                  pl.BlockSpec((1, CHUNK, LANES), lambda b, c: (b, chunk(c), d)),
                  pl.BlockSpec((1, LANES), lambda b, c: (0, 0)),
                  pl.BlockSpec((1, LANES), lambda b, c: (0, 0)),
                  pl.BlockSpec((CHUNK, CHUNK), lambda b, c: (0, 0)),
                  pl.BlockSpec((LANES, SSM_INNER), lambda b, c: (0, 0))],
        out_specs=pl.BlockSpec((1, CHUNK, SSM_INNER), lambda b, c: (b, chunk(c), 0)),
        out_shape=jax.ShapeDtypeStruct((B, T, SSM_INNER), F32),
        scratch_shapes=[pltpu.VMEM((SSM_STATE, SSM_INNER), F32)],
        compiler_params=_cparams(("arbitrary", "arbitrary")),
        name="ssd_scan_fwd" if fwd else "ssd_scan_bwd",
    )(u, dtr, pad(dt_bias), pad(a_log), jnp.asarray(tri.astype(np.float32)), jnp.asarray(expm))


def _ssm_finish_kernel(yf_ref, yb_ref, u_ref, z_ref, dsk_ref, g_ref, o_ref):
    y = yf_ref[0] + yb_ref[0] + dsk_ref[...] * u_ref[0].astype(F32)
    z = z_ref[0].astype(F32)
    o_ref[0] = _rms(y * (z * jax.nn.sigmoid(z)), g_ref[...]).astype(BF16)


def _ssm_finish(y_f, y_b, u, z, d_skip, g_ssm, *, tm=256):
    B, T, W = y_f.shape
    dsk = jnp.repeat(d_skip.astype(F32), W // SSM_HEADS).reshape(1, W)
    return pl.pallas_call(
        _ssm_finish_kernel,
        grid=(B, T // tm),
        in_specs=[pl.BlockSpec((1, tm, W), lambda b, i: (b, i, 0))] * 4 + [
                  pl.BlockSpec((1, W), lambda b, i: (0, 0)),
                  pl.BlockSpec((1, W), lambda b, i: (0, 0))],
        out_specs=pl.BlockSpec((1, tm, W), lambda b, i: (b, i, 0)),
        out_shape=jax.ShapeDtypeStruct((B, T, W), BF16),
        compiler_params=_cparams(("arbitrary", "arbitrary")),
        name="ssm_finish",
    )(y_f, y_b, u, z, dsk, g_ssm.reshape(1, W).astype(F32))


def _merge_kernel(y0, y1, y2, y3, s0, s1, s2, s3, wb_ref, wo_ref, x_ref, gate_ref, g_ref, o_ref, acc):
    j = pl.program_id(2)
    m = None
    for k, (y, s) in enumerate(((y0, s0), (y1, s1), (y2, s2), (y3, s3))):
        t = _dot(y[0], wb_ref[k]) * s[0].astype(F32)
        m = t if m is None else m + t
    contrib = _dot(m.astype(BF16), wo_ref[...])

    @pl.when(j == 0)
    def _():
        acc[...] = contrib

    @pl.when(j > 0)
    def _():
        acc[...] += contrib

    @pl.when(j == pl.num_programs(2) - 1)
    def _():
        o_ref[0] = x_ref[0] + gate_ref[0] * _rms(acc[...], g_ref[...])


def _merge(ys, p, w_branch, w_out, x, gate, g_post, *, tm):
    B, T, D = x.shape
    tm = min(tm, T)
    nj = D // TN
    y_spec = pl.BlockSpec((1, tm, BRANCH_W), lambda b, i, j: (b, i, 0))
    s_spec = lambda k: pl.BlockSpec((1, tm, TN), lambda b, i, j: (b, i, T_GATE + k * nj + j))
    return pl.pallas_call(
        _merge_kernel,
        grid=(B, T // tm, nj),
        in_specs=[y_spec] * 4 + [s_spec(k) for k in range(N_BRANCH)]
                 + [pl.BlockSpec((N_BRANCH, BRANCH_W, TN), lambda b, i, j: (0, 0, j)),
                    pl.BlockSpec((TN, D), lambda b, i, j: (j, 0)),
                    pl.BlockSpec((1, tm, D), lambda b, i, j: (b, i, 0)),
                    pl.BlockSpec((1, 1, D), lambda b, i, j: (b, 0, 0)),
                    pl.BlockSpec((1, D), lambda b, i, j: (0, 0))],
        out_specs=pl.BlockSpec((1, tm, D), lambda b, i, j: (b, i, 0)),
        out_shape=jax.ShapeDtypeStruct((B, T, D), F32),
        scratch_shapes=[pltpu.VMEM((tm, D), F32)],
        compiler_params=_cparams(("arbitrary", "arbitrary", "arbitrary")),
        name="merge_out_proj",
    )(*ys, p, p, p, p, w_branch, w_out, x, gate, g_post)


def _ffn_kernel(x_ref, g1_ref, sh_ref, sc_ref, gate_ref, g2_ref, w1_ref, w2_ref, o_ref, h_scr, acc):
    f = pl.program_id(2)

    @pl.when(f == 0)
    def _():
        h = _rms(x_ref[0], g1_ref[...]) * (1.0 + sc_ref[0]) + sh_ref[0]
        h_scr[...] = h.astype(BF16)

    a = jnp.square(jnp.maximum(_dot(h_scr[...], w1_ref[...]), 0.0)).astype(BF16)
    contrib = _dot(a, w2_ref[...])

    @pl.when(f == 0)
    def _():
        acc[...] = contrib

    @pl.when(f > 0)
    def _():
        acc[...] += contrib

    @pl.when(f == pl.num_programs(2) - 1)
    def _():
        o_ref[0] = x_ref[0] + gate_ref[0] * _rms(acc[...], g2_ref[...])


def _ffn(x, g_pre, shift, scale, gate, g_post, w1, w2, *, tm, tf=512):
    B, T, D = x.shape
    F = w1.shape[1]
    tm = min(tm, T)
    mod_spec = pl.BlockSpec((1, 1, D), lambda b, i, f: (b, 0, 0))
    vec_spec = pl.BlockSpec((1, D), lambda b, i, f: (0, 0))
    return pl.pallas_call(
        _ffn_kernel,
        grid=(B, T // tm, F // tf),
        in_specs=[pl.BlockSpec((1, tm, D), lambda b, i, f: (b, i, 0)),
                  vec_spec, mod_spec, mod_spec, mod_spec, vec_spec,
                  pl.BlockSpec((D, tf), lambda b, i, f: (0, f)),
                  pl.BlockSpec((tf, D), lambda b, i, f: (f, 0))],
        out_specs=pl.BlockSpec((1, tm, D), lambda b, i, f: (b, i, 0)),
        out_shape=jax.ShapeDtypeStruct((B, T, D), F32),
        scratch_shapes=[pltpu.VMEM((tm, D), BF16), pltpu.VMEM((tm, D), F32)],
        compiler_params=_cparams(("arbitrary", "arbitrary", "arbitrary")),
        name="mlp",
    )(x, g_pre, shift, scale, gate, g_post, w1, w2)


def _layout_w_in(w):
    D = w.shape[0]
    o = 0
    parts = {}
    for name, width in (("aq", 512), ("ak", 512), ("av", 512), ("bz", 512), ("bx", SSM_XBC), ("bdt", 16),
                        ("sq", 512), ("sk", 128), ("sv", 128), ("nq", 512), ("nk", 512), ("nv", 512),
                        ("gate", N_BRANCH * D)):
        parts[name] = w[:, o:o + width]
        o += width
    assert o == w.shape[1]
    sq = parts["sq"].reshape(D, SW_KV, 4, HEAD_DIM).transpose(0, 2, 1, 3).reshape(D, 512)
    zpad = jnp.zeros((D, LANES - SSM_HEADS), w.dtype)
    misc = jnp.concatenate([parts["sk"], parts["sv"], parts["bdt"][:, :SSM_HEADS], zpad,
                            parts["bdt"][:, SSM_HEADS:], zpad], axis=1)
    cols = [parts["aq"], parts["ak"], sq, misc, parts["nq"], parts["nk"], parts["nv"], parts["av"],
            parts["bz"], parts["bx"], parts["gate"]]
    out = jnp.concatenate(cols, axis=1).astype(BF16)
    assert out.shape[1] == N_TILES * TN
    return out


def _rope_tables(n):
    t = np.arange(n)
    nf = HEAD_DIM // 4
    inv = ROPE_BASE ** (-np.arange(nf, dtype=np.float32) / nf)
    row = (t // GRID_W).astype(np.float32)
    col = (t % GRID_W).astype(np.float32)
    ang = jnp.asarray(np.concatenate([row[:, None] * inv, col[:, None] * inv], -1).astype(np.float32))
    cos, sin = jnp.cos(ang), jnp.sin(ang)
    z = jnp.zeros_like(sin)
    rep = LANES // HEAD_DIM
    cos_t = jnp.tile(jnp.concatenate([cos, cos], -1), (1, rep))
    s1 = jnp.tile(jnp.concatenate([-sin, z], -1), (1, rep))
    s2 = jnp.tile(jnp.concatenate([z, sin], -1), (1, rep))
    return cos_t, s1, s2


def kernel(x, c, ctx, c_ctx, w_mod, b_mod, g_pre_mix, g_post_mix, g_pre_mlp, g_post_mlp, w_in, lam_q1, lam_k1, lam_q2, lam_k2, g_subln, conv_w, conv_b, dt_bias, a_log, d_skip, g_ssm, sink, rpb, w_branch, w_out, w_ff1, w_ff2):
    B, N, D = x.shape
    L = ctx.shape[1]
    depth = w_mod.shape[0]
    assert B + 1 <= 8 and N % GRID_W == 0
    R = N // GRID_W
    tm_lat, tm_ctx = 1024, 256

    cvec = jnp.zeros((8, D), F32).at[:B].set(c).at[B].set(c_ctx)
    mods = _modulation(cvec, w_mod, b_mod)
    rope_tabs = _rope_tables(N)
    ctx_tabs = tuple(t[:L] for t in rope_tabs)
    no_sink = jnp.full((NA_HEADS,), NEG, F32)
    row = lambda v: v.reshape(1, -1).astype(F32)

    cx = ctx
    for l in range(depth):
        need_ctx = l < depth - 1
        lam_init = 0.8 - 0.6 * math.exp(-0.3 * l)
        mod = mods[l, :B].reshape(B, 6, 1, D)
        modc = jnp.broadcast_to(mods[l, B].reshape(1, 6, 1, D), (B, 6, 1, D))
        w_in_l = _layout_w_in(w_in[l])
        p_lat, dt_lat = _in_proj(x, row(g_pre_mix[l]), mod[:, 0], mod[:, 1], rope_tabs, w_in_l,
                                 rope=True, tm=tm_lat)
        p_ctx, dt_ctx = _in_proj(cx, row(g_pre_mix[l]), modc[:, 0], modc[:, 1], ctx_tabs, w_in_l,
                                 rope=False, tm=tm_ctx)
        tile = lambda p, t, n=1: p[:, :, t * TN:(t + n) * TN]

        lamv = jnp.stack([lam_q1[l], lam_k1[l], lam_q2[l], lam_k2[l]]).astype(F32)
        k_all = jnp.concatenate([tile(p_ctx, T_AK), tile(p_lat, T_AK)], axis=1)
        v_all = jnp.concatenate([tile(p_ctx, T_AV), tile(p_lat, T_AV)], axis=1)
        ya = _diff_attn(p_lat, T_AQ * 4, k_all, v_all, lamv, row(g_subln[l]), lam_init, tq=512)

        xbc = jnp.concatenate([tile(p_ctx, T_BX, 2), tile(p_lat, T_BX, 2)], axis=1)
        z = jnp.concatenate([tile(p_ctx, T_BZ), tile(p_lat, T_BZ)], axis=1)
        dtr = jnp.concatenate([dt_ctx, dt_lat], axis=1)
        u = _conv_silu(xbc, conv_w[l], conv_b[l], L)
        y_f = _ssd(u, dtr, dt_bias[l], a_log[l], L, fwd=True)
        y_b = _ssd(u, dtr, dt_bias[l], a_log[l], L, fwd=False)
        yb_all = _ssm_finish(y_f, y_b, u, z, d_skip[l], g_ssm[l])
        yb, yb_c = yb_all[:, L:], yb_all[:, :L]

        ys = _swa(p_lat, p_ctx, sink[l].astype(F32))
        yn = _na(p_lat, p_ctx, _na_bias(rpb[l], R, L))

        wb = w_branch[l]
        wb = wb.at[2].set(wb[2].reshape(SW_KV, 4, HEAD_DIM, D).transpose(1, 0, 2, 3).reshape(BRANCH_W, D))
        wb = wb.astype(BF16)
        wo = w_out[l].astype(BF16)
        w1 = w_ff1[l].astype(BF16)
        w2 = w_ff2[l].astype(BF16)
        x = _merge([ya, yb, ys, yn], p_lat, wb, wo, x, mod[:, 2], row(g_post_mix[l]), tm=512)
        x = _ffn(x, row(g_pre_mlp[l]), mod[:, 3], mod[:, 4], mod[:, 5], row(g_post_mlp[l]), w1, w2, tm=512)
        if need_ctx:
            ya_c = _diff_attn(p_ctx, T_AQ * 4, tile(p_ctx, T_AK), tile(p_ctx, T_AV), lamv, row(g_subln[l]),
                              lam_init, tq=L)
            ys_c = _ctx_attend(p_ctx, sink[l].astype(F32), T_SQ, T_MISC * 4, T_MISC * 4 + 1, gqa=True)
            yn_c = _ctx_attend(p_ctx, no_sink, T_NQ, T_NK, T_NV, gqa=False)
            cx = _merge([ya_c, yb_c, ys_c, yn_c], p_ctx, wb, wo, cx, modc[:, 2], row(g_post_mix[l]), tm=tm_ctx)
            cx = _ffn(cx, row(g_pre_mlp[l]), modc[:, 3], modc[:, 4], modc[:, 5], row(g_post_mlp[l]), w1, w2,
                      tm=tm_ctx)
    return x
```

```python
import functools
import math

import numpy as np
import jax
import jax.numpy as jnp
from jax import lax
from jax.experimental import pallas as pl
from jax.experimental.pallas import tpu as pltpu

F32 = jnp.float32
BF16 = jnp.bfloat16
HIGHEST = lax.Precision.HIGHEST

LANES = 128
VMEM_LIMIT = 56 * 1024 * 1024
NEG = -1e30
EPS = 1e-6
GRID_W = 64
ROPE_BASE = 10000.0
HEAD_DIM = 64
QSCALE = HEAD_DIM ** -0.5
DA_HEADS = 4
SSM_HEADS = 8
SSM_INNER = 512
SSM_GROUPS = 2
SSM_STATE = 128
SSM_XBC = SSM_INNER + 2 * SSM_GROUPS * SSM_STATE
SSM_CONV = 5
CHUNK = 128
SW_HEADS = 8
SW_KV = 2
SW_BLOCK = 128
NA_HEADS = 8
NA_ROWS = 8
NA_COLS = 16
N_BRANCH = 4
BRANCH_W = 512

TN = 512
T_AQ, T_AK, T_SQ, T_MISC, T_NQ, T_NK, T_NV, T_AV, T_BZ, T_BX = 0, 1, 2, 3, 4, 5, 6, 7, 8, 9
T_GATE = 11
N_TILES = T_GATE + N_BRANCH * 4


def _cparams(sem):
    return pltpu.CompilerParams(dimension_semantics=sem, vmem_limit_bytes=VMEM_LIMIT)


def _rms(t, g):
    return t * lax.rsqrt(jnp.mean(t * t, -1, keepdims=True) + EPS) * g


def _dot(a, b):
    return jnp.dot(a, b, preferred_element_type=F32)


def _dot_nt(a, b):
    return lax.dot_general(a, b, (((1,), (1,)), ((), ())), preferred_element_type=F32)


def _dot_tn(a, b):
    return lax.dot_general(a, b, (((0,), (0,)), ((), ())), preferred_element_type=F32)


def _half_mask(hh):
    lane = lax.broadcasted_iota(jnp.int32, (1, LANES), 1)
    return (lane // HEAD_DIM) == hh


def _mod_kernel(c_ref, w_ref, b_ref, o_ref):
    c = c_ref[...]
    s = (c * jax.nn.sigmoid(c)).astype(BF16)
    o_ref[0] = _dot(s, w_ref[0].astype(BF16)) + b_ref[0]


def _modulation(cvec, w_mod, b_mod):
    depth, D, C = w_mod.shape
    tn = 1024
    return pl.pallas_call(
        _mod_kernel,
        grid=(depth, C // tn),
        in_specs=[pl.BlockSpec((8, D), lambda l, j: (0, 0)),
                  pl.BlockSpec((1, D, tn), lambda l, j: (l, 0, j)),
                  pl.BlockSpec((1, 1, tn), lambda l, j: (l, 0, j))],
        out_specs=pl.BlockSpec((1, 8, tn), lambda l, j: (l, 0, j)),
        out_shape=jax.ShapeDtypeStruct((depth, 8, C), F32),
        compiler_params=_cparams(("arbitrary", "arbitrary")),
        name="modulation",
    )(cvec, w_mod, b_mod.reshape(depth, 1, C))


def _rope(t, cos, s1, s2):
    w = t.shape[1]
    reps = w // LANES
    tile = lambda a: a if reps == 1 else jnp.concatenate([a] * reps, axis=1)
    return (t * tile(cos) + pltpu.roll(t, w - HEAD_DIM // 2, 1) * tile(s1)
            + pltpu.roll(t, HEAD_DIM // 2, 1) * tile(s2))


def _win_kernel(x_ref, g_ref, sh_ref, sc_ref, cos_ref, s1_ref, s2_ref, w_ref, o_ref, odt_ref, h_scr, *, rope):
    j = pl.program_id(2)

    @pl.when(j == 0)
    def _():
        h = _rms(x_ref[0], g_ref[...]) * (1.0 + sc_ref[0]) + sh_ref[0]
        h_scr[...] = h.astype(BF16)

    acc = _dot(h_scr[...], w_ref[...])
    rot = (lambda t: _rope(t, cos_ref[...], s1_ref[...], s2_ref[...])) if rope else (lambda t: t)

    @pl.when((j == T_AQ) | (j == T_SQ))
    def _():
        o_ref[0] = (rot(acc) * QSCALE).astype(BF16)

    @pl.when(j == T_AK)
    def _():
        o_ref[0] = rot(acc).astype(BF16)

    @pl.when(j == T_MISC)
    def _():
        o_ref[0] = jnp.concatenate([rot(acc[:, :LANES]), acc[:, LANES:]], axis=1).astype(BF16)
        odt_ref[0] = acc[:, 2 * LANES:]

    @pl.when(j == T_NQ)
    def _():
        o_ref[0] = (acc * QSCALE).astype(BF16)

    @pl.when((j > T_NQ) & (j < T_GATE))
    def _():
        o_ref[0] = acc.astype(BF16)

    @pl.when(j >= T_GATE)
    def _():
        o_ref[0] = jax.nn.sigmoid(acc).astype(BF16)


def _in_proj(x, g, shift, scale, rope_tabs, w, *, rope, tm):
    B, T, D = x.shape
    C = w.shape[1]
    tm = min(tm, T)
    cos, s1, s2 = rope_tabs
    tab_spec = pl.BlockSpec((tm, LANES), lambda b, i, j: (i, 0))
    mod_spec = pl.BlockSpec((1, 1, D), lambda b, i, j: (b, 0, 0))
    return pl.pallas_call(
        functools.partial(_win_kernel, rope=rope),
        grid=(B, T // tm, C // TN),
        in_specs=[pl.BlockSpec((1, tm, D), lambda b, i, j: (b, i, 0)),
                  pl.BlockSpec((1, D), lambda b, i, j: (0, 0)),
                  mod_spec, mod_spec, tab_spec, tab_spec, tab_spec,
                  pl.BlockSpec((D, TN), lambda b, i, j: (0, j))],
        out_specs=[pl.BlockSpec((1, tm, TN), lambda b, i, j: (b, i, j)),
                   pl.BlockSpec((1, tm, 2 * LANES), lambda b, i, j: (b, i, 0))],
        out_shape=[jax.ShapeDtypeStruct((B, T, C), BF16),
                   jax.ShapeDtypeStruct((B, T, 2 * LANES), F32)],
        scratch_shapes=[pltpu.VMEM((tm, D), BF16)],
        compiler_params=_cparams(("arbitrary", "arbitrary", "arbitrary")),
        name="in_proj_rope" if rope else "in_proj_ctx",
    )(x, g, shift, scale, cos, s1, s2, w)


def _diff_attn_kernel(lam_ref, g_ref, q_ref, k_ref, v_ref, o_ref, m_scr, l_scr, acc_scr, *, lam_init):
    kk = pl.program_id(3)

    @pl.when(kk == 0)
    def _():
        m_scr[...] = jnp.full(m_scr.shape, -jnp.inf, F32)
        l_scr[...] = jnp.zeros(l_scr.shape, F32)
        acc_scr[...] = jnp.zeros(acc_scr.shape, F32)

    q = q_ref[0]
    k = k_ref[0]
    v = v_ref[0]
    reps = k.shape[0] // LANES
    for m in range(2):
        s = _dot_nt(jnp.where(_half_mask(m), q, jnp.zeros_like(q)), k)
        m_prev = m_scr[m]
        m_next = jnp.maximum(m_prev, jnp.max(s, -1, keepdims=True))
        alpha = jnp.exp(m_prev - m_next)
        p = jnp.exp(s - jnp.concatenate([m_next] * reps, axis=1))
        l_scr[m] = alpha * l_scr[m] + jnp.sum(p, -1, keepdims=True)
        acc_scr[m] = alpha * acc_scr[m] + _dot(p.astype(BF16), v)
        m_scr[m] = m_next

    @pl.when(kk == pl.num_programs(3) - 1)
    def _():
        lv = lam_ref[...]
        lam = (jnp.exp(jnp.sum(lv[0:1] * lv[1:2], -1, keepdims=True))
               - jnp.exp(jnp.sum(lv[2:3] * lv[3:4], -1, keepdims=True)) + lam_init)
        o = acc_scr[0] / l_scr[0] - lam * (acc_scr[1] / l_scr[1])
        o_ref[0] = (_rms(o, g_ref[...]) * (1.0 - lam_init)).astype(BF16)


def _largest_divisor(n, cands):
    for c in cands:
        if n % c == 0:
            return c
    raise ValueError(f"no tile in {cands} divides {n}")


def _diff_attn(q_arr, q_blk0, k_all, v_all, lamv, g_subln, lam_init, *, tq):
    B, Tq = q_arr.shape[:2]
    Tk = k_all.shape[1]
    tq = min(tq, Tq)
    tk = _largest_divisor(Tk, (768, 640, 512, 384, 256, 128))
    return pl.pallas_call(
        functools.partial(_diff_attn_kernel, lam_init=lam_init),
        grid=(B, DA_HEADS, Tq // tq, Tk // tk),
        in_specs=[pl.BlockSpec((4, HEAD_DIM), lambda b, h, i, kk: (0, 0)),
                  pl.BlockSpec((1, LANES), lambda b, h, i, kk: (0, 0)),
                  pl.BlockSpec((1, tq, LANES), lambda b, h, i, kk: (b, i, q_blk0 + h)),
                  pl.BlockSpec((1, tk, LANES), lambda b, h, i, kk: (b, kk, h)),
                  pl.BlockSpec((1, tk, LANES), lambda b, h, i, kk: (b, kk, h))],
        out_specs=pl.BlockSpec((1, tq, LANES), lambda b, h, i, kk: (b, i, h)),
        out_shape=jax.ShapeDtypeStruct((B, Tq, DA_HEADS * LANES), BF16),
        scratch_shapes=[pltpu.VMEM((2, tq, LANES), F32), pltpu.VMEM((2, tq, LANES), F32),
                        pltpu.VMEM((2, tq, LANES), F32)],
        compiler_params=_cparams(("arbitrary",) * 4),
        name="diff_attn",
    )(lamv, g_subln, q_arr, k_all, v_all)


def _swa_kernel(sink_ref, q_ref, kp_ref, kc_ref, kn_ref, vp_ref, vc_ref, vn_ref, kx_ref, vx_ref, o_ref):
    i = pl.program_id(1)
    nb = pl.num_programs(1)
    blk = SW_BLOCK
    kband = jnp.concatenate([kp_ref[0], kc_ref[0], kn_ref[0], kx_ref[0]], axis=0)
    vband = jnp.concatenate([vp_ref[0], vc_ref[0], vn_ref[0], vx_ref[0]], axis=0)
    nkeys = kband.shape[0]
    row = lax.broadcasted_iota(jnp.int32, (blk, nkeys), 0)
    col = lax.broadcasted_iota(jnp.int32, (blk, nkeys), 1)
    big = 4 * blk
    off_prev = jnp.where(i > 0, 0, big)
    off_next = jnp.where(i < nb - 1, 0, big)
    ok_prev = jnp.where(col >= row + off_prev, 0.0, NEG)
    ok_next = jnp.where(col - 2 * blk <= row - off_next, 0.0, NEG)
    bias = jnp.where(col < blk, ok_prev, jnp.where((col >= 2 * blk) & (col < 3 * blk), ok_next, 0.0))
    bias = jnp.concatenate([bias] * 4, axis=0)
    q = q_ref[0]
    outs = [None] * 4
    for kv in range(SW_KV):
        hm = _half_mask(kv)
        qs = jnp.concatenate([jnp.where(hm, q[:, g * LANES:(g + 1) * LANES], jnp.zeros((blk, LANES), BF16))
                              for g in range(4)], axis=0)
        s = _dot_nt(qs, kband) + bias
        sink = jnp.concatenate([jnp.full((blk, 1), sink_ref[kv * 4 + g], F32) for g in range(4)], axis=0)
        m = jnp.maximum(jnp.max(s, -1, keepdims=True), sink)
        p = jnp.exp(s - m)
        denom = jnp.sum(p, -1, keepdims=True) + jnp.exp(sink - m)
        o = _dot(p.astype(BF16), jnp.where(hm, vband, jnp.zeros_like(vband))) / denom
        for g in range(4):
            part = o[g * blk:(g + 1) * blk]
            outs[g] = part if outs[g] is None else outs[g] + part
    o_ref[0] = jnp.concatenate(outs, axis=1).astype(BF16)


def _swa(p_lat, p_ctx, sink):
    B, N = p_lat.shape[:2]
    L = p_ctx.shape[1]
    nb = N // SW_BLOCK
    kcol, vcol = T_MISC * 4, T_MISC * 4 + 1
    kv_spec = lambda colblk, off: pl.BlockSpec(
        (1, SW_BLOCK, LANES), lambda b, i: (b, jnp.clip(i + off, 0, nb - 1), colblk))
    return pl.pallas_call(
        _swa_kernel,
        grid=(B, nb),
        in_specs=[pl.BlockSpec(memory_space=pltpu.SMEM),
                  pl.BlockSpec((1, SW_BLOCK, TN), lambda b, i: (b, i, T_SQ)),
                  kv_spec(kcol, -1), kv_spec(kcol, 0), kv_spec(kcol, 1),
                  kv_spec(vcol, -1), kv_spec(vcol, 0), kv_spec(vcol, 1),
                  pl.BlockSpec((1, L, LANES), lambda b, i: (b, 0, kcol)),
                  pl.BlockSpec((1, L, LANES), lambda b, i: (b, 0, vcol))],
        out_specs=pl.BlockSpec((1, SW_BLOCK, TN), lambda b, i: (b, i, 0)),
        out_shape=jax.ShapeDtypeStruct((B, N, TN), BF16),
        compiler_params=_cparams(("arbitrary", "arbitrary")),
        name="window_attn",
    )(sink, p_lat, p_lat, p_lat, p_lat, p_lat, p_lat, p_lat, p_ctx, p_ctx)


NA_QROWS = 2
NA_KBLKS = 5


def _na_kernel(bias_ref, q_ref, k0, k1, k2, k3, k4, v0, v1, v2, v3, v4, kx_ref, vx_ref, o_ref):
    kall = jnp.concatenate([k0[0], k1[0], k2[0], k3[0], k4[0], kx_ref[0]], axis=0)
    vall = jnp.concatenate([v0[0], v1[0], v2[0], v3[0], v4[0], vx_ref[0]], axis=0)
    q = q_ref[0]
    outs = []
    for pb in range(4):
        sl = slice(pb * LANES, (pb + 1) * LANES)
        qb, kb, vb = q[:, sl], kall[:, sl], vall[:, sl]
        ob = None
        for hh in range(2):
            hm = _half_mask(hh)
            s = _dot_nt(jnp.where(hm, qb, jnp.zeros_like(qb)), kb) + bias_ref[0, pb * 2 + hh]
            m = jnp.max(s, -1, keepdims=True)
            p = jnp.exp(s - m)
            o = _dot(p.astype(BF16), jnp.where(hm, vb, jnp.zeros_like(vb))) / jnp.sum(p, -1, keepdims=True)
            ob = o if ob is None else ob + o
        outs.append(ob)
    o_ref[0] = jnp.concatenate(outs, axis=1).astype(BF16)


def _na_bias(rpb, R, L):
    W = GRID_W
    H = rpb.shape[0]
    clamp = lambda v, lo, hi: min(max(v, lo), hi)
    rp = rpb.astype(F32)
    cols = []
    for c in range(W):
        cb = clamp(c - NA_COLS // 2, 0, W - NA_COLS)
        lo = cb - c + NA_COLS - 1
        cols.append(jnp.pad(rp[:, :, lo:lo + NA_COLS], ((0, 0), (0, 0), (cb, W - NA_COLS - cb)),
                            constant_values=NEG))
    t2 = jnp.stack(cols, axis=2)
    masked = jnp.full((H, W, W), NEG, F32)
    nblk = R // NA_QROWS
    per_cls = []
    for i in (0, 1, 2, nblk - 2, nblk - 1):
        base_blk = clamp(i - 2, 0, nblk - NA_KBLKS)
        per_t = []
        for t in range(NA_QROWS):
            rq = NA_QROWS * i + t
            base = clamp(rq - NA_ROWS // 2, 0, R - NA_ROWS)
            blocks = []
            for s in range(NA_KBLKS * NA_QROWS):
                rk = NA_QROWS * base_blk + s
                blocks.append(t2[:, rk - rq + NA_ROWS - 1] if 0 <= rk - base < NA_ROWS else masked)
            per_t.append(jnp.stack(blocks, axis=2))
        per_cls.append(jnp.stack(per_t, axis=1))
    b = jnp.stack(per_cls).reshape(5, H, NA_QROWS * W, NA_KBLKS * NA_QROWS * W)
    return jnp.concatenate([b, jnp.zeros(b.shape[:3] + (L,), F32)], axis=-1)


def _na(p_lat, p_ctx, bias):
    B, N = p_lat.shape[:2]
    L = p_ctx.shape[1]
    rows = NA_QROWS * GRID_W
    nblk = N // rows
    assert nblk >= NA_KBLKS + 2
    cls = lambda i: jnp.where(i < 2, i, jnp.where(i >= nblk - 2, i - (nblk - 5), 2))
    kv_spec = lambda tile, s: pl.BlockSpec(
        (1, rows, TN), lambda b, i: (b, jnp.clip(i - 2, 0, nblk - NA_KBLKS) + s, tile))
    nkeys = NA_KBLKS * rows + L
    return pl.pallas_call(
        _na_kernel,
        grid=(B, nblk),
        in_specs=[pl.BlockSpec((1, NA_HEADS, rows, nkeys), lambda b, i: (cls(i), 0, 0, 0)),
                  pl.BlockSpec((1, rows, TN), lambda b, i: (b, i, T_NQ))]
                 + [kv_spec(T_NK, s) for s in range(NA_KBLKS)]
                 + [kv_spec(T_NV, s) for s in range(NA_KBLKS)]
                 + [pl.BlockSpec((1, L, TN), lambda b, i: (b, 0, T_NK)),
                    pl.BlockSpec((1, L, TN), lambda b, i: (b, 0, T_NV))],
        out_specs=pl.BlockSpec((1, rows, TN), lambda b, i: (b, i, 0)),
        out_shape=jax.ShapeDtypeStruct((B, N, TN), BF16),
        compiler_params=_cparams(("arbitrary", "arbitrary")),
        name="neighbourhood_attn",
    )(bias, p_lat, *([p_lat] * (2 * NA_KBLKS)), p_ctx, p_ctx)


def _ctx_attend_kernel(sink_ref, q_ref, k_ref, v_ref, o_ref, *, gqa):
    q = q_ref[0]
    k = k_ref[0]
    v = v_ref[0]
    outs = []
    for blk in range(4):
        sl = slice(blk * LANES, (blk + 1) * LANES)
        qb = q[:, sl]
        kb, vb = (k, v) if gqa else (k[:, sl], v[:, sl])
        ob = None
        for hh in range(2):
            hm = _half_mask(hh)
            head = hh * 4 + blk if gqa else blk * 2 + hh
            s = _dot_nt(jnp.where(hm, qb, jnp.zeros_like(qb)), kb)
            sink = sink_ref[head]
            m = jnp.maximum(jnp.max(s, -1, keepdims=True), sink)
            p = jnp.exp(s - m)
            denom = jnp.sum(p, -1, keepdims=True) + jnp.exp(sink - m)
            o = _dot(p.astype(BF16), jnp.where(hm, vb, jnp.zeros_like(vb))) / denom
            ob = o if ob is None else ob + o
        outs.append(ob)
    o_ref[0] = jnp.concatenate(outs, axis=1).astype(BF16)


def _ctx_attend(p_ctx, sink, q_tile, k_blk, v_blk, *, gqa):
    B, L = p_ctx.shape[:2]
    kw = LANES if gqa else TN
    return pl.pallas_call(
        functools.partial(_ctx_attend_kernel, gqa=gqa),
        grid=(B,),
        in_specs=[pl.BlockSpec(memory_space=pltpu.SMEM),
                  pl.BlockSpec((1, L, TN), lambda b: (b, 0, q_tile)),
                  pl.BlockSpec((1, L, kw), lambda b: (b, 0, k_blk)),
                  pl.BlockSpec((1, L, kw), lambda b: (b, 0, v_blk))],
        out_specs=pl.BlockSpec((1, L, TN), lambda b: (b, 0, 0)),
        out_shape=jax.ShapeDtypeStruct((B, L, TN), BF16),
        compiler_params=_cparams(("arbitrary",)),
        name="ctx_attend_gqa" if gqa else "ctx_attend",
    )(sink, p_ctx, p_ctx, p_ctx)


CONV_BLK = 256
HALO = 16


def _conv_kernel(prev_ref, cur_ref, next_ref, w_ref, b_ref, o_ref, *, ctx_blocks):
    i = pl.program_id(1)
    last = pl.num_programs(1) - 1
    has_prev = (i != 0) & (i != ctx_blocks)
    has_next = (i != ctx_blocks - 1) & (i != last)
    prev = prev_ref[0].astype(F32) * has_prev.astype(F32)
    nxt = next_ref[0].astype(F32) * has_next.astype(F32)
    ext = jnp.concatenate([prev, cur_ref[0].astype(F32), nxt], axis=0)
    n = ext.shape[0]
    w = w_ref[...]
    acc = None
    for k in range(SSM_CONV):
        shift = (SSM_CONV // 2 - k) % n
        t = (ext if shift == 0 else pltpu.roll(ext, shift, 0))[HALO:HALO + CONV_BLK] * w[k:k + 1]
        acc = t if acc is None else acc + t
    acc = acc + b_ref[...]
    o_ref[0] = (acc * jax.nn.sigmoid(acc)).astype(BF16)


def _conv_silu(xbc, conv_w, conv_b, L):
    B, T, C = xbc.shape
    assert L % CONV_BLK == 0 and T % CONV_BLK == 0
    per = CONV_BLK // HALO
    nh = T // HALO
    w = jnp.zeros((8, C), F32).at[:SSM_CONV].set(conv_w.astype(F32))
    return pl.pallas_call(
        functools.partial(_conv_kernel, ctx_blocks=L // CONV_BLK),
        grid=(B, T // CONV_BLK),
        in_specs=[pl.BlockSpec((1, HALO, C), lambda b, i: (b, jnp.maximum(i * per - 1, 0), 0)),
                  pl.BlockSpec((1, CONV_BLK, C), lambda b, i: (b, i, 0)),
                  pl.BlockSpec((1, HALO, C), lambda b, i: (b, jnp.minimum((i + 1) * per, nh - 1), 0)),
                  pl.BlockSpec((8, C), lambda b, i: (0, 0)),
                  pl.BlockSpec((1, C), lambda b, i: (0, 0))],
        out_specs=pl.BlockSpec((1, CONV_BLK, C), lambda b, i: (b, i, 0)),
        out_shape=jax.ShapeDtypeStruct((B, T, C), BF16),
        compiler_params=_cparams(("arbitrary", "arbitrary")),
        name="ssm_conv",
    )(xbc, xbc, xbc, w, conv_b.reshape(1, C).astype(F32))


def _ssd_kernel(u_ref, dtr_ref, dtb_ref, alog_ref, tri_ref, exp_ref, y_ref, h_scr, *, fwd):
    @pl.when(pl.program_id(1) == 0)
    def _():
        h_scr[...] = jnp.zeros(h_scr.shape, F32)

    last = CHUNK - 1 if fwd else 0
    u = u_ref[0]
    xs = u[:, :SSM_INNER].astype(F32)
    bm = u[:, SSM_INNER:SSM_INNER + SSM_GROUPS * SSM_STATE]
    cm = u[:, SSM_INNER + SSM_GROUPS * SSM_STATE:]
    dt = jax.nn.softplus(dtr_ref[0] + dtb_ref[...])
    a = dt * (-jnp.exp(alog_ref[...]))
    acum = jnp.dot(tri_ref[...], a, precision=HIGHEST, preferred_element_type=F32)
    eac = jnp.exp(acum)
    dst = jnp.exp(acum[last:last + 1] - acum)
    expand = lambda t: jnp.dot(t, exp_ref[...], precision=HIGHEST, preferred_element_type=F32)
    dt_e, eac_e, dst_e = expand(dt), expand(eac), expand(dst)
    xg = xs * dt_e
    xg_b = xg.astype(BF16)
    xgd_b = (xg * dst_e).astype(BF16)
    acum_t = acum.T
    row = lax.broadcasted_iota(jnp.int32, (CHUNK, CHUNK), 0)
    col = lax.broadcasted_iota(jnp.int32, (CHUNK, CHUNK), 1)
    keep = (row >= col) if fwd else (row <= col)
    state = h_scr[...]
    state_b = state.astype(BF16)
    gw = SSM_INNER // SSM_GROUPS
    ys, st_new = [], []
    for g in range(SSM_GROUPS):
        bmg = bm[:, g * SSM_STATE:(g + 1) * SSM_STATE]
        cmg = cm[:, g * SSM_STATE:(g + 1) * SSM_STATE]
        cb = _dot_nt(cmg, bmg)
        yoff = _dot(cmg, state_b[:, g * gw:(g + 1) * gw])
        st_new.append(_dot_tn(bmg, xgd_b[:, g * gw:(g + 1) * gw]))
        for pb in range(gw // LANES):
            blk = g * (gw // LANES) + pb
            sl = slice(blk * LANES, (blk + 1) * LANES)
            xgb = xg_b[:, sl]
            yd = None
            for hh in range(2):
                h = blk * 2 + hh
                seg = acum[:, h:h + 1] - acum_t[h:h + 1, :]
                ld = jnp.exp(jnp.where(keep, seg, -jnp.inf))
                t = _dot((cb * ld).astype(BF16), jnp.where(_half_mask(hh), xgb, jnp.zeros_like(xgb)))
                yd = t if yd is None else yd + t
            ys.append(yd + yoff[:, pb * LANES:(pb + 1) * LANES] * eac_e[:, sl])
    y_ref[0] = jnp.concatenate(ys, axis=1)
    h_scr[...] = state * eac_e[last:last + 1] + jnp.concatenate(st_new, axis=1)


def _ssd(u, dtr, dt_bias, a_log, L, *, fwd):
    B, T, _ = u.shape
    nc, ncx = T // CHUNK, L // CHUNK
    d = 0 if fwd else 1
    chunk = (lambda c: c) if fwd else (lambda c: jnp.where(c < ncx, ncx - 1 - c, nc + ncx - 1 - c))
    pad = lambda t: jnp.zeros((1, LANES), F32).at[0, :SSM_HEADS].set(t[d].astype(F32))
    li = np.arange(CHUNK)
    tri = (li[:, None] >= li[None, :]) if fwd else (li[:, None] <= li[None, :])
    expm = np.zeros((LANES, SSM_INNER), np.float32)
    for h in range(SSM_HEADS):
        expm[h, h * 64:(h + 1) * 64] = 1.0
    return pl.pallas_call(
        functools.partial(_ssd_kernel, fwd=fwd),
        grid=(B, nc),
        in_specs=[pl.BlockSpec((1, CHUNK, SSM_XBC), lambda b, c: (b, chunk(c), 0)),
                  pl.BlockSpec((1, CHUNK, LANES), lambda b, c: (b, chunk(c), d)),
                  pl.BlockSpec((1, LANES), lambda b, c: (0, 0)),
                  pl.BlockSpec((1, LANES), lambda b, c: (0, 0)),
                  pl.BlockSpec((CHUNK, CHUNK), lambda b, c: (0, 0)),
                  pl.BlockSpec((LANES, SSM_INNER), lambda b, c: (0, 0))],
        out_specs=pl.BlockSpec((1, CHUNK, SSM_INNER), lambda b, c: (b, chunk(c), 0)),
        out_shape=jax.ShapeDtypeStruct((B, T, SSM_INNER), F32),
        scratch_shapes=[pltpu.VMEM((SSM_STATE, SSM_INNER), F32)],
        compiler_params=_cparams(("arbitrary", "arbitrary")),
        name="ssd_scan_fwd" if fwd else "ssd_scan_bwd",
    )(u, dtr, pad(dt_bias), pad(a_log), jnp.asarray(tri.astype(np.float32)), jnp.asarray(expm))


def _ssm_finish_kernel(yf_ref, yb_ref, u_ref, z_ref, dsk_ref, g_ref, o_ref):
    y = yf_ref[0] + yb_ref[0] + dsk_ref[...] * u_ref[0].astype(F32)
    z = z_ref[0].astype(F32)
    o_ref[0] = _rms(y * (z * jax.nn.sigmoid(z)), g_ref[...]).astype(BF16)


def _ssm_finish(y_f, y_b, u, z, d_skip, g_ssm, *, tm=256):
    B, T, W = y_f.shape
    dsk = jnp.repeat(d_skip.astype(F32), W // SSM_HEADS).reshape(1, W)
    return pl.pallas_call(
        _ssm_finish_kernel,
        grid=(B, T // tm),
        in_specs=[pl.BlockSpec((1, tm, W), lambda b, i: (b, i, 0))] * 4 + [
                  pl.BlockSpec((1, W), lambda b, i: (0, 0)),
                  pl.BlockSpec((1, W), lambda b, i: (0, 0))],
        out_specs=pl.BlockSpec((1, tm, W), lambda b, i: (b, i, 0)),
        out_shape=jax.ShapeDtypeStruct((B, T, W), BF16),
        compiler_params=_cparams(("arbitrary", "arbitrary")),
        name="ssm_finish",
    )(y_f, y_b, u, z, dsk, g_ssm.reshape(1, W).astype(F32))


def _merge_kernel(y0, y1, y2, y3, s0, s1, s2, s3, wb_ref, wo_ref, x_ref, gate_ref, g_ref, o_ref, acc):
    j = pl.program_id(2)
    m = None
    for k, (y, s) in enumerate(((y0, s0), (y1, s1), (y2, s2), (y3, s3))):
        t = _dot(y[0], wb_ref[k]) * s[0].astype(F32)
        m = t if m is None else m + t
    contrib = _dot(m.astype(BF16), wo_ref[...])

    @pl.when(j == 0)
    def _():
        acc[...] = contrib

    @pl.when(j > 0)
    def _():
        acc[...] += contrib

    @pl.when(j == pl.num_programs(2) - 1)
    def _():
        o_ref[0] = x_ref[0] + gate_ref[0] * _rms(acc[...], g_ref[...])


def _merge(ys, p, w_branch, w_out, x, gate, g_post, *, tm):
    B, T, D = x.shape
    tm = min(tm, T)
    nj = D // TN
    y_spec = pl.BlockSpec((1, tm, BRANCH_W), lambda b, i, j: (b, i, 0))
    s_spec = lambda k: pl.BlockSpec((1, tm, TN), lambda b, i, j: (b, i, T_GATE + k * nj + j))
    return pl.pallas_call(
        _merge_kernel,
        grid=(B, T // tm, nj),
        in_specs=[y_spec] * 4 + [s_spec(k) for k in range(N_BRANCH)]
                 + [pl.BlockSpec((N_BRANCH, BRANCH_W, TN), lambda b, i, j: (0, 0, j)),
                    pl.BlockSpec((TN, D), lambda b, i, j: (j, 0)),
                    pl.BlockSpec((1, tm, D), lambda b, i, j: (b, i, 0)),
                    pl.BlockSpec((1, 1, D), lambda b, i, j: (b, 0, 0)),
                    pl.BlockSpec((1, D), lambda b, i, j: (0, 0))],
        out_specs=pl.BlockSpec((1, tm, D), lambda b, i, j: (b, i, 0)),
        out_shape=jax.ShapeDtypeStruct((B, T, D), F32),
        scratch_shapes=[pltpu.VMEM((tm, D), F32)],
        compiler_params=_cparams(("arbitrary", "arbitrary", "arbitrary")),
        name="merge_out_proj",
    )(*ys, p, p, p, p, w_branch, w_out, x, gate, g_post)


def _ffn_kernel(x_ref, g1_ref, sh_ref, sc_ref, gate_ref, g2_ref, w1_ref, w2_ref, o_ref, h_scr, acc):
    f = pl.program_id(2)

    @pl.when(f == 0)
    def _():
        h = _rms(x_ref[0], g1_ref[...]) * (1.0 + sc_ref[0]) + sh_ref[0]
        h_scr[...] = h.astype(BF16)

    a = jnp.square(jnp.maximum(_dot(h_scr[...], w1_ref[...]), 0.0)).astype(BF16)
    contrib = _dot(a, w2_ref[...])

    @pl.when(f == 0)
    def _():
        acc[...] = contrib

    @pl.when(f > 0)
    def _():
        acc[...] += contrib

    @pl.when(f == pl.num_programs(2) - 1)
    def _():
        o_ref[0] = x_ref[0] + gate_ref[0] * _rms(acc[...], g2_ref[...])


def _ffn(x, g_pre, shift, scale, gate, g_post, w1, w2, *, tm, tf=512):
    B, T, D = x.shape
    F = w1.shape[1]
    tm = min(tm, T)
    mod_spec = pl.BlockSpec((1, 1, D), lambda b, i, f: (b, 0, 0))
    vec_spec = pl.BlockSpec((1, D), lambda b, i, f: (0, 0))
    return pl.pallas_call(
        _ffn_kernel,
        grid=(B, T // tm, F // tf),
        in_specs=[pl.BlockSpec((1, tm, D), lambda b, i, f: (b, i, 0)),
                  vec_spec, mod_spec, mod_spec, mod_spec, vec_spec,
                  pl.BlockSpec((D, tf), lambda b, i, f: (0, f)),
                  pl.BlockSpec((tf, D), lambda b, i, f: (f, 0))],
        out_specs=pl.BlockSpec((1, tm, D), lambda b, i, f: (b, i, 0)),
        out_shape=jax.ShapeDtypeStruct((B, T, D), F32),
        scratch_shapes=[pltpu.VMEM((tm, D), BF16), pltpu.VMEM((tm, D), F32)],
        compiler_params=_cparams(("arbitrary", "arbitrary", "arbitrary")),
        name="mlp",
    )(x, g_pre, shift, scale, gate, g_post, w1, w2)


def _layout_w_in(w):
    D = w.shape[0]
    o = 0
    parts = {}
    for name, width in (("aq", 512), ("ak", 512), ("av", 512), ("bz", 512), ("bx", SSM_XBC), ("bdt", 16),
                        ("sq", 512), ("sk", 128), ("sv", 128), ("nq", 512), ("nk", 512), ("nv", 512),
                        ("gate", N_BRANCH * D)):
        parts[name] = w[:, o:o + width]
        o += width
    assert o == w.shape[1]
    sq = parts["sq"].reshape(D, SW_KV, 4, HEAD_DIM).transpose(0, 2, 1, 3).reshape(D, 512)
    zpad = jnp.zeros((D, LANES - SSM_HEADS), w.dtype)
    misc = jnp.concatenate([parts["sk"], parts["sv"], parts["bdt"][:, :SSM_HEADS], zpad,
                            parts["bdt"][:, SSM_HEADS:], zpad], axis=1)
    cols = [parts["aq"], parts["ak"], sq, misc, parts["nq"], parts["nk"], parts["nv"], parts["av"],
            parts["bz"], parts["bx"], parts["gate"]]
    out = jnp.concatenate(cols, axis=1).astype(BF16)
    assert out.shape[1] == N_TILES * TN
    return out


def _rope_tables(n):
    t = np.arange(n)
    nf = HEAD_DIM // 4
    inv = ROPE_BASE ** (-np.arange(nf, dtype=np.float32) / nf)
    row = (t // GRID_W).astype(np.float32)
    col = (t % GRID_W).astype(np.float32)
    ang = jnp.asarray(np.concatenate([row[:, None] * inv, col[:, None] * inv], -1).astype(np.float32))
    cos, sin = jnp.cos(ang), jnp.sin(ang)
    z = jnp.zeros_like(sin)
    rep = LANES // HEAD_DIM
    cos_t = jnp.tile(jnp.concatenate([cos, cos], -1), (1, rep))
    s1 = jnp.tile(jnp.concatenate([-sin, z], -1), (1, rep))
    s2 = jnp.tile(jnp.concatenate([z, sin], -1), (1, rep))
    return cos_t, s1, s2


def kernel(x, c, ctx, c_ctx, w_mod, b_mod, g_pre_mix, g_post_mix, g_pre_mlp, g_post_mlp, w_in, lam_q1, lam_k1, lam_q2, lam_k2, g_subln, conv_w, conv_b, dt_bias, a_log, d_skip, g_ssm, sink, rpb, w_branch, w_out, w_ff1, w_ff2):
    B, N, D = x.shape
    L = ctx.shape[1]
    depth = w_mod.shape[0]
    assert B + 1 <= 8 and N % GRID_W == 0
    R = N // GRID_W
    tm_lat, tm_ctx = 1024, 256

    cvec = jnp.zeros((8, D), F32).at[:B].set(c).at[B].set(c_ctx)
    mods = _modulation(cvec, w_mod, b_mod)
    rope_tabs = _rope_tables(N)
    ctx_tabs = tuple(t[:L] for t in rope_tabs)
    no_sink = jnp.full((NA_HEADS,), NEG, F32)
    row = lambda v: v.reshape(1, -1).astype(F32)

    cx = ctx
    for l in range(depth):
        need_ctx = l < depth - 1
        lam_init = 0.8 - 0.6 * math.exp(-0.3 * l)
        mod = mods[l, :B].reshape(B, 6, 1, D)
        modc = jnp.broadcast_to(mods[l, B].reshape(1, 6, 1, D), (B, 6, 1, D))
        w_in_l = _layout_w_in(w_in[l])
        p_lat, dt_lat = _in_proj(x, row(g_pre_mix[l]), mod[:, 0], mod[:, 1], rope_tabs, w_in_l,
                                 rope=True, tm=tm_lat)
        p_ctx, dt_ctx = _in_proj(cx, row(g_pre_mix[l]), modc[:, 0], modc[:, 1], ctx_tabs, w_in_l,
                                 rope=False, tm=tm_ctx)
        tile = lambda p, t, n=1: p[:, :, t * TN:(t + n) * TN]

        lamv = jnp.stack([lam_q1[l], lam_k1[l], lam_q2[l], lam_k2[l]]).astype(F32)
        k_all = jnp.concatenate([tile(p_ctx, T_AK), tile(p_lat, T_AK)], axis=1)
        v_all = jnp.concatenate([tile(p_ctx, T_AV), tile(p_lat, T_AV)], axis=1)
        ya = _diff_attn(p_lat, T_AQ * 4, k_all, v_all, lamv, row(g_subln[l]), lam_init, tq=512)

        xbc = jnp.concatenate([tile(p_ctx, T_BX, 2), tile(p_lat, T_BX, 2)], axis=1)
        z = jnp.concatenate([tile(p_ctx, T_BZ), tile(p_lat, T_BZ)], axis=1)
        dtr = jnp.concatenate([dt_ctx, dt_lat], axis=1)
        u = _conv_silu(xbc, conv_w[l], conv_b[l], L)
        y_f = _ssd(u, dtr, dt_bias[l], a_log[l], L, fwd=True)
        y_b = _ssd(u, dtr, dt_bias[l], a_log[l], L, fwd=False)
        yb_all = _ssm_finish(y_f, y_b, u, z, d_skip[l], g_ssm[l])
        yb, yb_c = yb_all[:, L:], yb_all[:, :L]

        ys = _swa(p_lat, p_ctx, sink[l].astype(F32))
        yn = _na(p_lat, p_ctx, _na_bias(rpb[l], R, L))

        wb = w_branch[l]
        wb = wb.at[2].set(wb[2].reshape(SW_KV, 4, HEAD_DIM, D).transpose(1, 0, 2, 3).reshape(BRANCH_W, D))
        wb = wb.astype(BF16)
        wo = w_out[l].astype(BF16)
        w1 = w_ff1[l].astype(BF16)
        w2 = w_ff2[l].astype(BF16)
        x = _merge([ya, yb, ys, yn], p_lat, wb, wo, x, mod[:, 2], row(g_post_mix[l]), tm=512)
        x = _ffn(x, row(g_pre_mlp[l]), mod[:, 3], mod[:, 4], mod[:, 5], row(g_post_mlp[l]), w1, w2, tm=512)
        if need_ctx:
            ya_c = _diff_attn(p_ctx, T_AQ * 4, tile(p_ctx, T_AK), tile(p_ctx, T_AV), lamv, row(g_subln[l]),
                              lam_init, tq=L)
            ys_c = _ctx_attend(p_ctx, sink[l].astype(F32), T_SQ, T_MISC * 4, T_MISC * 4 + 1, gqa=True)
            yn_c = _ctx_attend(p_ctx, no_sink, T_NQ, T_NK, T_NV, gqa=False)
            cx = _merge([ya_c, yb_c, ys_c, yn_c], p_ctx, wb, wo, cx, modc[:, 2], row(g_post_mix[l]), tm=tm_ctx)
            cx = _ffn(cx, row(g_pre_mlp[l]), modc[:, 3], modc[:, 4], modc[:, 5], row(g_post_mlp[l]), w1, w2,
                      tm=tm_ctx)
    return x
```

```python
import functools
import math

import numpy as np
import jax
import jax.numpy as jnp
from jax import lax
from jax.experimental import pallas as pl
from jax.experimental.pallas import tpu as pltpu

F32 = jnp.float32
BF16 = jnp.bfloat16
HIGHEST = lax.Precision.HIGHEST

LANES = 128
VMEM_LIMIT = 56 * 1024 * 1024
NEG = -1e30
EPS = 1e-6
GRID_W = 64
ROPE_BASE = 10000.0
HEAD_DIM = 64
QSCALE = HEAD_DIM ** -0.5
LOG2E = math.log2(math.e)
DA_HEADS = 4
SSM_HEADS = 8
SSM_INNER = 512
SSM_GROUPS = 2
SSM_STATE = 128
SSM_XBC = SSM_INNER + 2 * SSM_GROUPS * SSM_STATE
SSM_CONV = 5
CHUNK = 128
SW_KV = 2
SW_BLOCK = 128
NA_HEADS = 8
NA_ROWS = 8
NA_COLS = 16
N_BRANCH = 4
BRANCH_W = 512

TN = 512
R_AQ, R_AK, R_SQ, R_MISC = 0, 1, 2, 3
P_NQ, P_NK, P_NV, P_AV, P_BZ = 2, 3, 4, 5, 6


def _cparams(sem):
    return pltpu.CompilerParams(dimension_semantics=sem, vmem_limit_bytes=VMEM_LIMIT)


def _rms(t, g):
    return t * lax.rsqrt(jnp.mean(t * t, -1, keepdims=True) + EPS) * g


def _dot(a, b):
    return jnp.dot(a, b, preferred_element_type=F32)


def _dot_nt(a, b):
    return lax.dot_general(a, b, (((1,), (1,)), ((), ())), preferred_element_type=F32)


def _dot_tn(a, b):
    return lax.dot_general(a, b, (((0,), (0,)), ((), ())), preferred_element_type=F32)


def _half_mask(hh):
    lane = lax.broadcasted_iota(jnp.int32, (1, LANES), 1)
    return (lane // HEAD_DIM) == hh


def _largest_divisor(n, cands):
    for c in cands:
        if n % c == 0:
            return c
    raise ValueError(f"no tile in {cands} divides {n}")


def _mod_kernel(c_ref, w_ref, b_ref, o_ref):
    c = c_ref[...]
    s = (c * jax.nn.sigmoid(c)).astype(BF16)
    o_ref[0] = _dot(s, w_ref[0].astype(BF16)) + b_ref[0]


def _modulation(cvec, w_mod, b_mod):
    depth, D, C = w_mod.shape
    tn = 1024
    return pl.pallas_call(
        _mod_kernel,
        grid=(depth, C // tn),
        in_specs=[pl.BlockSpec((8, D), lambda l, j: (0, 0)),
                  pl.BlockSpec((1, D, tn), lambda l, j: (l, 0, j)),
                  pl.BlockSpec((1, 1, tn), lambda l, j: (l, 0, j))],
        out_specs=pl.BlockSpec((1, 8, tn), lambda l, j: (l, 0, j)),
        out_shape=jax.ShapeDtypeStruct((depth, 8, C), F32),
        compiler_params=_cparams(("arbitrary", "arbitrary")),
        name="modulation",
    )(cvec, w_mod, b_mod.reshape(depth, 1, C))


NORM_BLK = 256


def _prenorm_kernel(x_ref, cx_ref, g_ref, sh_ref, sc_ref, o_ref, *, nlat):
    i = pl.program_id(1)

    def emit(t):
        o_ref[0] = (_rms(t, g_ref[...]) * (1.0 + sc_ref[0]) + sh_ref[0]).astype(BF16)

    @pl.when(i < nlat)
    def _():
        emit(x_ref[0])

    @pl.when(i >= nlat)
    def _():
        emit(cx_ref[0])


def _prenorm(x, cx, g, shift2, scale2):
    B, N, D = x.shape
    L = cx.shape[1]
    assert N % NORM_BLK == 0 and L % NORM_BLK == 0
    nlat, nctx = N // NORM_BLK, L // NORM_BLK
    mod_spec = pl.BlockSpec((1, 1, D), lambda b, i: (b * 2 + (i >= nlat).astype(jnp.int32), 0, 0))
    return pl.pallas_call(
        functools.partial(_prenorm_kernel, nlat=nlat),
        grid=(B, nlat + nctx),
        in_specs=[pl.BlockSpec((1, NORM_BLK, D), lambda b, i: (b, jnp.minimum(i, nlat - 1), 0)),
                  pl.BlockSpec((1, NORM_BLK, D), lambda b, i: (b, jnp.maximum(i - nlat, 0), 0)),
                  pl.BlockSpec((1, D), lambda b, i: (0, 0)),
                  mod_spec, mod_spec],
        out_specs=pl.BlockSpec((1, NORM_BLK, D), lambda b, i: (b, i, 0)),
        out_shape=jax.ShapeDtypeStruct((B, N + L, D), BF16),
        compiler_params=_cparams(("arbitrary", "arbitrary")),
        name="prenorm",
    )(x, cx, g, shift2, scale2)


def _rope(t, cos, s1, s2):
    w = t.shape[1]
    reps = w // LANES
    tile = lambda a: a if reps == 1 else jnp.concatenate([a] * reps, axis=1)
    return (t * tile(cos) + pltpu.roll(t, w - HEAD_DIM // 2, 1) * tile(s1)
            + pltpu.roll(t, HEAD_DIM // 2, 1) * tile(s2))


def _proj_rope_kernel(h_ref, cos_ref, s1_ref, s2_ref, w_ref, o_ref, odt_ref):
    j = pl.program_id(2)
    acc = _dot(h_ref[0], w_ref[...])
    rot = lambda t: _rope(t, cos_ref[...], s1_ref[...], s2_ref[...])

    @pl.when(j != R_MISC)
    def _():
        o_ref[0] = rot(acc).astype(BF16)

    @pl.when(j == R_MISC)
    def _():
        o_ref[0] = jnp.concatenate([rot(acc[:, :LANES]), acc[:, LANES:]], axis=1).astype(BF16)
        odt_ref[0] = acc[:, 2 * LANES:]


def _row_tile(T):
    return _largest_divisor(T, (1056, 1024, 768, 640, 512, 256, 128))


def _proj_rope(h, rope_tabs, w):
    B, T, D = h.shape
    C = w.shape[1]
    tm = _row_tile(T)
    cos, s1, s2 = rope_tabs
    tab_spec = pl.BlockSpec((tm, LANES), lambda b, i, j: (i, 0))
    return pl.pallas_call(
        _proj_rope_kernel,
        grid=(B, T // tm, C // TN),
        in_specs=[pl.BlockSpec((1, tm, D), lambda b, i, j: (b, i, 0)),
                  tab_spec, tab_spec, tab_spec,
                  pl.BlockSpec((D, TN), lambda b, i, j: (0, j))],
        out_specs=[pl.BlockSpec((1, tm, TN), lambda b, i, j: (b, i, j)),
                   pl.BlockSpec((1, tm, 2 * LANES), lambda b, i, j: (b, i, 0))],
        out_shape=[jax.ShapeDtypeStruct((B, T, C), BF16),
                   jax.ShapeDtypeStruct((B, T, 2 * LANES), F32)],
        compiler_params=_cparams(("arbitrary", "arbitrary", "arbitrary")),
        name="proj_rope",
    )(h, cos, s1, s2, w)


def _proj_kernel(h_ref, w_ref, o_ref, *, gate):
    acc = _dot(h_ref[0], w_ref[...])
    o_ref[0] = (jax.nn.sigmoid(acc) if gate else acc).astype(BF16)


def _proj(h, w, *, gate, tn):
    B, T, D = h.shape
    C = w.shape[1]
    tm = _row_tile(T)
    return pl.pallas_call(
        functools.partial(_proj_kernel, gate=gate),
        grid=(B, T // tm, C // tn),
        in_specs=[pl.BlockSpec((1, tm, D), lambda b, i, j: (b, i, 0)),
                  pl.BlockSpec((D, tn), lambda b, i, j: (0, j))],
        out_specs=pl.BlockSpec((1, tm, tn), lambda b, i, j: (b, i, j)),
        out_shape=jax.ShapeDtypeStruct((B, T, C), BF16),
        compiler_params=_cparams(("arbitrary", "arbitrary", "arbitrary")),
        name="proj_gate" if gate else "proj_plain",
    )(h, w)


VT_ROWS = LANES + 16


def _diff_attn_kernel(lam_ref, g_ref, q_ref, k_ref, v_ref, o_ref, s_scr, *, lam_init, tk):
    q_t = q_ref[0]
    tq = q_t.shape[1]
    nk = v_ref.shape[2]
    dim = lax.broadcasted_iota(jnp.int32, (LANES, tq), 0)
    q_m = [jnp.where((dim // HEAD_DIM) == m, q_t, jnp.zeros_like(q_t)) for m in range(2)]

    def scores(c):
        kc = k_ref[0, c * tk:(c + 1) * tk, :]
        for m in range(2):
            s_scr[c % 2, m] = _dot(kc, q_m[m])

    carry = [(jnp.full((1, tq), -jnp.inf, F32), jnp.zeros((VT_ROWS, tq), F32)) for _ in range(2)]
    scores(0)
    for c in range(nk):
        if c + 1 < nk:
            scores(c + 1)
        vc = v_ref[0, 0, c]
        for m in range(2):
            mx, acc = carry[m]
            m_new = jnp.maximum(mx, jnp.max(s_scr[c % 2, m], 0, keepdims=True))
            p = jnp.exp2((s_scr[c % 2, m] - m_new).astype(BF16))
            acc = jnp.exp2(mx - m_new) * acc + _dot(vc, p)
            carry[m] = (m_new, acc)
    (_, a0), (_, a1) = carry
    lv = lam_ref[...]
    lam = (jnp.exp(jnp.sum(lv[0:1] * lv[1:2], -1, keepdims=True))
           - jnp.exp(jnp.sum(lv[2:3] * lv[3:4], -1, keepdims=True)) + lam_init)
    o = a0[:LANES] / a0[LANES:LANES + 1] - lam * (a1[:LANES] / a1[LANES:LANES + 1])
    o = o * lax.rsqrt(jnp.mean(o * o, 0, keepdims=True) + EPS) * g_ref[...] * (1.0 - lam_init)
    o_ref[0] = o.T.astype(BF16)


def _diff_attn(q_t, p_rope, v_t, lamv, g_col, lam_init, *, row_blk0, tq):
    B, _, Tq = q_t.shape
    _, _, nk, _, tk = v_t.shape
    Tk = nk * tk
    tq = min(tq, Tq)
    return pl.pallas_call(
        functools.partial(_diff_attn_kernel, lam_init=lam_init, tk=tk),
        grid=(B, DA_HEADS, Tq // tq),
        in_specs=[pl.BlockSpec((4, HEAD_DIM), lambda b, h, i: (0, 0)),
                  pl.BlockSpec((LANES, 1), lambda b, h, i: (0, 0)),
                  pl.BlockSpec((1, LANES, tq), lambda b, h, i: (b, h, i)),
                  pl.BlockSpec((1, Tk, LANES), lambda b, h, i: (b, row_blk0, R_AK * 4 + h)),
                  pl.BlockSpec((1, 1, nk, VT_ROWS, tk), lambda b, h, i: (b, h, 0, 0, 0))],
        out_specs=pl.BlockSpec((1, tq, LANES), lambda b, h, i: (b, i, h)),
        out_shape=jax.ShapeDtypeStruct((B, Tq, DA_HEADS * LANES), BF16),
        scratch_shapes=[pltpu.VMEM((2, 2, tk, tq), F32)],
        compiler_params=_cparams(("arbitrary",) * 3),
        name="diff_attn",
    )(lamv, g_col, q_t, p_rope, v_t)


def _transpose_chunks(v, tk):
    B, Tk, _ = v.shape
    nk = Tk // tk
    vt = v.reshape(B, nk, tk, DA_HEADS, LANES).transpose(0, 3, 1, 4, 2)
    extra = jnp.zeros((B, DA_HEADS, nk, VT_ROWS - LANES, tk), v.dtype).at[:, :, :, 0].set(1)
    return jnp.concatenate([vt, extra], axis=3)


def _swa_kernel(sink_ref, q_ref, kp_ref, kc_ref, kn_ref, vp_ref, vc_ref, vn_ref, kx_ref, vx_ref, o_ref):
    i = pl.program_id(1)
    nb = pl.num_programs(1)
    blk = SW_BLOCK
    kband = jnp.concatenate([kp_ref[0], kc_ref[0], kn_ref[0], kx_ref[0]], axis=0)
    vband = jnp.concatenate([vp_ref[0], vc_ref[0], vn_ref[0], vx_ref[0]], axis=0)
    nkeys = kband.shape[0]
    row = lax.broadcasted_iota(jnp.int32, (blk, nkeys), 0)
    col = lax.broadcasted_iota(jnp.int32, (blk, nkeys), 1)
    big = 4 * blk
    off_prev = jnp.where(i > 0, 0, big)
    off_next = jnp.where(i < nb - 1, 0, big)
    ok_prev = jnp.where(col >= row + off_prev, 0.0, NEG)
    ok_next = jnp.where(col - 2 * blk <= row - off_next, 0.0, NEG)
    bias = jnp.where(col < blk, ok_prev, jnp.where((col >= 2 * blk) & (col < 3 * blk), ok_next, 0.0))
    bias = jnp.concatenate([bias] * 4, axis=0)
    q = q_ref[0]
    outs = [None] * 4
    for kv in range(SW_KV):
        hm = _half_mask(kv)
        qs = jnp.concatenate([jnp.where(hm, q[:, g * LANES:(g + 1) * LANES], jnp.zeros((blk, LANES), BF16))
                              for g in range(4)], axis=0)
        s = _dot_nt(qs, kband) + bias
        sink = jnp.concatenate([jnp.full((blk, 1), sink_ref[kv * 4 + g], F32) for g in range(4)], axis=0)
        m = jnp.maximum(jnp.max(s, -1, keepdims=True), sink)
        p = jnp.exp(s - m)
        denom = jnp.sum(p, -1, keepdims=True) + jnp.exp(sink - m)
        o = _dot(p.astype(BF16), jnp.where(hm, vband, jnp.zeros_like(vband))) / denom
        for g in range(4):
            part = o[g * blk:(g + 1) * blk]
            outs[g] = part if outs[g] is None else outs[g] + part
    o_ref[0] = jnp.concatenate(outs, axis=1).astype(BF16)


def _swa(p_rope, sink, N, L):
    B = p_rope.shape[0]
    nb = N // SW_BLOCK
    kcol, vcol = R_MISC * 4, R_MISC * 4 + 1
    kv_spec = lambda colblk, off: pl.BlockSpec(
        (1, SW_BLOCK, LANES), lambda b, i: (b, jnp.clip(i + off, 0, nb - 1), colblk))
    return pl.pallas_call(
        _swa_kernel,
        grid=(B, nb),
        in_specs=[pl.BlockSpec(memory_space=pltpu.SMEM),
                  pl.BlockSpec((1, SW_BLOCK, TN), lambda b, i: (b, i, R_SQ)),
                  kv_spec(kcol, -1), kv_spec(kcol, 0), kv_spec(kcol, 1),
                  kv_spec(vcol, -1), kv_spec(vcol, 0), kv_spec(vcol, 1),
                  pl.BlockSpec((1, L, LANES), lambda b, i: (b, N // L, kcol)),
                  pl.BlockSpec((1, L, LANES), lambda b, i: (b, N // L, vcol))],
        out_specs=pl.BlockSpec((1, SW_BLOCK, TN), lambda b, i: (b, i, 0)),
        out_shape=jax.ShapeDtypeStruct((B, N, TN), BF16),
        compiler_params=_cparams(("arbitrary", "arbitrary")),
        name="window_attn",
    )(sink, *([p_rope] * 9))


NA_QROWS = 2
NA_KBLKS = 5


def _na_kernel(bias_ref, q_ref, k0, k1, k2, k3, k4, v0, v1, v2, v3, v4, kx_ref, vx_ref, o_ref):
    kall = jnp.concatenate([k0[0], k1[0], k2[0], k3[0], k4[0], kx_ref[0]], axis=0)
    vall = jnp.concatenate([v0[0], v1[0], v2[0], v3[0], v4[0], vx_ref[0]], axis=0)
    q = q_ref[0]
    outs = []
    for pb in range(4):
        sl = slice(pb * LANES, (pb + 1) * LANES)
        qb, kb, vb = q[:, sl], kall[:, sl], vall[:, sl]
        ob = None
        for hh in range(2):
            hm = _half_mask(hh)
            s = _dot_nt(jnp.where(hm, qb, jnp.zeros_like(qb)), kb) + bias_ref[0, pb * 2 + hh]
            m = jnp.max(s, -1, keepdims=True)
            p = jnp.exp(s - m)
            o = _dot(p.astype(BF16), jnp.where(hm, vb, jnp.zeros_like(vb))) / jnp.sum(p, -1, keepdims=True)
            ob = o if ob is None else ob + o
        outs.append(ob)
    o_ref[0] = jnp.concatenate(outs, axis=1).astype(BF16)


def _na_bias(rpb, R, L):
    W = GRID_W
    H = rpb.shape[0]
    clamp = lambda v, lo, hi: min(max(v, lo), hi)
    rp = rpb.astype(F32)
    cols = []
    for c in range(W):
        cb = clamp(c - NA_COLS // 2, 0, W - NA_COLS)
        lo = cb - c + NA_COLS - 1
        cols.append(jnp.pad(rp[:, :, lo:lo + NA_COLS], ((0, 0), (0, 0), (cb, W - NA_COLS - cb)),
                            constant_values=NEG))
    t2 = jnp.stack(cols, axis=2)
    masked = jnp.full((H, W, W), NEG, F32)
    nblk = R // NA_QROWS
    per_cls = []
    for i in (0, 1, 2, nblk - 2, nblk - 1):
        base_blk = clamp(i - 2, 0, nblk - NA_KBLKS)
        per_t = []
        for t in range(NA_QROWS):
            rq = NA_QROWS * i + t
            base = clamp(rq - NA_ROWS // 2, 0, R - NA_ROWS)
            blocks = []
            for s in range(NA_KBLKS * NA_QROWS):
                rk = NA_QROWS * base_blk + s
                blocks.append(t2[:, rk - rq + NA_ROWS - 1] if 0 <= rk - base < NA_ROWS else masked)
            per_t.append(jnp.stack(blocks, axis=2))
        per_cls.append(jnp.stack(per_t, axis=1))
    b = jnp.stack(per_cls).reshape(5, H, NA_QROWS * W, NA_KBLKS * NA_QROWS * W)
    return jnp.concatenate([b, jnp.zeros(b.shape[:3] + (L,), F32)], axis=-1)


def _na(p_plain, bias, N, L):
    B = p_plain.shape[0]
    rows = NA_QROWS * GRID_W
    nblk = N // rows
    assert nblk >= NA_KBLKS + 2
    cls = lambda i: jnp.where(i < 2, i, jnp.where(i >= nblk - 2, i - (nblk - 5), 2))
    kv_spec = lambda tile, s: pl.BlockSpec(
        (1, rows, TN), lambda b, i: (b, jnp.clip(i - 2, 0, nblk - NA_KBLKS) + s, tile))
    nkeys = NA_KBLKS * rows + L
    return pl.pallas_call(
        _na_kernel,
        grid=(B, nblk),
        in_specs=[pl.BlockSpec((1, NA_HEADS, rows, nkeys), lambda b, i: (cls(i), 0, 0, 0)),
                  pl.BlockSpec((1, rows, TN), lambda b, i: (b, i, P_NQ))]
                 + [kv_spec(P_NK, s) for s in range(NA_KBLKS)]
                 + [kv_spec(P_NV, s) for s in range(NA_KBLKS)]
                 + [pl.BlockSpec((1, L, TN), lambda b, i: (b, N // L, P_NK)),
                    pl.BlockSpec((1, L, TN), lambda b, i: (b, N // L, P_NV))],
        out_specs=pl.BlockSpec((1, rows, TN), lambda b, i: (b, i, 0)),
        out_shape=jax.ShapeDtypeStruct((B, N, TN), BF16),
        compiler_params=_cparams(("arbitrary", "arbitrary")),
        name="neighbourhood_attn",
    )(bias, *([p_plain] * (2 * NA_KBLKS + 3)))


def _ctx_attend_kernel(sink_ref, q_ref, k_ref, v_ref, o_ref, *, gqa):
    q = q_ref[0]
    k = k_ref[0]
    v = v_ref[0]
    outs = []
    for blk in range(4):
        sl = slice(blk * LANES, (blk + 1) * LANES)
        qb = q[:, sl]
        kb, vb = (k, v) if gqa else (k[:, sl], v[:, sl])
        ob = None
        for hh in range(2):
            hm = _half_mask(hh)
            head = hh * 4 + blk if gqa else blk * 2 + hh
            s = _dot_nt(jnp.where(hm, qb, jnp.zeros_like(qb)), kb)
            sink = sink_ref[head]
            m = jnp.maximum(jnp.max(s, -1, keepdims=True), sink)
            p = jnp.exp(s - m)
            denom = jnp.sum(p, -1, keepdims=True) + jnp.exp(sink - m)
            o = _dot(p.astype(BF16), jnp.where(hm, vb, jnp.zeros_like(vb))) / denom
            ob = o if ob is None else ob + o
        outs.append(ob)
    o_ref[0] = jnp.concatenate(outs, axis=1).astype(BF16)


def _ctx_attend(p, sink, q_tile, k_blk, v_blk, N, L, *, gqa):
    B = p.shape[0]
    kw = LANES if gqa else TN
    rb = N // L
    return pl.pallas_call(
        functools.partial(_ctx_attend_kernel, gqa=gqa),
        grid=(B,),
        in_specs=[pl.BlockSpec(memory_space=pltpu.SMEM),
                  pl.BlockSpec((1, L, TN), lambda b: (b, rb, q_tile)),
                  pl.BlockSpec((1, L, kw), lambda b: (b, rb, k_blk)),
                  pl.BlockSpec((1, L, kw), lambda b: (b, rb, v_blk))],
        out_specs=pl.BlockSpec((1, L, TN), lambda b: (b, 0, 0)),
        out_shape=jax.ShapeDtypeStruct((B, L, TN), BF16),
        compiler_params=_cparams(("arbitrary",)),
        name="ctx_attend_gqa" if gqa else "ctx_attend",
    )(sink, p, p, p)


CONV_BLK = 256
HALO = 16


def _conv_kernel(prev_ref, cur_ref, next_ref, w_ref, b_ref, o_ref, *, lat_blocks):
    i = pl.program_id(1)
    last = pl.num_programs(1) - 1
    has_prev = (i != 0) & (i != lat_blocks)
    has_next = (i != lat_blocks - 1) & (i != last)
    prev = prev_ref[0].astype(F32) * has_prev.astype(F32)
    nxt = next_ref[0].astype(F32) * has_next.astype(F32)
    ext = jnp.concatenate([prev, cur_ref[0].astype(F32), nxt], axis=0)
    n = ext.shape[0]
    w = w_ref[...]
    acc = None
    for k in range(SSM_CONV):
        shift = (SSM_CONV // 2 - k) % n
        t = (ext if shift == 0 else pltpu.roll(ext, shift, 0))[HALO:HALO + CONV_BLK] * w[k:k + 1]
        acc = t if acc is None else acc + t
    acc = acc + b_ref[...]
    o_ref[0] = (acc * jax.nn.sigmoid(acc)).astype(BF16)


def _conv_silu(p_plain, conv_w, conv_b, N):
    B, T, _ = p_plain.shape
    C = SSM_XBC
    assert N % CONV_BLK == 0 and T % CONV_BLK == 0
    per = CONV_BLK // HALO
    nh = T // HALO
    w = jnp.zeros((8, C), F32).at[:SSM_CONV].set(conv_w.astype(F32))
    return pl.pallas_call(
        functools.partial(_conv_kernel, lat_blocks=N // CONV_BLK),
        grid=(B, T // CONV_BLK),
        in_specs=[pl.BlockSpec((1, HALO, C), lambda b, i: (b, jnp.maximum(i * per - 1, 0), 0)),
                  pl.BlockSpec((1, CONV_BLK, C), lambda b, i: (b, i, 0)),
                  pl.BlockSpec((1, HALO, C), lambda b, i: (b, jnp.minimum((i + 1) * per, nh - 1), 0)),
                  pl.BlockSpec((8, C), lambda b, i: (0, 0)),
                  pl.BlockSpec((1, C), lambda b, i: (0, 0))],
        out_specs=pl.BlockSpec((1, CONV_BLK, C), lambda b, i: (b, i, 0)),
        out_shape=jax.ShapeDtypeStruct((B, T, C), BF16),
        compiler_params=_cparams(("arbitrary", "arbitrary")),
        name="ssm_conv",
    )(p_plain, p_plain, p_plain, w, conv_b.reshape(1, C).astype(F32))


def _ssd_kernel(u_ref, dtr_ref, dtb_ref, alog_ref, tri_ref, exp_ref, y_ref, h_scr, *, fwd):
    @pl.when(pl.program_id(1) == 0)
    def _():
        h_scr[...] = jnp.zeros(h_scr.shape, F32)

    last = CHUNK - 1 if fwd else 0
    u = u_ref[0]
    xs = u[:, :SSM_INNER].astype(F32)
    bm = u[:, SSM_INNER:SSM_INNER + SSM_GROUPS * SSM_STATE]
    cm = u[:, SSM_INNER + SSM_GROUPS * SSM_STATE:]
    dt = jax.nn.softplus(dtr_ref[0] + dtb_ref[...])
    a = dt * (-jnp.exp(alog_ref[...]))
    acum = jnp.dot(tri_ref[...], a, precision=HIGHEST, preferred_element_type=F32)
    eac = jnp.exp(acum)
    dst = jnp.exp(acum[last:last + 1] - acum)
    expand = lambda t: jnp.dot(t, exp_ref[...], precision=HIGHEST, preferred_element_type=F32)
    dt_e, eac_e, dst_e = expand(dt), expand(eac), expand(dst)
    xg = xs * dt_e
    xg_b = xg.astype(BF16)
    xgd_b = (xg * dst_e).astype(BF16)
    acum_t = acum.T
    row = lax.broadcasted_iota(jnp.int32, (CHUNK, CHUNK), 0)
    col = lax.broadcasted_iota(jnp.int32, (CHUNK, CHUNK), 1)
    keep = (row >= col) if fwd else (row <= col)
    state = h_scr[...]
    state_b = state.astype(BF16)
    gw = SSM_INNER // SSM_GROUPS
    ys, st_new = [], []
    for g in range(SSM_GROUPS):
        bmg = bm[:, g * SSM_STATE:(g + 1) * SSM_STATE]
        cmg = cm[:, g * SSM_STATE:(g + 1) * SSM_STATE]
        cb = _dot_nt(cmg, bmg)
        yoff = _dot(cmg, state_b[:, g * gw:(g + 1) * gw])
        st_new.append(_dot_tn(bmg, xgd_b[:, g * gw:(g + 1) * gw]))
        for pb in range(gw // LANES):
            blk = g * (gw // LANES) + pb
            sl = slice(blk * LANES, (blk + 1) * LANES)
            xgb = xg_b[:, sl]
            yd = None
            for hh in range(2):
                h = blk * 2 + hh
                seg = acum[:, h:h + 1] - acum_t[h:h + 1, :]
                ld = jnp.exp(jnp.where(keep, seg, -jnp.inf))
                t = _dot((cb * ld).astype(BF16), jnp.where(_half_mask(hh), xgb, jnp.zeros_like(xgb)))
                yd = t if yd is None else yd + t
            ys.append(yd + yoff[:, pb * LANES:(pb + 1) * LANES] * eac_e[:, sl])
    y_ref[0] = jnp.concatenate(ys, axis=1)
    h_scr[...] = state * eac_e[last:last + 1] + jnp.concatenate(st_new, axis=1)


def _ssd(u, dtr, dt_bias, a_log, N, *, fwd):
    B, T, _ = u.shape
    nc, nlat = T // CHUNK, N // CHUNK
    ncx = nc - nlat
    d = 0 if fwd else 1
    if fwd:
        chunk = lambda c: jnp.where(c < ncx, nlat + c, c - ncx)
    else:
        chunk = lambda c: jnp.where(c < ncx, nlat + ncx - 1 - c, nlat - 1 - (c - ncx))
    pad = lambda t: jnp.zeros((1, LANES), F32).at[0, :SSM_HEADS].set(t[d].astype(F32))
    li = np.arange(CHUNK)
    tri = (li[:, None] >= li[None, :]) if fwd else (li[:, None] <= li[None, :])
    expm = np.zeros((LANES, SSM_INNER), np.float32)
    for h in range(SSM_HEADS):
        expm[h, h * 64:(h + 1) * 64] = 1.0
    return pl.pallas_call(
        functools.partial(_ssd_kernel, fwd=fwd),
        grid=(B, nc),
        in_specs=[pl.BlockSpec((1, CHUNK, SSM_XBC), lambda b, c: (b, chunk(c), 0)),
                  pl.BlockSpec((1, CHUNK, LANES), lambda b, c: (b, chunk(c), d)),
                  pl.BlockSpec((1, LANES), lambda b, c: (0, 0)),
                  pl.BlockSpec((1, LANES), lambda b, c: (0, 0)),
                  pl.BlockSpec((CHUNK, CHUNK), lambda b, c: (0, 0)),
                  pl.BlockSpec((LANES, SSM_INNER), lambda b, c: (0, 0))],
        out_specs=pl.BlockSpec((1, CHUNK, SSM_INNER), lambda b, c: (b, chunk(c), 0)),
        out_shape=jax.ShapeDtypeStruct((B, T, SSM_INNER), F32),
        scratch_shapes=[pltpu.VMEM((SSM_STATE, SSM_INNER), F32)],
        compiler_params=_cparams(("arbitrary", "arbitrary")),
        name="ssd_scan_fwd" if fwd else "ssd_scan_bwd",
    )(u, dtr, pad(dt_bias), pad(a_log), jnp.asarray(tri.astype(np.float32)), jnp.asarray(expm))


def _ssm_finish_kernel(yf_ref, yb_ref, u_ref, z_ref, dsk_ref, g_ref, o_ref):
    y = yf_ref[0] + yb_ref[0] + dsk_ref[...] * u_ref[0].astype(F32)
    z = z_ref[0].astype(F32)
    o_ref[0] = _rms(y * (z * jax.nn.sigmoid(z)), g_ref[...]).astype(BF16)


def _ssm_finish(y_f, y_b, u, p_plain, d_skip, g_ssm, *, tm=256):
    B, T, W = y_f.shape
    dsk = jnp.repeat(d_skip.astype(F32), W // SSM_HEADS).reshape(1, W)
    row_spec = pl.BlockSpec((1, tm, W), lambda b, i: (b, i, 0))
    return pl.pallas_call(
        _ssm_finish_kernel,
        grid=(B, T // tm),
        in_specs=[row_spec, row_spec, row_spec,
                  pl.BlockSpec((1, tm, W), lambda b, i: (b, i, P_BZ)),
                  pl.BlockSpec((1, W), lambda b, i: (0, 0)),
                  pl.BlockSpec((1, W), lambda b, i: (0, 0))],
        out_specs=row_spec,
        out_shape=jax.ShapeDtypeStruct((B, T, W), BF16),
        compiler_params=_cparams(("arbitrary", "arbitrary")),
        name="ssm_finish",
    )(y_f, y_b, u, p_plain, dsk, g_ssm.reshape(1, W).astype(F32))


def _merge_kernel(y0, y1, y2, y3, s0, s1, s2, s3, wb_ref, wo_ref, x_ref, gate_ref, g_ref, o_ref, acc):
    j = pl.program_id(2)
    m = None
    for k, (y, s) in enumerate(((y0, s0), (y1, s1), (y2, s2), (y3, s3))):
        t = _dot(y[0], wb_ref[k]) * s[0].astype(F32)
        m = t if m is None else m + t
    contrib = _dot(m.astype(BF16), wo_ref[...])

    @pl.when(j == 0)
    def _():
        acc[...] = contrib

    @pl.when(j > 0)
    def _():
        acc[...] += contrib

    @pl.when(j == pl.num_programs(2) - 1)
    def _():
        o_ref[0] = x_ref[0] + gate_ref[0] * _rms(acc[...], g_ref[...])


def _merge(ya, yb_all, ys, yn, p_gate, w_branch, w_out, x, gate, g_post, *, tm, row_off):
    B, Tx, D = x.shape
    tm = min(tm, Tx)
    nj = D // TN
    y_spec = pl.BlockSpec((1, tm, BRANCH_W), lambda b, i, j: (b, i, 0))
    yb_spec = pl.BlockSpec((1, tm, BRANCH_W), lambda b, i, j: (b, row_off + i, 0))
    s_spec = lambda k: pl.BlockSpec((1, tm, TN), lambda b, i, j: (b, row_off + i, k * nj + j))
    return pl.pallas_call(
        _merge_kernel,
        grid=(B, Tx // tm, nj),
        in_specs=[y_spec, yb_spec, y_spec, y_spec] + [s_spec(k) for k in range(N_BRANCH)]
                 + [pl.BlockSpec((N_BRANCH, BRANCH_W, TN), lambda b, i, j: (0, 0, j)),
                    pl.BlockSpec((TN, D), lambda b, i, j: (j, 0)),
                    pl.BlockSpec((1, tm, D), lambda b, i, j: (b, i, 0)),
                    pl.BlockSpec((1, 1, D), lambda b, i, j: (b, 0, 0)),
                    pl.BlockSpec((1, D), lambda b, i, j: (0, 0))],
        out_specs=pl.BlockSpec((1, tm, D), lambda b, i, j: (b, i, 0)),
        out_shape=jax.ShapeDtypeStruct((B, Tx, D), F32),
        scratch_shapes=[pltpu.VMEM((tm, D), F32)],
        compiler_params=_cparams(("arbitrary", "arbitrary", "arbitrary")),
        name="merge_out_proj",
    )(ya, yb_all, ys, yn, p_gate, p_gate, p_gate, p_gate, w_branch, w_out, x, gate, g_post)


def _ffn_kernel(x_ref, g1_ref, sh_ref, sc_ref, gate_ref, g2_ref, w1_ref, w2_ref, o_ref, h_scr, acc):
    f = pl.program_id(2)

    @pl.when(f == 0)
    def _():
        h = _rms(x_ref[0], g1_ref[...]) * (1.0 + sc_ref[0]) + sh_ref[0]
        h_scr[...] = h.astype(BF16)

    a = jnp.square(jnp.maximum(_dot(h_scr[...], w1_ref[...]), 0.0)).astype(BF16)
    contrib = _dot(a, w2_ref[...])

    @pl.when(f == 0)
    def _():
        acc[...] = contrib

    @pl.when(f > 0)
    def _():
        acc[...] += contrib

    @pl.when(f == pl.num_programs(2) - 1)
    def _():
        o_ref[0] = x_ref[0] + gate_ref[0] * _rms(acc[...], g2_ref[...])


def _ffn(x, g_pre, shift, scale, gate, g_post, w1, w2, *, tm, tf=1024):
    B, T, D = x.shape
    F = w1.shape[1]
    tm = min(tm, T)
    mod_spec = pl.BlockSpec((1, 1, D), lambda b, i, f: (b, 0, 0))
    vec_spec = pl.BlockSpec((1, D), lambda b, i, f: (0, 0))
    return pl.pallas_call(
        _ffn_kernel,
        grid=(B, T // tm, F // tf),
        in_specs=[pl.BlockSpec((1, tm, D), lambda b, i, f: (b, i, 0)),
                  vec_spec, mod_spec, mod_spec, mod_spec, vec_spec,
                  pl.BlockSpec((D, tf), lambda b, i, f: (0, f)),
                  pl.BlockSpec((tf, D), lambda b, i, f: (f, 0))],
        out_specs=pl.BlockSpec((1, tm, D), lambda b, i, f: (b, i, 0)),
        out_shape=jax.ShapeDtypeStruct((B, T, D), F32),
        scratch_shapes=[pltpu.VMEM((tm, D), BF16), pltpu.VMEM((tm, D), F32)],
        compiler_params=_cparams(("arbitrary", "arbitrary", "arbitrary")),
        name="mlp",
    )(x, g_pre, shift, scale, gate, g_post, w1, w2)


def _layout_w_in(w):
    D = w.shape[0]
    o = 0
    parts = {}
    for name, width in (("aq", 512), ("ak", 512), ("av", 512), ("bz", 512), ("bx", SSM_XBC), ("bdt", 16),
                        ("sq", 512), ("sk", 128), ("sv", 128), ("nq", 512), ("nk", 512), ("nv", 512),
                        ("gate", N_BRANCH * D)):
        parts[name] = w[:, o:o + width]
        o += width
    assert o == w.shape[1]
    sq = parts["sq"].reshape(D, SW_KV, 4, HEAD_DIM).transpose(0, 2, 1, 3).reshape(D, 512)
    zpad = jnp.zeros((D, LANES - SSM_HEADS), w.dtype)
    misc = jnp.concatenate([parts["sk"], parts["sv"], parts["bdt"][:, :SSM_HEADS], zpad,
                            parts["bdt"][:, SSM_HEADS:], zpad], axis=1)
    w_rope = jnp.concatenate([parts["aq"] * (QSCALE * LOG2E), parts["ak"], sq * QSCALE, misc], axis=1)
    w_plain = jnp.concatenate([parts["bx"], parts["nq"] * QSCALE, parts["nk"], parts["nv"], parts["av"],
                               parts["bz"]], axis=1)
    return w_rope.astype(BF16), w_plain.astype(BF16), parts["gate"].astype(BF16)


def _rope_tables(n, n_ctx):
    t = np.arange(n)
    nf = HEAD_DIM // 4
    inv = ROPE_BASE ** (-np.arange(nf, dtype=np.float32) / nf)
    row = (t // GRID_W).astype(np.float32)
    col = (t % GRID_W).astype(np.float32)
    ang = jnp.asarray(np.concatenate([row[:, None] * inv, col[:, None] * inv], -1).astype(np.float32))
    cos, sin = jnp.cos(ang), jnp.sin(ang)
    cos = jnp.concatenate([cos, jnp.ones((n_ctx, cos.shape[1]), F32)], axis=0)
    sin = jnp.concatenate([sin, jnp.zeros((n_ctx, sin.shape[1]), F32)], axis=0)
    z = jnp.zeros_like(sin)
    rep = LANES // HEAD_DIM
    cos_t = jnp.tile(jnp.concatenate([cos, cos], -1), (1, rep))
    s1 = jnp.tile(jnp.concatenate([-sin, z], -1), (1, rep))
    s2 = jnp.tile(jnp.concatenate([z, sin], -1), (1, rep))
    return cos_t, s1, s2


def kernel(x, c, ctx, c_ctx, w_mod, b_mod, g_pre_mix, g_post_mix, g_pre_mlp, g_post_mlp, w_in, lam_q1, lam_k1, lam_q2, lam_k2, g_subln, conv_w, conv_b, dt_bias, a_log, d_skip, g_ssm, sink, rpb, w_branch, w_out, w_ff1, w_ff2):
    B, N, D = x.shape
    L = ctx.shape[1]
    T = N + L
    depth = w_mod.shape[0]
    assert B + 1 <= 8 and N % GRID_W == 0 and N % L == 0
    R = N // GRID_W

    cvec = jnp.zeros((8, D), F32).at[:B].set(c).at[B].set(c_ctx)
    mods = _modulation(cvec, w_mod, b_mod)
    rope_tabs = _rope_tables(N, L)
    no_sink = jnp.full((NA_HEADS,), NEG, F32)
    row = lambda v: v.reshape(1, -1).astype(F32)
    tk_lat = _largest_divisor(T, (768, 512, 256, 128))

    cx = ctx
    for l in range(depth):
        need_ctx = l < depth - 1
        lam_init = 0.8 - 0.6 * math.exp(-0.3 * l)
        mod = mods[l, :B].reshape(B, 6, 1, D)
        modc = jnp.broadcast_to(mods[l, B].reshape(1, 6, 1, D), (B, 6, 1, D))
        both = lambda k: jnp.stack([mod[:, k], modc[:, k]], axis=1).reshape(B * 2, 1, D)
        w_rope, w_plain, w_gate = _layout_w_in(w_in[l])

        h = _prenorm(x, cx, row(g_pre_mix[l]), both(0), both(1))
        p_rope, dtr = _proj_rope(h, rope_tabs, w_rope)
        p_plain = _proj(h, w_plain, gate=False, tn=TN)
        p_gate = _proj(h, w_gate, gate=True, tn=1024)
        col = lambda p, t: p[:, :, t * TN:(t + 1) * TN]

        lamv = jnp.stack([lam_q1[l], lam_k1[l], lam_q2[l], lam_k2[l]]).astype(F32)
        g_col = g_subln[l].reshape(LANES, 1).astype(F32)
        aq, av = col(p_rope, R_AQ), col(p_plain, P_AV)
        ya = _diff_attn(aq[:, :N].transpose(0, 2, 1), p_rope, _transpose_chunks(av, tk_lat), lamv, g_col,
                        lam_init, row_blk0=0, tq=512)

        u = _conv_silu(p_plain, conv_w[l], conv_b[l], N)
        y_f = _ssd(u, dtr, dt_bias[l], a_log[l], N, fwd=True)
        y_b = _ssd(u, dtr, dt_bias[l], a_log[l], N, fwd=False)
        yb_all = _ssm_finish(y_f, y_b, u, p_plain, d_skip[l], g_ssm[l])

        ys = _swa(p_rope, sink[l].astype(F32), N, L)
        yn = _na(p_plain, _na_bias(rpb[l], R, L), N, L)

        wb = w_branch[l]
        wb = wb.at[2].set(wb[2].reshape(SW_KV, 4, HEAD_DIM, D).transpose(1, 0, 2, 3).reshape(BRANCH_W, D))
        wb = wb.astype(BF16)
        wo = w_out[l].astype(BF16)
        w1 = w_ff1[l].astype(BF16)
        w2 = w_ff2[l].astype(BF16)
        x_new = _merge(ya, yb_all, ys, yn, p_gate, wb, wo, x, mod[:, 2], row(g_post_mix[l]), tm=512, row_off=0)
        x_new = _ffn(x_new, row(g_pre_mlp[l]), mod[:, 3], mod[:, 4], mod[:, 5], row(g_post_mlp[l]), w1, w2, tm=512)
        if need_ctx:
            ya_c = _diff_attn(aq[:, N:].transpose(0, 2, 1), p_rope, _transpose_chunks(av[:, N:], L), lamv, g_col,
                              lam_init, row_blk0=N // L, tq=L)
            ys_c = _ctx_attend(p_rope, sink[l].astype(F32), R_SQ, R_MISC * 4, R_MISC * 4 + 1, N, L, gqa=True)
            yn_c = _ctx_attend(p_plain, no_sink, P_NQ, P_NK, P_NV, N, L, gqa=False)
            cx = _merge(ya_c, yb_all, ys_c, yn_c, p_gate, wb, wo, cx, modc[:, 2], row(g_post_mix[l]),
                        tm=L, row_off=N // L)
            cx = _ffn(cx, row(g_pre_mlp[l]), modc[:, 3], modc[:, 4], modc[:, 5], row(g_post_mlp[l]), w1, w2, tm=L)
        x = x_new
    return x
```

```python
import functools
import math

import numpy as np
import jax
import jax.numpy as jnp
from jax import lax
from jax.experimental import pallas as pl
from jax.experimental.pallas import tpu as pltpu

F32 = jnp.float32
BF16 = jnp.bfloat16

LANES = 128
VMEM_LIMIT = 56 * 1024 * 1024
NEG = -1e30
EPS = 1e-6
GRID_W = 64
ROPE_BASE = 10000.0
HEAD_DIM = 64
QSCALE = HEAD_DIM ** -0.5
LOG2E = math.log2(math.e)
DA_HEADS = 4
SSM_HEADS = 8
SSM_INNER = 512
SSM_GROUPS = 2
SSM_STATE = 128
SSM_XBC = SSM_INNER + 2 * SSM_GROUPS * SSM_STATE
SSM_CONV = 5
CHUNK = 128
SW_KV = 2
SW_BLOCK = 128
NA_HEADS = 8
NA_ROWS = 8
NA_COLS = 16
N_BRANCH = 4
BRANCH_W = 512

TN = 512
R_AQ, R_AK, R_SQ, R_MISC = 0, 1, 2, 3
P_NQ, P_NK, P_NV, P_AV, P_BZ = 2, 3, 4, 5, 6


def _cparams(sem):
    return pltpu.CompilerParams(dimension_semantics=sem, vmem_limit_bytes=VMEM_LIMIT)


def _rms(t, g):
    return t * lax.rsqrt(jnp.mean(t * t, -1, keepdims=True) + EPS) * g


def _dot(a, b):
    return jnp.dot(a, b, preferred_element_type=F32)


def _dot_nt(a, b):
    return lax.dot_general(a, b, (((1,), (1,)), ((), ())), preferred_element_type=F32)


def _dot_tn(a, b):
    return lax.dot_general(a, b, (((0,), (0,)), ((), ())), preferred_element_type=F32)


def _half_mask(hh):
    lane = lax.broadcasted_iota(jnp.int32, (1, LANES), 1)
    return (lane // HEAD_DIM) == hh


def _largest_divisor(n, cands):
    for c in cands:
        if n % c == 0:
            return c
    raise ValueError(f"no tile in {cands} divides {n}")


MOD_KB = 256


def _mod_kernel(c_ref, w_ref, b_ref, o_ref):
    k = pl.program_id(1)
    c = c_ref[...]
    part = _dot((c * jax.nn.sigmoid(c)).astype(BF16), w_ref[0].astype(BF16))

    @pl.when(k == 0)
    def _():
        o_ref[0] = part + b_ref[0]

    @pl.when(k > 0)
    def _():
        o_ref[0] += part


def _modulation(cvec, w_mod, b_mod):
    depth, D, C = w_mod.shape
    return pl.pallas_call(
        _mod_kernel,
        grid=(depth, D // MOD_KB),
        in_specs=[pl.BlockSpec((8, MOD_KB), lambda l, k: (0, k)),
                  pl.BlockSpec((1, MOD_KB, C), lambda l, k: (l, k, 0)),
                  pl.BlockSpec((1, 1, C), lambda l, k: (l, 0, 0))],
        out_specs=pl.BlockSpec((1, 8, C), lambda l, k: (l, 0, 0)),
        out_shape=jax.ShapeDtypeStruct((depth, 8, C), F32),
        compiler_params=_cparams(("arbitrary", "arbitrary")),
        name="modulation",
    )(cvec, w_mod, b_mod.reshape(depth, 1, C))


NORM_BLK = 256


def _prenorm_kernel(x_ref, cx_ref, g_ref, sh_ref, sc_ref, o_ref, *, nlat):
    i = pl.program_id(1)

    def emit(t):
        o_ref[0] = (_rms(t, g_ref[...]) * (1.0 + sc_ref[0]) + sh_ref[0]).astype(BF16)

    @pl.when(i < nlat)
    def _():
        emit(x_ref[0])

    @pl.when(i >= nlat)
    def _():
        emit(cx_ref[0])


def _prenorm(x, cx, g, shift2, scale2):
    B, N, D = x.shape
    L = cx.shape[1]
    assert N % NORM_BLK == 0 and L % NORM_BLK == 0
    nlat, nctx = N // NORM_BLK, L // NORM_BLK
    mod_spec = pl.BlockSpec((1, 1, D), lambda b, i: (b * 2 + (i >= nlat).astype(jnp.int32), 0, 0))
    return pl.pallas_call(
        functools.partial(_prenorm_kernel, nlat=nlat),
        grid=(B, nlat + nctx),
        in_specs=[pl.BlockSpec((1, NORM_BLK, D), lambda b, i: (b, jnp.minimum(i, nlat - 1), 0)),
                  pl.BlockSpec((1, NORM_BLK, D), lambda b, i: (b, jnp.maximum(i - nlat, 0), 0)),
                  pl.BlockSpec((1, D), lambda b, i: (0, 0)),
                  mod_spec, mod_spec],
        out_specs=pl.BlockSpec((1, NORM_BLK, D), lambda b, i: (b, i, 0)),
        out_shape=jax.ShapeDtypeStruct((B, N + L, D), BF16),
        compiler_params=_cparams(("arbitrary", "arbitrary")),
        name="prenorm",
    )(x, cx, g, shift2, scale2)


def _rope(t, cos, s1, s2):
    w = t.shape[1]
    reps = w // LANES
    tile = lambda a: a if reps == 1 else jnp.concatenate([a] * reps, axis=1)
    return (t * tile(cos) + pltpu.roll(t, w - HEAD_DIM // 2, 1) * tile(s1)
            + pltpu.roll(t, HEAD_DIM // 2, 1) * tile(s2))


def _proj_rope_kernel(h_ref, cos_ref, s1_ref, s2_ref, w_ref, o_ref, odt_ref):
    j = pl.program_id(2)
    acc = _dot(h_ref[0], w_ref[...])
    rot = lambda t: _rope(t, cos_ref[...], s1_ref[...], s2_ref[...])

    @pl.when(j != R_MISC)
    def _():
        o_ref[0] = rot(acc).astype(BF16)

    @pl.when(j == R_MISC)
    def _():
        o_ref[0] = jnp.concatenate([rot(acc[:, :LANES]), acc[:, LANES:]], axis=1).astype(BF16)
        odt_ref[0] = acc[:, 2 * LANES:]


def _row_tile(T):
    return _largest_divisor(T, (1056, 1024, 768, 640, 512, 256, 128))


def _proj_rope(h, rope_tabs, w):
    B, T, D = h.shape
    C = w.shape[1]
    tm = _row_tile(T)
    cos, s1, s2 = rope_tabs
    tab_spec = pl.BlockSpec((tm, LANES), lambda b, i, j: (i, 0))
    return pl.pallas_call(
        _proj_rope_kernel,
        grid=(B, T // tm, C // TN),
        in_specs=[pl.BlockSpec((1, tm, D), lambda b, i, j: (b, i, 0)),
                  tab_spec, tab_spec, tab_spec,
                  pl.BlockSpec((D, TN), lambda b, i, j: (0, j))],
        out_specs=[pl.BlockSpec((1, tm, TN), lambda b, i, j: (b, i, j)),
                   pl.BlockSpec((1, tm, 2 * LANES), lambda b, i, j: (b, i, 0))],
        out_shape=[jax.ShapeDtypeStruct((B, T, C), BF16),
                   jax.ShapeDtypeStruct((B, T, 2 * LANES), F32)],
        compiler_params=_cparams(("arbitrary", "arbitrary", "arbitrary")),
        name="proj_rope",
    )(h, cos, s1, s2, w)


def _proj_kernel(h_ref, w_ref, o_ref, *, gate):
    acc = _dot(h_ref[0], w_ref[...])
    o_ref[0] = (jax.nn.sigmoid(acc) if gate else acc).astype(BF16)


def _proj(h, w, *, gate, tn):
    B, T, D = h.shape
    C = w.shape[1]
    tm = _row_tile(T)
    return pl.pallas_call(
        functools.partial(_proj_kernel, gate=gate),
        grid=(B, T // tm, C // tn),
        in_specs=[pl.BlockSpec((1, tm, D), lambda b, i, j: (b, i, 0)),
                  pl.BlockSpec((D, tn), lambda b, i, j: (0, j))],
        out_specs=pl.BlockSpec((1, tm, tn), lambda b, i, j: (b, i, j)),
        out_shape=jax.ShapeDtypeStruct((B, T, C), BF16),
        compiler_params=_cparams(("arbitrary", "arbitrary", "arbitrary")),
        name="proj_gate" if gate else "proj_plain",
    )(h, w)


VT_ROWS = LANES + 16


def _diff_attn_kernel(lam_ref, g_ref, q_ref, k_ref, v_ref, o_ref, s_scr, *, lam_init, tk):
    q_t = q_ref[0]
    tq = q_t.shape[1]
    nk = v_ref.shape[2]
    dim = lax.broadcasted_iota(jnp.int32, (LANES, tq), 0)
    q_m = [jnp.where((dim // HEAD_DIM) == m, q_t, jnp.zeros_like(q_t)) for m in range(2)]

    def scores(c):
        kc = k_ref[0, c * tk:(c + 1) * tk, :]
        for m in range(2):
            s_scr[c % 2, m] = _dot(kc, q_m[m])

    carry = [(jnp.full((1, tq), -jnp.inf, F32), jnp.zeros((VT_ROWS, tq), F32)) for _ in range(2)]
    scores(0)
    for c in range(nk):
        if c + 1 < nk:
            scores(c + 1)
        vc = v_ref[0, 0, c]
        for m in range(2):
            mx, acc = carry[m]
            m_new = jnp.maximum(mx, jnp.max(s_scr[c % 2, m], 0, keepdims=True))
            p = jnp.exp2((s_scr[c % 2, m] - m_new).astype(BF16))
            acc = jnp.exp2(mx - m_new) * acc + _dot(vc, p)
            carry[m] = (m_new, acc)
    (_, a0), (_, a1) = carry
    lv = lam_ref[...]
    lam = (jnp.exp(jnp.sum(lv[0:1] * lv[1:2], -1, keepdims=True))
           - jnp.exp(jnp.sum(lv[2:3] * lv[3:4], -1, keepdims=True)) + lam_init)
    o = a0[:LANES] / a0[LANES:LANES + 1] - lam * (a1[:LANES] / a1[LANES:LANES + 1])
    o = o * lax.rsqrt(jnp.mean(o * o, 0, keepdims=True) + EPS) * g_ref[...] * (1.0 - lam_init)
    o_ref[0] = o.T.astype(BF16)


def _diff_attn(q_t, p_rope, v_t, lamv, g_col, lam_init, *, row_blk0, tq):
    B, _, Tq = q_t.shape
    _, _, nk, _, tk = v_t.shape
    Tk = nk * tk
    tq = min(tq, Tq)
    return pl.pallas_call(
        functools.partial(_diff_attn_kernel, lam_init=lam_init, tk=tk),
        grid=(B, DA_HEADS, Tq // tq),
        in_specs=[pl.BlockSpec((4, HEAD_DIM), lambda b, h, i: (0, 0)),
                  pl.BlockSpec((LANES, 1), lambda b, h, i: (0, 0)),
                  pl.BlockSpec((1, LANES, tq), lambda b, h, i: (b, h, i)),
                  pl.BlockSpec((1, Tk, LANES), lambda b, h, i: (b, row_blk0, R_AK * 4 + h)),
                  pl.BlockSpec((1, 1, nk, VT_ROWS, tk), lambda b, h, i: (b, h, 0, 0, 0))],
        out_specs=pl.BlockSpec((1, tq, LANES), lambda b, h, i: (b, i, h)),
        out_shape=jax.ShapeDtypeStruct((B, Tq, DA_HEADS * LANES), BF16),
        scratch_shapes=[pltpu.VMEM((2, 2, tk, tq), F32)],
        compiler_params=_cparams(("arbitrary",) * 3),
        name="diff_attn",
    )(lamv, g_col, q_t, p_rope, v_t)


def _transpose_chunks(v, tk):
    B, Tk, _ = v.shape
    nk = Tk // tk
    vt = v.reshape(B, nk, tk, DA_HEADS, LANES).transpose(0, 3, 1, 4, 2)
    extra = jnp.zeros((B, DA_HEADS, nk, VT_ROWS - LANES, tk), v.dtype).at[:, :, :, 0].set(1)
    return jnp.concatenate([vt, extra], axis=3)


def _swa_kernel(sink_ref, q_ref, kp_ref, kc_ref, kn_ref, vp_ref, vc_ref, vn_ref, kx_ref, vx_ref, o_ref):
    i = pl.program_id(1)
    nb = pl.num_programs(1)
    blk = SW_BLOCK
    kband = jnp.concatenate([kp_ref[0], kc_ref[0], kn_ref[0], kx_ref[0]], axis=0)
    vband = jnp.concatenate([vp_ref[0], vc_ref[0], vn_ref[0], vx_ref[0]], axis=0)
    nkeys = kband.shape[0]
    row = lax.broadcasted_iota(jnp.int32, (blk, nkeys), 0)
    col = lax.broadcasted_iota(jnp.int32, (blk, nkeys), 1)
    big = 4 * blk
    off_prev = jnp.where(i > 0, 0, big)
    off_next = jnp.where(i < nb - 1, 0, big)
    ok_prev = jnp.where(col >= row + off_prev, 0.0, NEG)
    ok_next = jnp.where(col - 2 * blk <= row - off_next, 0.0, NEG)
    bias = jnp.where(col < blk, ok_prev, jnp.where((col >= 2 * blk) & (col < 3 * blk), ok_next, 0.0))
    bias = jnp.concatenate([bias] * 4, axis=0)
    q = q_ref[0]
    outs = [None] * 4

    def scores(kv):
        qs = jnp.concatenate([jnp.where(_half_mask(kv), q[:, g * LANES:(g + 1) * LANES],
                                        jnp.zeros((blk, LANES), BF16)) for g in range(4)], axis=0)
        return _dot_nt(qs, kband) + bias

    s_all = [scores(kv) for kv in range(SW_KV)]
    for kv in range(SW_KV):
        hm = _half_mask(kv)
        s = s_all[kv]
        sink = jnp.concatenate([jnp.full((blk, 1), sink_ref[kv * 4 + g], F32) for g in range(4)], axis=0)
        m = jnp.maximum(jnp.max(s, -1, keepdims=True), sink)
        p = jnp.exp(s - m)
        denom = jnp.sum(p, -1, keepdims=True) + jnp.exp(sink - m)
        o = _dot(p.astype(BF16), jnp.where(hm, vband, jnp.zeros_like(vband))) / denom
        for g in range(4):
            part = o[g * blk:(g + 1) * blk]
            outs[g] = part if outs[g] is None else outs[g] + part
    o_ref[0] = jnp.concatenate(outs, axis=1).astype(BF16)


def _swa(p_rope, sink, N, L):
    B = p_rope.shape[0]
    nb = N // SW_BLOCK
    kcol, vcol = R_MISC * 4, R_MISC * 4 + 1
    kv_spec = lambda colblk, off: pl.BlockSpec(
        (1, SW_BLOCK, LANES), lambda b, i: (b, jnp.clip(i + off, 0, nb - 1), colblk))
    return pl.pallas_call(
        _swa_kernel,
        grid=(B, nb),
        in_specs=[pl.BlockSpec(memory_space=pltpu.SMEM),
                  pl.BlockSpec((1, SW_BLOCK, TN), lambda b, i: (b, i, R_SQ)),
                  kv_spec(kcol, -1), kv_spec(kcol, 0), kv_spec(kcol, 1),
                  kv_spec(vcol, -1), kv_spec(vcol, 0), kv_spec(vcol, 1),
                  pl.BlockSpec((1, L, LANES), lambda b, i: (b, N // L, kcol)),
                  pl.BlockSpec((1, L, LANES), lambda b, i: (b, N // L, vcol))],
        out_specs=pl.BlockSpec((1, SW_BLOCK, TN), lambda b, i: (b, i, 0)),
        out_shape=jax.ShapeDtypeStruct((B, N, TN), BF16),
        compiler_params=_cparams(("arbitrary", "arbitrary")),
        name="window_attn",
    )(sink, *([p_rope] * 9))


NA_QROWS = 2
NA_KBLKS = 5


def _na_kernel(bias_ref, q_ref, k0, k1, k2, k3, k4, v0, v1, v2, v3, v4, kx_ref, vx_ref, o_ref):
    kall = jnp.concatenate([k0[0], k1[0], k2[0], k3[0], k4[0], kx_ref[0]], axis=0)
    vall = jnp.concatenate([v0[0], v1[0], v2[0], v3[0], v4[0], vx_ref[0]], axis=0)
    q = q_ref[0]
    blocks = [slice(pb * LANES, (pb + 1) * LANES) for pb in range(4)]
    scores = [[_dot_nt(jnp.where(_half_mask(hh), q[:, sl], jnp.zeros((q.shape[0], LANES), BF16)), kall[:, sl])
               + bias_ref[0, pb * 2 + hh] for hh in range(2)] for pb, sl in enumerate(blocks)]
    outs = []
    for pb, sl in enumerate(blocks):
        vb = vall[:, sl]
        ob = None
        for hh in range(2):
            hm = _half_mask(hh)
            s = scores[pb][hh]
            m = jnp.max(s, -1, keepdims=True)
            p = jnp.exp(s - m)
            o = _dot(p.astype(BF16), jnp.where(hm, vb, jnp.zeros_like(vb))) / jnp.sum(p, -1, keepdims=True)
            ob = o if ob is None else ob + o
        outs.append(ob)
    o_ref[0] = jnp.concatenate(outs, axis=1).astype(BF16)


def _col_bias_kernel(r_ref, sel_ref, neg_ref, o_ref):
    o_ref[...] = sum(_dot(part, sel_ref[...]) for part in reversed(_split3(r_ref[...]))) + neg_ref[...]


def _na_col_bias(rpb_all):
    W = GRID_W
    nco = 2 * NA_COLS - 1
    lead = rpb_all.shape[:3]
    rows = int(np.prod(lead))
    c = np.arange(W)
    cb = np.clip(c - NA_COLS // 2, 0, W - NA_COLS)
    dc = c[None, :] - cb[:, None]
    inside = (dc >= 0) & (dc < NA_COLS)
    co = c[None, :] - c[:, None] + NA_COLS - 1
    sel = ((co[None] == np.arange(32)[:, None, None]) & inside[None]).reshape(32, W * W)
    neg = np.where(inside, 0.0, NEG).reshape(1, W * W).astype(np.float32)
    table = jnp.pad(rpb_all.astype(F32).reshape(rows, nco), ((0, (-rows) % 8), (0, 32 - nco)))
    out = pl.pallas_call(
        _col_bias_kernel,
        out_shape=jax.ShapeDtypeStruct((table.shape[0], W * W), F32),
        name="na_col_bias",
    )(table, jnp.asarray(sel, BF16), jnp.asarray(neg))
    return out[:rows].reshape(lead + (W, W))


def _na_bias(t2, R, L):
    W = GRID_W
    H = t2.shape[0]
    clamp = lambda v, lo, hi: min(max(v, lo), hi)
    masked = jnp.full((H, W, W), NEG, F32)
    nblk = R // NA_QROWS
    per_cls = []
    for i in (0, 1, 2, nblk - 2, nblk - 1):
        base_blk = clamp(i - 2, 0, nblk - NA_KBLKS)
        per_t = []
        for t in range(NA_QROWS):
            rq = NA_QROWS * i + t
            base = clamp(rq - NA_ROWS // 2, 0, R - NA_ROWS)
            blocks = []
            for s in range(NA_KBLKS * NA_QROWS):
                rk = NA_QROWS * base_blk + s
                blocks.append(t2[:, rk - rq + NA_ROWS - 1] if 0 <= rk - base < NA_ROWS else masked)
            per_t.append(jnp.stack(blocks, axis=2))
        per_cls.append(jnp.stack(per_t, axis=1))
    b = jnp.stack(per_cls).reshape(5, H, NA_QROWS * W, NA_KBLKS * NA_QROWS * W)
    return jnp.concatenate([b, jnp.zeros(b.shape[:3] + (L,), F32)], axis=-1)


def _na(p_plain, bias, N, L):
    B = p_plain.shape[0]
    rows = NA_QROWS * GRID_W
    nblk = N // rows
    assert nblk >= NA_KBLKS + 2
    cls = lambda i: jnp.where(i < 2, i, jnp.where(i >= nblk - 2, i - (nblk - 5), 2))
    kv_spec = lambda tile, s: pl.BlockSpec(
        (1, rows, TN), lambda b, i: (b, jnp.clip(i - 2, 0, nblk - NA_KBLKS) + s, tile))
    nkeys = NA_KBLKS * rows + L
    return pl.pallas_call(
        _na_kernel,
        grid=(B, nblk),
        in_specs=[pl.BlockSpec((1, NA_HEADS, rows, nkeys), lambda b, i: (cls(i), 0, 0, 0)),
                  pl.BlockSpec((1, rows, TN), lambda b, i: (b, i, P_NQ))]
                 + [kv_spec(P_NK, s) for s in range(NA_KBLKS)]
                 + [kv_spec(P_NV, s) for s in range(NA_KBLKS)]
                 + [pl.BlockSpec((1, L, TN), lambda b, i: (b, N // L, P_NK)),
                    pl.BlockSpec((1, L, TN), lambda b, i: (b, N // L, P_NV))],
        out_specs=pl.BlockSpec((1, rows, TN), lambda b, i: (b, i, 0)),
        out_shape=jax.ShapeDtypeStruct((B, N, TN), BF16),
        compiler_params=_cparams(("arbitrary", "arbitrary")),
        name="neighbourhood_attn",
    )(bias, *([p_plain] * (2 * NA_KBLKS + 3)))


def _ctx_attend_kernel(sink_ref, q_ref, k_ref, v_ref, o_ref, *, gqa):
    q = q_ref[0]
    k = k_ref[0]
    v = v_ref[0]
    outs = []
    for blk in range(4):
        sl = slice(blk * LANES, (blk + 1) * LANES)
        qb = q[:, sl]
        kb, vb = (k, v) if gqa else (k[:, sl], v[:, sl])
        ob = None
        for hh in range(2):
            hm = _half_mask(hh)
            head = hh * 4 + blk if gqa else blk * 2 + hh
            s = _dot_nt(jnp.where(hm, qb, jnp.zeros_like(qb)), kb)
            sink = sink_ref[head]
            m = jnp.maximum(jnp.max(s, -1, keepdims=True), sink)
            p = jnp.exp(s - m)
            denom = jnp.sum(p, -1, keepdims=True) + jnp.exp(sink - m)
            o = _dot(p.astype(BF16), jnp.where(hm, vb, jnp.zeros_like(vb))) / denom
            ob = o if ob is None else ob + o
        outs.append(ob)
    o_ref[0] = jnp.concatenate(outs, axis=1).astype(BF16)


def _ctx_attend(p, sink, q_tile, k_blk, v_blk, N, L, *, gqa):
    B = p.shape[0]
    kw = LANES if gqa else TN
    rb = N // L
    return pl.pallas_call(
        functools.partial(_ctx_attend_kernel, gqa=gqa),
        grid=(B,),
        in_specs=[pl.BlockSpec(memory_space=pltpu.SMEM),
                  pl.BlockSpec((1, L, TN), lambda b: (b, rb, q_tile)),
                  pl.BlockSpec((1, L, kw), lambda b: (b, rb, k_blk)),
                  pl.BlockSpec((1, L, kw), lambda b: (b, rb, v_blk))],
        out_specs=pl.BlockSpec((1, L, TN), lambda b: (b, 0, 0)),
        out_shape=jax.ShapeDtypeStruct((B, L, TN), BF16),
        compiler_params=_cparams(("arbitrary",)),
        name="ctx_attend_gqa" if gqa else "ctx_attend",
    )(sink, p, p, p)


CONV_BLK = 256
HALO = 16


def _conv_kernel(prev_ref, cur_ref, next_ref, w_ref, b_ref, o_ref, *, lat_blocks):
    i = pl.program_id(1)
    last = pl.num_programs(1) - 1
    has_prev = (i != 0) & (i != lat_blocks)
    has_next = (i != lat_blocks - 1) & (i != last)
    prev = prev_ref[0].astype(F32) * has_prev.astype(F32)
    nxt = next_ref[0].astype(F32) * has_next.astype(F32)
    ext = jnp.concatenate([prev, cur_ref[0].astype(F32), nxt], axis=0)
    n = ext.shape[0]
    w = w_ref[...]
    acc = None
    for k in range(SSM_CONV):
        shift = (SSM_CONV // 2 - k) % n
        t = (ext if shift == 0 else pltpu.roll(ext, shift, 0))[HALO:HALO + CONV_BLK] * w[k:k + 1]
        acc = t if acc is None else acc + t
    acc = acc + b_ref[...]
    o_ref[0] = (acc * jax.nn.sigmoid(acc)).astype(BF16)


def _conv_silu(p_plain, conv_w, conv_b, N):
    B, T, _ = p_plain.shape
    C = SSM_XBC
    assert N % CONV_BLK == 0 and T % CONV_BLK == 0
    per = CONV_BLK // HALO
    nh = T // HALO
    w = jnp.zeros((8, C), F32).at[:SSM_CONV].set(conv_w.astype(F32))
    return pl.pallas_call(
        functools.partial(_conv_kernel, lat_blocks=N // CONV_BLK),
        grid=(B, T // CONV_BLK),
        in_specs=[pl.BlockSpec((1, HALO, C), lambda b, i: (b, jnp.maximum(i * per - 1, 0), 0)),
                  pl.BlockSpec((1, CONV_BLK, C), lambda b, i: (b, i, 0)),
                  pl.BlockSpec((1, HALO, C), lambda b, i: (b, jnp.minimum((i + 1) * per, nh - 1), 0)),
                  pl.BlockSpec((8, C), lambda b, i: (0, 0)),
                  pl.BlockSpec((1, C), lambda b, i: (0, 0))],
        out_specs=pl.BlockSpec((1, CONV_BLK, C), lambda b, i: (b, i, 0)),
        out_shape=jax.ShapeDtypeStruct((B, T, C), BF16),
        compiler_params=_cparams(("arbitrary", "arbitrary")),
        name="ssm_conv",
    )(p_plain, p_plain, p_plain, w, conv_b.reshape(1, C).astype(F32))


def _split3(t):
    hi = t.astype(BF16)
    r = t - hi.astype(F32)
    mid = r.astype(BF16)
    return hi, mid, (r - mid.astype(F32)).astype(BF16)


def _ssd_chain(u, dtr, dtb, alog, tri, expm, state, *, fwd):
    last = CHUNK - 1 if fwd else 0
    xs = u[:, :SSM_INNER].astype(F32)
    bm = u[:, SSM_INNER:SSM_INNER + SSM_GROUPS * SSM_STATE]
    cm = u[:, SSM_INNER + SSM_GROUPS * SSM_STATE:]
    dt = jax.nn.softplus(dtr + dtb)
    a = dt * (-jnp.exp(alog))
    acum = sum(_dot(tri, part) for part in reversed(_split3(a)))
    eac = jnp.exp(acum)
    dst = jnp.exp(acum[last:last + 1] - acum)
    e3 = sum(_dot(part, expm) for part in reversed(_split3(jnp.concatenate([dt, eac, dst], axis=0))))
    dt_e, eac_e, dst_e = e3[:CHUNK], e3[CHUNK:2 * CHUNK], e3[2 * CHUNK:]
    xg = xs * dt_e
    xg_b = xg.astype(BF16)
    xgd_b = (xg * dst_e).astype(BF16)
    acum_t = acum.T
    row = lax.broadcasted_iota(jnp.int32, (CHUNK, CHUNK), 0)
    col = lax.broadcasted_iota(jnp.int32, (CHUNK, CHUNK), 1)
    keep = (row >= col) if fwd else (row <= col)
    state_b = state.astype(BF16)
    gw = SSM_INNER // SSM_GROUPS
    ys, st_new = [], []
    for g in range(SSM_GROUPS):
        bmg = bm[:, g * SSM_STATE:(g + 1) * SSM_STATE]
        cmg = cm[:, g * SSM_STATE:(g + 1) * SSM_STATE]
        cb = _dot_nt(cmg, bmg)
        yoff = _dot(cmg, state_b[:, g * gw:(g + 1) * gw])
        st_new.append(_dot_tn(bmg, xgd_b[:, g * gw:(g + 1) * gw]))
        for pb in range(gw // LANES):
            blk = g * (gw // LANES) + pb
            sl = slice(blk * LANES, (blk + 1) * LANES)
            xgb = xg_b[:, sl]
            yd = None
            for hh in range(2):
                h = blk * 2 + hh
                seg = acum[:, h:h + 1] - acum_t[h:h + 1, :]
                ld = jnp.exp(jnp.where(keep, seg, -jnp.inf))
                t = _dot((cb * ld).astype(BF16), jnp.where(_half_mask(hh), xgb, jnp.zeros_like(xgb)))
                yd = t if yd is None else yd + t
            ys.append(yd + yoff[:, pb * LANES:(pb + 1) * LANES] * eac_e[:, sl])
    y = jnp.concatenate(ys, axis=1)
    return y, state * eac_e[last:last + 1] + jnp.concatenate(st_new, axis=1)


def _ssd_kernel(uf_ref, ub_ref, dtf_ref, dtb_ref, bias_ref, alog_ref, tri_ref, exp_ref, yf_ref, yb_ref, h_scr):
    @pl.when(pl.program_id(0) == 0)
    def _():
        h_scr[...] = jnp.zeros(h_scr.shape, F32)

    for b in range(uf_ref.shape[0]):
        for d, (u_ref, dt_ref, y_ref) in enumerate(((uf_ref, dtf_ref, yf_ref), (ub_ref, dtb_ref, yb_ref))):
            y, st = _ssd_chain(u_ref[b], dt_ref[b], bias_ref[d], alog_ref[d], tri_ref[d], exp_ref[...],
                               h_scr[b, d], fwd=(d == 0))
            y_ref[b] = y
            h_scr[b, d] = st


def _ssd(u, dtr, dt_bias, a_log, N):
    B, T, _ = u.shape
    nc, nlat = T // CHUNK, N // CHUNK
    ncx = nc - nlat
    chunk_f = lambda c: jnp.where(c < ncx, nlat + c, c - ncx)
    chunk_b = lambda c: jnp.where(c < ncx, nlat + ncx - 1 - c, nlat - 1 - (c - ncx))
    pad = lambda t: jnp.zeros((2, 1, LANES), F32).at[:, 0, :SSM_HEADS].set(t.astype(F32))
    li = np.arange(CHUNK)
    tri = np.stack([li[:, None] >= li[None, :], li[:, None] <= li[None, :]]).astype(np.float32)
    expm = np.zeros((LANES, SSM_INNER), np.float32)
    for h in range(SSM_HEADS):
        expm[h, h * 64:(h + 1) * 64] = 1.0
    full = lambda shape: pl.BlockSpec(shape, lambda c: (0,) * len(shape))
    y_shape = jax.ShapeDtypeStruct((B, T, SSM_INNER), F32)
    return pl.pallas_call(
        _ssd_kernel,
        grid=(nc,),
        in_specs=[pl.BlockSpec((B, CHUNK, SSM_XBC), lambda c: (0, chunk_f(c), 0)),
                  pl.BlockSpec((B, CHUNK, SSM_XBC), lambda c: (0, chunk_b(c), 0)),
                  pl.BlockSpec((B, CHUNK, LANES), lambda c: (0, chunk_f(c), 0)),
                  pl.BlockSpec((B, CHUNK, LANES), lambda c: (0, chunk_b(c), 1)),
                  full((2, 1, LANES)), full((2, 1, LANES)), full((2, CHUNK, CHUNK)), full((LANES, SSM_INNER))],
        out_specs=[pl.BlockSpec((B, CHUNK, SSM_INNER), lambda c: (0, chunk_f(c), 0)),
                   pl.BlockSpec((B, CHUNK, SSM_INNER), lambda c: (0, chunk_b(c), 0))],
        out_shape=[y_shape, y_shape],
        scratch_shapes=[pltpu.VMEM((B, 2, SSM_STATE, SSM_INNER), F32)],
        compiler_params=_cparams(("arbitrary",)),
        name="ssd_scan",
    )(u, u, dtr, dtr, pad(dt_bias), pad(a_log), jnp.asarray(tri, BF16), jnp.asarray(expm, BF16))


def _ssm_finish_kernel(yf_ref, yb_ref, u_ref, z_ref, dsk_ref, g_ref, o_ref):
    y = yf_ref[0] + yb_ref[0] + dsk_ref[...] * u_ref[0].astype(F32)
    z = z_ref[0].astype(F32)
    o_ref[0] = _rms(y * (z * jax.nn.sigmoid(z)), g_ref[...]).astype(BF16)


def _ssm_finish(y_f, y_b, u, p_plain, d_skip, g_ssm, *, tm=256):
    B, T, W = y_f.shape
    dsk = jnp.repeat(d_skip.astype(F32), W // SSM_HEADS).reshape(1, W)
    row_spec = pl.BlockSpec((1, tm, W), lambda b, i: (b, i, 0))
    return pl.pallas_call(
        _ssm_finish_kernel,
        grid=(B, T // tm),
        in_specs=[row_spec, row_spec, row_spec,
                  pl.BlockSpec((1, tm, W), lambda b, i: (b, i, P_BZ)),
                  pl.BlockSpec((1, W), lambda b, i: (0, 0)),
                  pl.BlockSpec((1, W), lambda b, i: (0, 0))],
        out_specs=row_spec,
        out_shape=jax.ShapeDtypeStruct((B, T, W), BF16),
        compiler_params=_cparams(("arbitrary", "arbitrary")),
        name="ssm_finish",
    )(y_f, y_b, u, p_plain, dsk, g_ssm.reshape(1, W).astype(F32))


def _merge_kernel(y0, y1, y2, y3, s0, s1, s2, s3, wb_ref, wo_ref, x_ref, gate_ref, g_ref, o_ref):
    m = None
    for k, (y, s) in enumerate(((y0, s0), (y1, s1), (y2, s2), (y3, s3))):
        t = _dot(y[0], wb_ref[k]) * s[0].astype(F32)
        m = t if m is None else m + t
    o = _dot(m.astype(BF16), wo_ref[...])
    o_ref[0] = x_ref[0] + gate_ref[0] * _rms(o, g_ref[...])


MERGE_TM = 256


def _merge(ya, yb_all, ys, yn, p_gate, w_branch, w_out, x, gate, g_post, *, row_off):
    B, Tx, D = x.shape
    tm = MERGE_TM
    assert Tx % tm == 0 and row_off % tm == 0
    off = row_off // tm
    y_spec = pl.BlockSpec((1, tm, BRANCH_W), lambda b, i: (b, i, 0))
    yb_spec = pl.BlockSpec((1, tm, BRANCH_W), lambda b, i: (b, off + i, 0))
    s_spec = lambda k: pl.BlockSpec((1, tm, D), lambda b, i: (b, off + i, k))
    resident = lambda shape: pl.BlockSpec(shape, lambda b, i: (0,) * len(shape), pipeline_mode=pl.Buffered(1))
    return pl.pallas_call(
        _merge_kernel,
        grid=(B, Tx // tm),
        in_specs=[y_spec, yb_spec, y_spec, y_spec] + [s_spec(k) for k in range(N_BRANCH)]
                 + [resident((N_BRANCH, BRANCH_W, D)), resident((D, D)),
                    pl.BlockSpec((1, tm, D), lambda b, i: (b, i, 0)),
                    pl.BlockSpec((1, 1, D), lambda b, i: (b, 0, 0)),
                    pl.BlockSpec((1, D), lambda b, i: (0, 0))],
        out_specs=pl.BlockSpec((1, tm, D), lambda b, i: (b, i, 0)),
        out_shape=jax.ShapeDtypeStruct((B, Tx, D), F32),
        compiler_params=_cparams(("arbitrary", "arbitrary")),
        name="merge_out_proj",
    )(ya, yb_all, ys, yn, p_gate, p_gate, p_gate, p_gate, w_branch, w_out, x, gate, g_post)


def _ffn_kernel(x_ref, g1_ref, sh_ref, sc_ref, gate_ref, g2_ref, w1_ref, w2_ref, o_ref, h_scr, acc):
    f = pl.program_id(2)

    @pl.when(f == 0)
    def _():
        h = _rms(x_ref[0], g1_ref[...]) * (1.0 + sc_ref[0]) + sh_ref[0]
        h_scr[...] = h.astype(BF16)

    a = jnp.square(jnp.maximum(_dot(h_scr[...], w1_ref[...]), 0.0)).astype(BF16)
    contrib = _dot(a, w2_ref[...])

    @pl.when(f == 0)
    def _():
        acc[...] = contrib

    @pl.when(f > 0)
    def _():
        acc[...] += contrib

    @pl.when(f == pl.num_programs(2) - 1)
    def _():
        o_ref[0] = x_ref[0] + gate_ref[0] * _rms(acc[...], g2_ref[...])


def _ffn(x, g_pre, shift, scale, gate, g_post, w1, w2, *, tm, tf=1024):
    B, T, D = x.shape
    F = w1.shape[1]
    tm = min(tm, T)
    mod_spec = pl.BlockSpec((1, 1, D), lambda b, i, f: (b, 0, 0))
    vec_spec = pl.BlockSpec((1, D), lambda b, i, f: (0, 0))
    return pl.pallas_call(
        _ffn_kernel,
        grid=(B, T // tm, F // tf),
        in_specs=[pl.BlockSpec((1, tm, D), lambda b, i, f: (b, i, 0)),
                  vec_spec, mod_spec, mod_spec, mod_spec, vec_spec,
                  pl.BlockSpec((D, tf), lambda b, i, f: (0, f)),
                  pl.BlockSpec((tf, D), lambda b, i, f: (f, 0))],
        out_specs=pl.BlockSpec((1, tm, D), lambda b, i, f: (b, i, 0)),
        out_shape=jax.ShapeDtypeStruct((B, T, D), F32),
        scratch_shapes=[pltpu.VMEM((tm, D), BF16), pltpu.VMEM((tm, D), F32)],
        compiler_params=_cparams(("arbitrary", "arbitrary", "arbitrary")),
        name="mlp",
    )(x, g_pre, shift, scale, gate, g_post, w1, w2)


def _layout_w_in(w):
    D = w.shape[0]
    o = 0
    parts = {}
    for name, width in (("aq", 512), ("ak", 512), ("av", 512), ("bz", 512), ("bx", SSM_XBC), ("bdt", 16),
                        ("sq", 512), ("sk", 128), ("sv", 128), ("nq", 512), ("nk", 512), ("nv", 512),
                        ("gate", N_BRANCH * D)):
        parts[name] = w[:, o:o + width]
        o += width
    assert o == w.shape[1]
    sq = parts["sq"].reshape(D, SW_KV, 4, HEAD_DIM).transpose(0, 2, 1, 3).reshape(D, 512)
    zpad = jnp.zeros((D, LANES - SSM_HEADS), w.dtype)
    misc = jnp.concatenate([parts["sk"], parts["sv"], parts["bdt"][:, :SSM_HEADS], zpad,
                            parts["bdt"][:, SSM_HEADS:], zpad], axis=1)
    w_rope = jnp.concatenate([parts["aq"] * (QSCALE * LOG2E), parts["ak"], sq * QSCALE, misc], axis=1)
    w_plain = jnp.concatenate([parts["bx"], parts["nq"] * QSCALE, parts["nk"], parts["nv"], parts["av"],
                               parts["bz"]], axis=1)
    return w_rope.astype(BF16), w_plain.astype(BF16), parts["gate"].astype(BF16)


def _rope_tables(n, n_ctx):
    t = np.arange(n)
    nf = HEAD_DIM // 4
    inv = ROPE_BASE ** (-np.arange(nf, dtype=np.float32) / nf)
    row = (t // GRID_W).astype(np.float32)
    col = (t % GRID_W).astype(np.float32)
    ang = jnp.asarray(np.concatenate([row[:, None] * inv, col[:, None] * inv], -1).astype(np.float32))
    cos, sin = jnp.cos(ang), jnp.sin(ang)
    cos = jnp.concatenate([cos, jnp.ones((n_ctx, cos.shape[1]), F32)], axis=0)
    sin = jnp.concatenate([sin, jnp.zeros((n_ctx, sin.shape[1]), F32)], axis=0)
    z = jnp.zeros_like(sin)
    rep = LANES // HEAD_DIM
    cos_t = jnp.tile(jnp.concatenate([cos, cos], -1), (1, rep))
    s1 = jnp.tile(jnp.concatenate([-sin, z], -1), (1, rep))
    s2 = jnp.tile(jnp.concatenate([z, sin], -1), (1, rep))
    return cos_t, s1, s2


def kernel(x, c, ctx, c_ctx, w_mod, b_mod, g_pre_mix, g_post_mix, g_pre_mlp, g_post_mlp, w_in, lam_q1, lam_k1, lam_q2, lam_k2, g_subln, conv_w, conv_b, dt_bias, a_log, d_skip, g_ssm, sink, rpb, w_branch, w_out, w_ff1, w_ff2):
    B, N, D = x.shape
    L = ctx.shape[1]
    T = N + L
    depth = w_mod.shape[0]
    assert B + 1 <= 8 and N % GRID_W == 0 and N % L == 0
    R = N // GRID_W

    cvec = jnp.zeros((8, D), F32).at[:B].set(c).at[B].set(c_ctx)
    mods = _modulation(cvec, w_mod, b_mod)
    rope_tabs = _rope_tables(N, L)
    col_bias = _na_col_bias(rpb)
    no_sink = jnp.full((NA_HEADS,), NEG, F32)
    row = lambda v: v.reshape(1, -1).astype(F32)
    tk_lat = _largest_divisor(T, (768, 512, 256, 128))

    cx = ctx
    for l in range(depth):
        need_ctx = l < depth - 1
        lam_init = 0.8 - 0.6 * math.exp(-0.3 * l)
        mod = mods[l, :B].reshape(B, 6, 1, D)
        modc = jnp.broadcast_to(mods[l, B].reshape(1, 6, 1, D), (B, 6, 1, D))
        both = lambda k: jnp.stack([mod[:, k], modc[:, k]], axis=1).reshape(B * 2, 1, D)
        w_rope, w_plain, w_gate = _layout_w_in(w_in[l])

        h = _prenorm(x, cx, row(g_pre_mix[l]), both(0), both(1))
        p_rope, dtr = _proj_rope(h, rope_tabs, w_rope)
        p_plain = _proj(h, w_plain, gate=False, tn=TN)
        p_gate = _proj(h, w_gate, gate=True, tn=1024)
        col = lambda p, t: p[:, :, t * TN:(t + 1) * TN]

        lamv = jnp.stack([lam_q1[l], lam_k1[l], lam_q2[l], lam_k2[l]]).astype(F32)
        g_col = g_subln[l].reshape(LANES, 1).astype(F32)
        aq, av = col(p_rope, R_AQ), col(p_plain, P_AV)
        ya = _diff_attn(aq[:, :N].transpose(0, 2, 1), p_rope, _transpose_chunks(av, tk_lat), lamv, g_col,
                        lam_init, row_blk0=0, tq=512)

        u = _conv_silu(p_plain, conv_w[l], conv_b[l], N)
        y_f, y_b = _ssd(u, dtr, dt_bias[l], a_log[l], N)
        yb_all = _ssm_finish(y_f, y_b, u, p_plain, d_skip[l], g_ssm[l])

        ys = _swa(p_rope, sink[l].astype(F32), N, L)
        yn = _na(p_plain, _na_bias(col_bias[l], R, L), N, L)

        wb = w_branch[l]
        wb = wb.at[2].set(wb[2].reshape(SW_KV, 4, HEAD_DIM, D).transpose(1, 0, 2, 3).reshape(BRANCH_W, D))
        wb = wb.astype(BF16)
        wo = w_out[l].astype(BF16)
        w1 = w_ff1[l].astype(BF16)
        w2 = w_ff2[l].astype(BF16)
        x_new = _merge(ya, yb_all, ys, yn, p_gate, wb, wo, x, mod[:, 2], row(g_post_mix[l]), row_off=0)
        x_new = _ffn(x_new, row(g_pre_mlp[l]), mod[:, 3], mod[:, 4], mod[:, 5], row(g_post_mlp[l]), w1, w2, tm=512)
        if need_ctx:
            ya_c = _diff_attn(aq[:, N:].transpose(0, 2, 1), p_rope, _transpose_chunks(av[:, N:], L), lamv, g_col,
                              lam_init, row_blk0=N // L, tq=L)
            ys_c = _ctx_attend(p_rope, sink[l].astype(F32), R_SQ, R_MISC * 4, R_MISC * 4 + 1, N, L, gqa=True)
            yn_c = _ctx_attend(p_plain, no_sink, P_NQ, P_NK, P_NV, N, L, gqa=False)
            cx = _merge(ya_c, yb_all, ys_c, yn_c, p_gate, wb, wo, cx, modc[:, 2], row(g_post_mix[l]), row_off=N)
            cx = _ffn(cx, row(g_pre_mlp[l]), modc[:, 3], modc[:, 4], modc[:, 5], row(g_post_mlp[l]), w1, w2, tm=L)
        x = x_new
    return x
```

```python
import functools
import math

import numpy as np
import jax
import jax.numpy as jnp
from jax import lax
from jax.experimental import pallas as pl
from jax.experimental.pallas import tpu as pltpu

F32 = jnp.float32
BF16 = jnp.bfloat16

LANES = 128
VMEM_LIMIT = 56 * 1024 * 1024
NEG = -1e30
EPS = 1e-6
GRID_W = 64
ROPE_BASE = 10000.0
HEAD_DIM = 64
QSCALE = HEAD_DIM ** -0.5
LOG2E = math.log2(math.e)
DA_HEADS = 4
SSM_HEADS = 8
SSM_INNER = 512
SSM_GROUPS = 2
SSM_STATE = 128
SSM_XBC = SSM_INNER + 2 * SSM_GROUPS * SSM_STATE
SSM_CONV = 5
CHUNK = 128
SW_KV = 2
SW_BLOCK = 128
NA_HEADS = 8
NA_ROWS = 8
NA_COLS = 16
N_BRANCH = 4
BRANCH_W = 512

TN = 512
R_AQ, R_AK, R_SQ, R_MISC = 0, 1, 2, 3
P_NQ, P_NK, P_NV, P_AV, P_BZ = 2, 3, 4, 5, 6


def _cparams(sem):
    return pltpu.CompilerParams(dimension_semantics=sem, vmem_limit_bytes=VMEM_LIMIT)


def _rms(t, g):
    return t * lax.rsqrt(jnp.mean(t * t, -1, keepdims=True) + EPS) * g


def _dot(a, b):
    return jnp.dot(a, b, preferred_element_type=F32)


def _dot_nt(a, b):
    return lax.dot_general(a, b, (((1,), (1,)), ((), ())), preferred_element_type=F32)


def _dot_tn(a, b):
    return lax.dot_general(a, b, (((0,), (0,)), ((), ())), preferred_element_type=F32)


def _half_mask(hh):
    lane = lax.broadcasted_iota(jnp.int32, (1, LANES), 1)
    return (lane // HEAD_DIM) == hh


def _largest_divisor(n, cands):
    for c in cands:
        if n % c == 0:
            return c
    raise ValueError(f"no tile in {cands} divides {n}")


MOD_KB = 256


def _mod_kernel(c_ref, wa_ref, wb_ref, b_ref, o_ref):
    k = pl.program_id(1)
    c = c_ref[...]
    s = (c * jax.nn.sigmoid(c)).astype(BF16)
    half = MOD_KB // 2
    part = _dot(s[:, :half], wa_ref[0].astype(BF16)) + _dot(s[:, half:], wb_ref[0].astype(BF16))

    @pl.when(k == 0)
    def _():
        o_ref[0] = part + b_ref[0]

    @pl.when(k > 0)
    def _():
        o_ref[0] += part


def _modulation(cvec, w_mod, b_mod):
    depth, D, C = w_mod.shape
    return pl.pallas_call(
        _mod_kernel,
        grid=(depth, D // MOD_KB),
        in_specs=[pl.BlockSpec((8, MOD_KB), lambda l, k: (0, k)),
                  pl.BlockSpec((1, MOD_KB // 2, C), lambda l, k: (l, 2 * k, 0)),
                  pl.BlockSpec((1, MOD_KB // 2, C), lambda l, k: (l, 2 * k + 1, 0)),
                  pl.BlockSpec((1, 1, C), lambda l, k: (l, 0, 0))],
        out_specs=pl.BlockSpec((1, 8, C), lambda l, k: (l, 0, 0)),
        out_shape=jax.ShapeDtypeStruct((depth, 8, C), F32),
        compiler_params=_cparams(("arbitrary", "arbitrary")),
        name="modulation",
    )(cvec, w_mod, w_mod, b_mod.reshape(depth, 1, C))


NORM_BLK = 256


def _prenorm_kernel(x_ref, cx_ref, g_ref, sh_ref, sc_ref, o_ref, *, nlat):
    i = pl.program_id(1)

    def emit(t):
        o_ref[0] = (_rms(t, g_ref[...]) * (1.0 + sc_ref[0]) + sh_ref[0]).astype(BF16)

    @pl.when(i < nlat)
    def _():
        emit(x_ref[0])

    @pl.when(i >= nlat)
    def _():
        emit(cx_ref[0])


def _prenorm(x, cx, g, shift2, scale2):
    B, N, D = x.shape
    L = cx.shape[1]
    assert N % NORM_BLK == 0 and L % NORM_BLK == 0
    nlat, nctx = N // NORM_BLK, L // NORM_BLK
    mod_spec = pl.BlockSpec((1, 1, D), lambda b, i: (b * 2 + (i >= nlat).astype(jnp.int32), 0, 0))
    return pl.pallas_call(
        functools.partial(_prenorm_kernel, nlat=nlat),
        grid=(B, nlat + nctx),
        in_specs=[pl.BlockSpec((1, NORM_BLK, D), lambda b, i: (b, jnp.minimum(i, nlat - 1), 0)),
                  pl.BlockSpec((1, NORM_BLK, D), lambda b, i: (b, jnp.maximum(i - nlat, 0), 0)),
                  pl.BlockSpec((1, D), lambda b, i: (0, 0)),
                  mod_spec, mod_spec],
        out_specs=pl.BlockSpec((1, NORM_BLK, D), lambda b, i: (b, i, 0)),
        out_shape=jax.ShapeDtypeStruct((B, N + L, D), BF16),
        compiler_params=_cparams(("arbitrary", "arbitrary")),
        name="prenorm",
    )(x, cx, g, shift2, scale2)


def _rope(t, cos, s1, s2):
    w = t.shape[1]
    reps = w // LANES
    tile = lambda a: a if reps == 1 else jnp.concatenate([a] * reps, axis=1)
    return (t * tile(cos) + pltpu.roll(t, w - HEAD_DIM // 2, 1) * tile(s1)
            + pltpu.roll(t, HEAD_DIM // 2, 1) * tile(s2))


def _proj_rope_kernel(h_ref, cos_ref, s1_ref, s2_ref, w_ref, o_ref, odt_ref):
    j = pl.program_id(2)
    acc = _dot(h_ref[0], w_ref[...])
    rot = lambda t: _rope(t, cos_ref[...], s1_ref[...], s2_ref[...])

    @pl.when(j != R_MISC)
    def _():
        o_ref[0] = rot(acc).astype(BF16)

    @pl.when(j == R_MISC)
    def _():
        o_ref[0] = jnp.concatenate([rot(acc[:, :LANES]), acc[:, LANES:]], axis=1).astype(BF16)
        odt_ref[0] = acc[:, 2 * LANES:]


def _row_tile(T):
    return _largest_divisor(T, (1056, 1024, 768, 640, 512, 256, 128))


def _proj_rope(h, rope_tabs, w):
    B, T, D = h.shape
    C = w.shape[1]
    tm = _row_tile(T)
    cos, s1, s2 = rope_tabs
    tab_spec = pl.BlockSpec((tm, LANES), lambda b, i, j: (i, 0))
    return pl.pallas_call(
        _proj_rope_kernel,
        grid=(B, T // tm, C // TN),
        in_specs=[pl.BlockSpec((1, tm, D), lambda b, i, j: (b, i, 0)),
                  tab_spec, tab_spec, tab_spec,
                  pl.BlockSpec((D, TN), lambda b, i, j: (0, j))],
        out_specs=[pl.BlockSpec((1, tm, TN), lambda b, i, j: (b, i, j)),
                   pl.BlockSpec((1, tm, 2 * LANES), lambda b, i, j: (b, i, 0))],
        out_shape=[jax.ShapeDtypeStruct((B, T, C), BF16),
                   jax.ShapeDtypeStruct((B, T, 2 * LANES), F32)],
        compiler_params=_cparams(("arbitrary", "arbitrary", "arbitrary")),
        name="proj_rope",
    )(h, cos, s1, s2, w)


def _proj_kernel(h_ref, w_ref, o_ref, *, gate):
    acc = _dot(h_ref[0], w_ref[...])
    o_ref[0] = (jax.nn.sigmoid(acc) if gate else acc).astype(BF16)


def _proj(h, w, *, gate, tn):
    B, T, D = h.shape
    C = w.shape[1]
    tm = _row_tile(T)
    return pl.pallas_call(
        functools.partial(_proj_kernel, gate=gate),
        grid=(B, T // tm, C // tn),
        in_specs=[pl.BlockSpec((1, tm, D), lambda b, i, j: (b, i, 0)),
                  pl.BlockSpec((D, tn), lambda b, i, j: (0, j))],
        out_specs=pl.BlockSpec((1, tm, tn), lambda b, i, j: (b, i, j)),
        out_shape=jax.ShapeDtypeStruct((B, T, C), BF16),
        compiler_params=_cparams(("arbitrary", "arbitrary", "arbitrary")),
        name="proj_gate" if gate else "proj_plain",
    )(h, w)


VT_ROWS = LANES + 16


def _diff_attn_kernel(lam_ref, g_ref, q_ref, k_ref, v_ref, o_ref, s_scr, vt_scr, *, lam_init, tk):
    q = q_ref[0]
    tq = q.shape[0]
    nk = k_ref.shape[1] // tk
    q_m = [jnp.where(_half_mask(m), q, jnp.zeros_like(q)) for m in range(2)]

    @pl.when(pl.program_id(2) == 0)
    def _():
        ones_row = (lax.broadcasted_iota(jnp.int32, (VT_ROWS - LANES, tk), 0) == 0).astype(BF16)
        for c in range(nk):
            v_t = v_ref[0, c * tk:(c + 1) * tk, :].astype(F32).T.astype(BF16)
            vt_scr[c] = jnp.concatenate([v_t, ones_row], axis=0)

    def scores(c):
        kc = k_ref[0, c * tk:(c + 1) * tk, :]
        for m in range(2):
            s_scr[c % 2, m] = _dot_nt(kc, q_m[m])

    carry = [(jnp.full((1, tq), -jnp.inf, F32), jnp.zeros((VT_ROWS, tq), F32)) for _ in range(2)]
    scores(0)
    for c in range(nk):
        if c + 1 < nk:
            scores(c + 1)
        vc = vt_scr[c]
        for m in range(2):
            mx, acc = carry[m]
            m_new = jnp.maximum(mx, jnp.max(s_scr[c % 2, m], 0, keepdims=True))
            p = jnp.exp2((s_scr[c % 2, m] - m_new).astype(BF16))
            acc = jnp.exp2(mx - m_new) * acc + _dot(vc, p)
            carry[m] = (m_new, acc)
    (_, a0), (_, a1) = carry
    lv = lam_ref[...]
    lam = (jnp.exp(jnp.sum(lv[0:1] * lv[1:2], -1, keepdims=True))
           - jnp.exp(jnp.sum(lv[2:3] * lv[3:4], -1, keepdims=True)) + lam_init)
    o = a0[:LANES] / a0[LANES:LANES + 1] - lam * (a1[:LANES] / a1[LANES:LANES + 1])
    o = o * lax.rsqrt(jnp.mean(o * o, 0, keepdims=True) + EPS) * g_ref[...] * (1.0 - lam_init)
    o_ref[0] = o.T.astype(BF16)


def _diff_attn(p_rope, p_plain, lamv, g_col, lam_init, *, q_row0, n_q, k_row0, n_k, tq, tk):
    B = p_rope.shape[0]
    tq = min(tq, n_q)
    assert n_q % tq == 0 and q_row0 % tq == 0 and k_row0 % n_k == 0 and n_k % tk == 0
    qb, kb = q_row0 // tq, k_row0 // n_k
    return pl.pallas_call(
        functools.partial(_diff_attn_kernel, lam_init=lam_init, tk=tk),
        grid=(B, DA_HEADS, n_q // tq),
        in_specs=[pl.BlockSpec((4, HEAD_DIM), lambda b, h, i: (0, 0)),
                  pl.BlockSpec((LANES, 1), lambda b, h, i: (0, 0)),
                  pl.BlockSpec((1, tq, LANES), lambda b, h, i: (b, qb + i, R_AQ * 4 + h)),
                  pl.BlockSpec((1, n_k, LANES), lambda b, h, i: (b, kb, R_AK * 4 + h)),
                  pl.BlockSpec((1, n_k, LANES), lambda b, h, i: (b, kb, P_AV * 4 + h))],
        out_specs=pl.BlockSpec((1, tq, LANES), lambda b, h, i: (b, i, h)),
        out_shape=jax.ShapeDtypeStruct((B, n_q, DA_HEADS * LANES), BF16),
        scratch_shapes=[pltpu.VMEM((2, 2, tk, tq), F32),
                        pltpu.VMEM((n_k // tk, VT_ROWS, tk), BF16)],
        compiler_params=_cparams(("arbitrary",) * 3),
        name="diff_attn",
    )(lamv, g_col, p_rope, p_rope, p_plain)


def _swa_kernel(sink_ref, q_ref, kp_ref, kc_ref, kn_ref, vp_ref, vc_ref, vn_ref, kx_ref, vx_ref, o_ref):
    i = pl.program_id(1)
    nb = pl.num_programs(1)
    blk = SW_BLOCK
    kband = jnp.concatenate([kp_ref[0], kc_ref[0], kn_ref[0], kx_ref[0]], axis=0)
    vband = jnp.concatenate([vp_ref[0], vc_ref[0], vn_ref[0], vx_ref[0]], axis=0)
    nkeys = kband.shape[0]
    row = lax.broadcasted_iota(jnp.int32, (blk, nkeys), 0)
    col = lax.broadcasted_iota(jnp.int32, (blk, nkeys), 1)
    big = 4 * blk
    off_prev = jnp.where(i > 0, 0, big)
    off_next = jnp.where(i < nb - 1, 0, big)
    ok_prev = jnp.where(col >= row + off_prev, 0.0, NEG)
    ok_next = jnp.where(col - 2 * blk <= row - off_next, 0.0, NEG)
    bias = jnp.where(col < blk, ok_prev, jnp.where((col >= 2 * blk) & (col < 3 * blk), ok_next, 0.0))
    bias = jnp.concatenate([bias] * 4, axis=0)
    q = q_ref[0]
    outs = [None] * 4

    def scores(kv):
        qs = jnp.concatenate([jnp.where(_half_mask(kv), q[:, g * LANES:(g + 1) * LANES],
                                        jnp.zeros((blk, LANES), BF16)) for g in range(4)], axis=0)
        return _dot_nt(qs, kband) + bias

    s_all = [scores(kv) for kv in range(SW_KV)]
    for kv in range(SW_KV):
        hm = _half_mask(kv)
        s = s_all[kv]
        sink = jnp.concatenate([jnp.full((blk, 1), sink_ref[kv * 4 + g], F32) for g in range(4)], axis=0)
        m = jnp.maximum(jnp.max(s, -1, keepdims=True), sink)
        p = jnp.exp(s - m)
        denom = jnp.sum(p, -1, keepdims=True) + jnp.exp(sink - m)
        o = _dot(p.astype(BF16), jnp.where(hm, vband, jnp.zeros_like(vband))) / denom
        for g in range(4):
            part = o[g * blk:(g + 1) * blk]
            outs[g] = part if outs[g] is None else outs[g] + part
    o_ref[0] = jnp.concatenate(outs, axis=1).astype(BF16)


def _swa(p_rope, sink, N, L):
    B = p_rope.shape[0]
    nb = N // SW_BLOCK
    kcol, vcol = R_MISC * 4, R_MISC * 4 + 1
    kv_spec = lambda colblk, off: pl.BlockSpec(
        (1, SW_BLOCK, LANES), lambda b, i: (b, jnp.clip(i + off, 0, nb - 1), colblk))
    return pl.pallas_call(
        _swa_kernel,
        grid=(B, nb),
        in_specs=[pl.BlockSpec(memory_space=pltpu.SMEM),
                  pl.BlockSpec((1, SW_BLOCK, TN), lambda b, i: (b, i, R_SQ)),
                  kv_spec(kcol, -1), kv_spec(kcol, 0), kv_spec(kcol, 1),
                  kv_spec(vcol, -1), kv_spec(vcol, 0), kv_spec(vcol, 1),
                  pl.BlockSpec((1, L, LANES), lambda b, i: (b, N // L, kcol)),
                  pl.BlockSpec((1, L, LANES), lambda b, i: (b, N // L, vcol))],
        out_specs=pl.BlockSpec((1, SW_BLOCK, TN), lambda b, i: (b, i, 0)),
        out_shape=jax.ShapeDtypeStruct((B, N, TN), BF16),
        compiler_params=_cparams(("arbitrary", "arbitrary")),
        name="window_attn",
    )(sink, *([p_rope] * 9))


NA_QROWS = 2
NA_KBLKS = 5


def _na_kernel(bias_ref, q_ref, k0, k1, k2, k3, k4, v0, v1, v2, v3, v4, kx_ref, vx_ref, o_ref):
    kall = jnp.concatenate([k0[0], k1[0], k2[0], k3[0], k4[0], kx_ref[0]], axis=0)
    vall = jnp.concatenate([v0[0], v1[0], v2[0], v3[0], v4[0], vx_ref[0]], axis=0)
    q = q_ref[0]
    blocks = [slice(pb * LANES, (pb + 1) * LANES) for pb in range(4)]
    scores = [[_dot_nt(jnp.where(_half_mask(hh), q[:, sl], jnp.zeros((q.shape[0], LANES), BF16)), kall[:, sl])
               + bias_ref[0, pb * 2 + hh] for hh in range(2)] for pb, sl in enumerate(blocks)]
    outs = []
    for pb, sl in enumerate(blocks):
        vb = vall[:, sl]
        ob = None
        for hh in range(2):
            hm = _half_mask(hh)
            s = scores[pb][hh]
            m = jnp.max(s, -1, keepdims=True)
            p = jnp.exp(s - m)
            o = _dot(p.astype(BF16), jnp.where(hm, vb, jnp.zeros_like(vb))) / jnp.sum(p, -1, keepdims=True)
            ob = o if ob is None else ob + o
        outs.append(ob)
    o_ref[0] = jnp.concatenate(outs, axis=1).astype(BF16)


def _col_bias_kernel(r_ref, sel_ref, neg_ref, o_ref):
    o_ref[...] = sum(_dot(part, sel_ref[...]) for part in reversed(_split3(r_ref[...]))) + neg_ref[...]


def _na_col_bias(rpb_all):
    W = GRID_W
    nco = 2 * NA_COLS - 1
    lead = rpb_all.shape[:3]
    rows = int(np.prod(lead))
    c = np.arange(W)
    cb = np.clip(c - NA_COLS // 2, 0, W - NA_COLS)
    dc = c[None, :] - cb[:, None]
    inside = (dc >= 0) & (dc < NA_COLS)
    co = c[None, :] - c[:, None] + NA_COLS - 1
    sel = ((co[None] == np.arange(32)[:, None, None]) & inside[None]).reshape(32, W * W)
    neg = np.where(inside, 0.0, NEG).reshape(1, W * W).astype(np.float32)
    table = jnp.pad(rpb_all.astype(F32).reshape(rows, nco), ((0, (-rows) % 8), (0, 32 - nco)))
    out = pl.pallas_call(
        _col_bias_kernel,
        out_shape=jax.ShapeDtypeStruct((table.shape[0], W * W), F32),
        name="na_col_bias",
    )(table, jnp.asarray(sel, BF16), jnp.asarray(neg))
    return out[:rows].reshape(lead + (W, W))


def _na_bias(t2, R, L):
    W = GRID_W
    depth, H = t2.shape[:2]
    clamp = lambda v, lo, hi: min(max(v, lo), hi)
    nblk = R // NA_QROWS
    slots = NA_KBLKS * NA_QROWS
    per = []
    for i in (0, 1, 2, nblk - 2, nblk - 1):
        base_blk = clamp(i - 2, 0, nblk - NA_KBLKS)
        for t in range(NA_QROWS):
            rq = NA_QROWS * i + t
            base = clamp(rq - NA_ROWS // 2, 0, R - NA_ROWS)
            s_lo = base - NA_QROWS * base_blk
            assert 0 <= s_lo and s_lo + NA_ROWS <= slots
            ro = base - rq + NA_ROWS - 1
            per.append(jnp.pad(t2[:, :, ro:ro + NA_ROWS],
                               ((0, 0), (0, 0), (s_lo, slots - NA_ROWS - s_lo), (0, 0), (0, 0)),
                               constant_values=NEG))
    b = jnp.stack(per, axis=2).reshape(depth, H, 5, NA_QROWS, slots, W, W)
    b = b.transpose(0, 2, 1, 3, 5, 4, 6).reshape(depth, 5, H, NA_QROWS * W, slots * W)
    return jnp.concatenate([b, jnp.zeros(b.shape[:4] + (L,), F32)], axis=-1)


def _na(p_plain, bias, N, L):
    B = p_plain.shape[0]
    rows = NA_QROWS * GRID_W
    nblk = N // rows
    assert nblk >= NA_KBLKS + 2
    cls = lambda i: jnp.where(i < 2, i, jnp.where(i >= nblk - 2, i - (nblk - 5), 2))
    kv_spec = lambda tile, s: pl.BlockSpec(
        (1, rows, TN), lambda b, i: (b, jnp.clip(i - 2, 0, nblk - NA_KBLKS) + s, tile))
    nkeys = NA_KBLKS * rows + L
    return pl.pallas_call(
        _na_kernel,
        grid=(B, nblk),
        in_specs=[pl.BlockSpec((1, NA_HEADS, rows, nkeys), lambda b, i: (cls(i), 0, 0, 0)),
                  pl.BlockSpec((1, rows, TN), lambda b, i: (b, i, P_NQ))]
                 + [kv_spec(P_NK, s) for s in range(NA_KBLKS)]
                 + [kv_spec(P_NV, s) for s in range(NA_KBLKS)]
                 + [pl.BlockSpec((1, L, TN), lambda b, i: (b, N // L, P_NK)),
                    pl.BlockSpec((1, L, TN), lambda b, i: (b, N // L, P_NV))],
        out_specs=pl.BlockSpec((1, rows, TN), lambda b, i: (b, i, 0)),
        out_shape=jax.ShapeDtypeStruct((B, N, TN), BF16),
        compiler_params=_cparams(("arbitrary", "arbitrary")),
        name="neighbourhood_attn",
    )(bias, *([p_plain] * (2 * NA_KBLKS + 3)))


def _ctx_attend_kernel(sink_ref, q_ref, k_ref, v_ref, o_ref, *, gqa):
    q = q_ref[0]
    k = k_ref[0]
    v = v_ref[0]
    outs = []
    for blk in range(4):
        sl = slice(blk * LANES, (blk + 1) * LANES)
        qb = q[:, sl]
        kb, vb = (k, v) if gqa else (k[:, sl], v[:, sl])
        ob = None
        for hh in range(2):
            hm = _half_mask(hh)
            head = hh * 4 + blk if gqa else blk * 2 + hh
            s = _dot_nt(jnp.where(hm, qb, jnp.zeros_like(qb)), kb)
            sink = sink_ref[head]
            m = jnp.maximum(jnp.max(s, -1, keepdims=True), sink)
            p = jnp.exp(s - m)
            denom = jnp.sum(p, -1, keepdims=True) + jnp.exp(sink - m)
            o = _dot(p.astype(BF16), jnp.where(hm, vb, jnp.zeros_like(vb))) / denom
            ob = o if ob is None else ob + o
        outs.append(ob)
    o_ref[0] = jnp.concatenate(outs, axis=1).astype(BF16)


def _ctx_attend(p, sink, q_tile, k_blk, v_blk, N, L, *, gqa):
    B = p.shape[0]
    kw = LANES if gqa else TN
    rb = N // L
    return pl.pallas_call(
        functools.partial(_ctx_attend_kernel, gqa=gqa),
        grid=(B,),
        in_specs=[pl.BlockSpec(memory_space=pltpu.SMEM),
                  pl.BlockSpec((1, L, TN), lambda b: (b, rb, q_tile)),
                  pl.BlockSpec((1, L, kw), lambda b: (b, rb, k_blk)),
                  pl.BlockSpec((1, L, kw), lambda b: (b, rb, v_blk))],
        out_specs=pl.BlockSpec((1, L, TN), lambda b: (b, 0, 0)),
        out_shape=jax.ShapeDtypeStruct((B, L, TN), BF16),
        compiler_params=_cparams(("arbitrary",)),
        name="ctx_attend_gqa" if gqa else "ctx_attend",
    )(sink, p, p, p)


CONV_BLK = 256
HALO = 16


def _conv_kernel(prev_ref, cur_ref, next_ref, w_ref, b_ref, o_ref, *, lat_blocks):
    i = pl.program_id(1)
    last = pl.num_programs(1) - 1
    has_prev = (i != 0) & (i != lat_blocks)
    has_next = (i != lat_blocks - 1) & (i != last)
    prev = prev_ref[0].astype(F32) * has_prev.astype(F32)
    nxt = next_ref[0].astype(F32) * has_next.astype(F32)
    ext = jnp.concatenate([prev, cur_ref[0].astype(F32), nxt], axis=0)
    n = ext.shape[0]
    w = w_ref[...]
    acc = None
    for k in range(SSM_CONV):
        shift = (SSM_CONV // 2 - k) % n
        t = (ext if shift == 0 else pltpu.roll(ext, shift, 0))[HALO:HALO + CONV_BLK] * w[k:k + 1]
        acc = t if acc is None else acc + t
    acc = acc + b_ref[...]
    o_ref[0] = (acc * jax.nn.sigmoid(acc)).astype(BF16)


def _conv_silu(p_plain, conv_w, conv_b, N):
    B, T, _ = p_plain.shape
    C = SSM_XBC
    assert N % CONV_BLK == 0 and T % CONV_BLK == 0
    per = CONV_BLK // HALO
    nh = T // HALO
    w = jnp.zeros((8, C), F32).at[:SSM_CONV].set(conv_w.astype(F32))
    return pl.pallas_call(
        functools.partial(_conv_kernel, lat_blocks=N // CONV_BLK),
        grid=(B, T // CONV_BLK),
        in_specs=[pl.BlockSpec((1, HALO, C), lambda b, i: (b, jnp.maximum(i * per - 1, 0), 0)),
                  pl.BlockSpec((1, CONV_BLK, C), lambda b, i: (b, i, 0)),
                  pl.BlockSpec((1, HALO, C), lambda b, i: (b, jnp.minimum((i + 1) * per, nh - 1), 0)),
                  pl.BlockSpec((8, C), lambda b, i: (0, 0)),
                  pl.BlockSpec((1, C), lambda b, i: (0, 0))],
        out_specs=pl.BlockSpec((1, CONV_BLK, C), lambda b, i: (b, i, 0)),
        out_shape=jax.ShapeDtypeStruct((B, T, C), BF16),
        compiler_params=_cparams(("arbitrary", "arbitrary")),
        name="ssm_conv",
    )(p_plain, p_plain, p_plain, w, conv_b.reshape(1, C).astype(F32))


def _split3(t):
    hi = t.astype(BF16)
    r = t - hi.astype(F32)
    mid = r.astype(BF16)
    return hi, mid, (r - mid.astype(F32)).astype(BF16)


def _ssd_chain(u, dtr, dtb, alog, tri, expm, state, *, fwd):
    last = CHUNK - 1 if fwd else 0
    xs = u[:, :SSM_INNER].astype(F32)
    bm = u[:, SSM_INNER:SSM_INNER + SSM_GROUPS * SSM_STATE]
    cm = u[:, SSM_INNER + SSM_GROUPS * SSM_STATE:]
    dt = jax.nn.softplus(dtr + dtb)
    a = dt * (-jnp.exp(alog))
    acum = sum(_dot(tri, part) for part in reversed(_split3(a)))
    eac = jnp.exp(acum)
    dst = jnp.exp(acum[last:last + 1] - acum)
    e3 = sum(_dot(part, expm) for part in reversed(_split3(jnp.concatenate([dt, eac, dst], axis=0))))
    dt_e, eac_e, dst_e = e3[:CHUNK], e3[CHUNK:2 * CHUNK], e3[2 * CHUNK:]
    xg = xs * dt_e
    xg_b = xg.astype(BF16)
    xgd_b = (xg * dst_e).astype(BF16)
    acum_t = acum.T
    row = lax.broadcasted_iota(jnp.int32, (CHUNK, CHUNK), 0)
    col = lax.broadcasted_iota(jnp.int32, (CHUNK, CHUNK), 1)
    keep = (row >= col) if fwd else (row <= col)
    state_b = state.astype(BF16)
    gw = SSM_INNER // SSM_GROUPS
    ys, st_new = [], []
    for g in range(SSM_GROUPS):
        bmg = bm[:, g * SSM_STATE:(g + 1) * SSM_STATE]
        cmg = cm[:, g * SSM_STATE:(g + 1) * SSM_STATE]
        cb = _dot_nt(cmg, bmg)
        yoff = _dot(cmg, state_b[:, g * gw:(g + 1) * gw])
        st_new.append(_dot_tn(bmg, xgd_b[:, g * gw:(g + 1) * gw]))
        for pb in range(gw // LANES):
            blk = g * (gw // LANES) + pb
            sl = slice(blk * LANES, (blk + 1) * LANES)
            xgb = xg_b[:, sl]
            yd = None
            for hh in range(2):
                h = blk * 2 + hh
                seg = acum[:, h:h + 1] - acum_t[h:h + 1, :]
                ld = jnp.exp(jnp.where(keep, seg, -jnp.inf))
                t = _dot((cb * ld).astype(BF16), jnp.where(_half_mask(hh), xgb, jnp.zeros_like(xgb)))
                yd = t if yd is None else yd + t
            ys.append(yd + yoff[:, pb * LANES:(pb + 1) * LANES] * eac_e[:, sl])
    y = jnp.concatenate(ys, axis=1)
    return y, state * eac_e[last:last + 1] + jnp.concatenate(st_new, axis=1)


def _ssd_kernel(uf_ref, ub_ref, dtf_ref, dtb_ref, bias_ref, alog_ref, tri_ref, exp_ref, yf_ref, yb_ref, h_scr):
    @pl.when(pl.program_id(0) == 0)
    def _():
        h_scr[...] = jnp.zeros(h_scr.shape, F32)

    for b in range(uf_ref.shape[0]):
        for d, (u_ref, dt_ref, y_ref) in enumerate(((uf_ref, dtf_ref, yf_ref), (ub_ref, dtb_ref, yb_ref))):
            y, st = _ssd_chain(u_ref[b], dt_ref[b], bias_ref[d], alog_ref[d], tri_ref[d], exp_ref[...],
                               h_scr[b, d], fwd=(d == 0))
            y_ref[b] = y
            h_scr[b, d] = st


def _ssd(u, dtr, dt_bias, a_log, N):
    B, T, _ = u.shape
    nc, nlat = T // CHUNK, N // CHUNK
    ncx = nc - nlat
    chunk_f = lambda c: jnp.where(c < ncx, nlat + c, c - ncx)
    chunk_b = lambda c: jnp.where(c < ncx, nlat + ncx - 1 - c, nlat - 1 - (c - ncx))
    pad = lambda t: jnp.zeros((2, 1, LANES), F32).at[:, 0, :SSM_HEADS].set(t.astype(F32))
    li = np.arange(CHUNK)
    tri = np.stack([li[:, None] >= li[None, :], li[:, None] <= li[None, :]]).astype(np.float32)
    expm = np.zeros((LANES, SSM_INNER), np.float32)
    for h in range(SSM_HEADS):
        expm[h, h * 64:(h + 1) * 64] = 1.0
    full = lambda shape: pl.BlockSpec(shape, lambda c: (0,) * len(shape))
    y_shape = jax.ShapeDtypeStruct((B, T, SSM_INNER), F32)
    return pl.pallas_call(
        _ssd_kernel,
        grid=(nc,),
        in_specs=[pl.BlockSpec((B, CHUNK, SSM_XBC), lambda c: (0, chunk_f(c), 0)),
                  pl.BlockSpec((B, CHUNK, SSM_XBC), lambda c: (0, chunk_b(c), 0)),
                  pl.BlockSpec((B, CHUNK, LANES), lambda c: (0, chunk_f(c), 0)),
                  pl.BlockSpec((B, CHUNK, LANES), lambda c: (0, chunk_b(c), 1)),
                  full((2, 1, LANES)), full((2, 1, LANES)), full((2, CHUNK, CHUNK)), full((LANES, SSM_INNER))],
        out_specs=[pl.BlockSpec((B, CHUNK, SSM_INNER), lambda c: (0, chunk_f(c), 0)),
                   pl.BlockSpec((B, CHUNK, SSM_INNER), lambda c: (0, chunk_b(c), 0))],
        out_shape=[y_shape, y_shape],
        scratch_shapes=[pltpu.VMEM((B, 2, SSM_STATE, SSM_INNER), F32)],
        compiler_params=_cparams(("arbitrary",)),
        name="ssd_scan",
    )(u, u, dtr, dtr, pad(dt_bias), pad(a_log), jnp.asarray(tri, BF16), jnp.asarray(expm, BF16))


def _ssm_finish_kernel(yf_ref, yb_ref, u_ref, z_ref, dsk_ref, g_ref, o_ref):
    y = yf_ref[0] + yb_ref[0] + dsk_ref[...] * u_ref[0].astype(F32)
    z = z_ref[0].astype(F32)
    o_ref[0] = _rms(y * (z * jax.nn.sigmoid(z)), g_ref[...]).astype(BF16)


def _ssm_finish(y_f, y_b, u, p_plain, d_skip, g_ssm, *, tm=256):
    B, T, W = y_f.shape
    dsk = jnp.repeat(d_skip.astype(F32), W // SSM_HEADS).reshape(1, W)
    row_spec = pl.BlockSpec((1, tm, W), lambda b, i: (b, i, 0))
    return pl.pallas_call(
        _ssm_finish_kernel,
        grid=(B, T // tm),
        in_specs=[row_spec, row_spec, row_spec,
                  pl.BlockSpec((1, tm, W), lambda b, i: (b, i, P_BZ)),
                  pl.BlockSpec((1, W), lambda b, i: (0, 0)),
                  pl.BlockSpec((1, W), lambda b, i: (0, 0))],
        out_specs=row_spec,
        out_shape=jax.ShapeDtypeStruct((B, T, W), BF16),
        compiler_params=_cparams(("arbitrary", "arbitrary")),
        name="ssm_finish",
    )(y_f, y_b, u, p_plain, dsk, g_ssm.reshape(1, W).astype(F32))


def _merge_kernel(y0, y1, y2, y3, s0, s1, s2, s3, wb_ref, wo_ref, x_ref, gate_ref, g_ref, o_ref):
    m = None
    for k, (y, s) in enumerate(((y0, s0), (y1, s1), (y2, s2), (y3, s3))):
        t = _dot(y[0], wb_ref[k]) * s[0].astype(F32)
        m = t if m is None else m + t
    o = _dot(m.astype(BF16), wo_ref[...])
    o_ref[0] = x_ref[0] + gate_ref[0] * _rms(o, g_ref[...])


MERGE_TM = 256


def _merge(ya, yb_all, ys, yn, p_gate, w_branch, w_out, x, gate, g_post, *, row_off):
    B, Tx, D = x.shape
    tm = MERGE_TM
    assert Tx % tm == 0 and row_off % tm == 0
    off = row_off // tm
    y_spec = pl.BlockSpec((1, tm, BRANCH_W), lambda b, i: (b, i, 0))
    yb_spec = pl.BlockSpec((1, tm, BRANCH_W), lambda b, i: (b, off + i, 0))
    s_spec = lambda k: pl.BlockSpec((1, tm, D), lambda b, i: (b, off + i, k))
    resident = lambda shape: pl.BlockSpec(shape, lambda b, i: (0,) * len(shape), pipeline_mode=pl.Buffered(1))
    return pl.pallas_call(
        _merge_kernel,
        grid=(B, Tx // tm),
        in_specs=[y_spec, yb_spec, y_spec, y_spec] + [s_spec(k) for k in range(N_BRANCH)]
                 + [resident((N_BRANCH, BRANCH_W, D)), resident((D, D)),
                    pl.BlockSpec((1, tm, D), lambda b, i: (b, i, 0)),
                    pl.BlockSpec((1, 1, D), lambda b, i: (b, 0, 0)),
                    pl.BlockSpec((1, D), lambda b, i: (0, 0))],
        out_specs=pl.BlockSpec((1, tm, D), lambda b, i: (b, i, 0)),
        out_shape=jax.ShapeDtypeStruct((B, Tx, D), F32),
        compiler_params=_cparams(("arbitrary", "arbitrary")),
        name="merge_out_proj",
    )(ya, yb_all, ys, yn, p_gate, p_gate, p_gate, p_gate, w_branch, w_out, x, gate, g_post)


def _ffn_kernel(x_ref, g1_ref, sh_ref, sc_ref, gate_ref, g2_ref, w1_ref, w2_ref, o_ref, h_scr, acc):
    f = pl.program_id(2)

    @pl.when(f == 0)
    def _():
        h = _rms(x_ref[0], g1_ref[...]) * (1.0 + sc_ref[0]) + sh_ref[0]
        h_scr[...] = h.astype(BF16)

    a = jnp.square(jnp.maximum(_dot(h_scr[...], w1_ref[...]), 0.0)).astype(BF16)
    contrib = _dot(a, w2_ref[...])

    @pl.when(f == 0)
    def _():
        acc[...] = contrib

    @pl.when(f > 0)
    def _():
        acc[...] += contrib

    @pl.when(f == pl.num_programs(2) - 1)
    def _():
        o_ref[0] = x_ref[0] + gate_ref[0] * _rms(acc[...], g2_ref[...])


def _ffn(x, g_pre, shift, scale, gate, g_post, w1, w2, *, tm, tf=1024):
    B, T, D = x.shape
    F = w1.shape[1]
    tm = min(tm, T)
    mod_spec = pl.BlockSpec((1, 1, D), lambda b, i, f: (b, 0, 0))
    vec_spec = pl.BlockSpec((1, D), lambda b, i, f: (0, 0))
    return pl.pallas_call(
        _ffn_kernel,
        grid=(B, T // tm, F // tf),
        in_specs=[pl.BlockSpec((1, tm, D), lambda b, i, f: (b, i, 0)),
                  vec_spec, mod_spec, mod_spec, mod_spec, vec_spec,
                  pl.BlockSpec((D, tf), lambda b, i, f: (0, f)),
                  pl.BlockSpec((tf, D), lambda b, i, f: (f, 0))],
        out_specs=pl.BlockSpec((1, tm, D), lambda b, i, f: (b, i, 0)),
        out_shape=jax.ShapeDtypeStruct((B, T, D), F32),
        scratch_shapes=[pltpu.VMEM((tm, D), BF16), pltpu.VMEM((tm, D), F32)],
        compiler_params=_cparams(("arbitrary", "arbitrary", "arbitrary")),
        name="mlp",
    )(x, g_pre, shift, scale, gate, g_post, w1, w2)


def _layout_w_in(w):
    D = w.shape[0]
    o = 0
    parts = {}
    for name, width in (("aq", 512), ("ak", 512), ("av", 512), ("bz", 512), ("bx", SSM_XBC), ("bdt", 16),
                        ("sq", 512), ("sk", 128), ("sv", 128), ("nq", 512), ("nk", 512), ("nv", 512),
                        ("gate", N_BRANCH * D)):
        parts[name] = w[:, o:o + width]
        o += width
    assert o == w.shape[1]
    sq = parts["sq"].reshape(D, SW_KV, 4, HEAD_DIM).transpose(0, 2, 1, 3).reshape(D, 512)
    zpad = jnp.zeros((D, LANES - SSM_HEADS), w.dtype)
    misc = jnp.concatenate([parts["sk"], parts["sv"], parts["bdt"][:, :SSM_HEADS], zpad,
                            parts["bdt"][:, SSM_HEADS:], zpad], axis=1)
    w_rope = jnp.concatenate([parts["aq"] * (QSCALE * LOG2E), parts["ak"], sq * QSCALE, misc], axis=1)
    w_plain = jnp.concatenate([parts["bx"], parts["nq"] * QSCALE, parts["nk"], parts["nv"], parts["av"],
                               parts["bz"]], axis=1)
    return w_rope.astype(BF16), w_plain.astype(BF16), parts["gate"].astype(BF16)


def _rope_tables(n, n_ctx):
    t = np.arange(n)
    nf = HEAD_DIM // 4
    inv = ROPE_BASE ** (-np.arange(nf, dtype=np.float32) / nf)
    row = (t // GRID_W).astype(np.float32)
    col = (t % GRID_W).astype(np.float32)
    ang = jnp.asarray(np.concatenate([row[:, None] * inv, col[:, None] * inv], -1).astype(np.float32))
    cos, sin = jnp.cos(ang), jnp.sin(ang)
    cos = jnp.concatenate([cos, jnp.ones((n_ctx, cos.shape[1]), F32)], axis=0)
    sin = jnp.concatenate([sin, jnp.zeros((n_ctx, sin.shape[1]), F32)], axis=0)
    z = jnp.zeros_like(sin)
    rep = LANES // HEAD_DIM
    cos_t = jnp.tile(jnp.concatenate([cos, cos], -1), (1, rep))
    s1 = jnp.tile(jnp.concatenate([-sin, z], -1), (1, rep))
    s2 = jnp.tile(jnp.concatenate([z, sin], -1), (1, rep))
    return cos_t, s1, s2


def kernel(x, c, ctx, c_ctx, w_mod, b_mod, g_pre_mix, g_post_mix, g_pre_mlp, g_post_mlp, w_in, lam_q1, lam_k1, lam_q2, lam_k2, g_subln, conv_w, conv_b, dt_bias, a_log, d_skip, g_ssm, sink, rpb, w_branch, w_out, w_ff1, w_ff2):
    B, N, D = x.shape
    L = ctx.shape[1]
    T = N + L
    depth = w_mod.shape[0]
    assert B + 1 <= 8 and N % GRID_W == 0 and N % L == 0
    R = N // GRID_W

    cvec = jnp.zeros((8, D), F32).at[:B].set(c).at[B].set(c_ctx)
    mods = _modulation(cvec, w_mod, b_mod)
    rope_tabs = _rope_tables(N, L)
    na_bias = _na_bias(_na_col_bias(rpb), R, L)
    no_sink = jnp.full((NA_HEADS,), NEG, F32)
    row = lambda v: v.reshape(1, -1).astype(F32)
    tk_lat = _largest_divisor(T, (768, 512, 256, 128))

    cx = ctx
    for l in range(depth):
        need_ctx = l < depth - 1
        lam_init = 0.8 - 0.6 * math.exp(-0.3 * l)
        mod = mods[l, :B].reshape(B, 6, 1, D)
        modc = jnp.broadcast_to(mods[l, B].reshape(1, 6, 1, D), (B, 6, 1, D))
        both = lambda k: jnp.stack([mod[:, k], modc[:, k]], axis=1).reshape(B * 2, 1, D)
        w_rope, w_plain, w_gate = _layout_w_in(w_in[l])

        h = _prenorm(x, cx, row(g_pre_mix[l]), both(0), both(1))
        p_rope, dtr = _proj_rope(h, rope_tabs, w_rope)
        p_plain = _proj(h, w_plain, gate=False, tn=TN)
        p_gate = _proj(h, w_gate, gate=True, tn=1024)

        lamv = jnp.stack([lam_q1[l], lam_k1[l], lam_q2[l], lam_k2[l]]).astype(F32)
        g_col = g_subln[l].reshape(LANES, 1).astype(F32)
        ya = _diff_attn(p_rope, p_plain, lamv, g_col, lam_init, q_row0=0, n_q=N, k_row0=0, n_k=T, tq=512, tk=tk_lat)

        u = _conv_silu(p_plain, conv_w[l], conv_b[l], N)
        y_f, y_b = _ssd(u, dtr, dt_bias[l], a_log[l], N)
        yb_all = _ssm_finish(y_f, y_b, u, p_plain, d_skip[l], g_ssm[l])

        ys = _swa(p_rope, sink[l].astype(F32), N, L)
        yn = _na(p_plain, na_bias[l], N, L)

        wb = w_branch[l]
        wb = wb.at[2].set(wb[2].reshape(SW_KV, 4, HEAD_DIM, D).transpose(1, 0, 2, 3).reshape(BRANCH_W, D))
        wb = wb.astype(BF16)
        wo = w_out[l].astype(BF16)
        w1 = w_ff1[l].astype(BF16)
        w2 = w_ff2[l].astype(BF16)
        x_new = _merge(ya, yb_all, ys, yn, p_gate, wb, wo, x, mod[:, 2], row(g_post_mix[l]), row_off=0)
        x_new = _ffn(x_new, row(g_pre_mlp[l]), mod[:, 3], mod[:, 4], mod[:, 5], row(g_post_mlp[l]), w1, w2, tm=512)
        if need_ctx:
            ya_c = _diff_attn(p_rope, p_plain, lamv, g_col, lam_init, q_row0=N, n_q=L, k_row0=N, n_k=L, tq=L, tk=L)
            ys_c = _ctx_attend(p_rope, sink[l].astype(F32), R_SQ, R_MISC * 4, R_MISC * 4 + 1, N, L, gqa=True)
            yn_c = _ctx_attend(p_plain, no_sink, P_NQ, P_NK, P_NV, N, L, gqa=False)
            cx = _merge(ya_c, yb_all, ys_c, yn_c, p_gate, wb, wo, cx, modc[:, 2], row(g_post_mix[l]), row_off=N)
            cx = _ffn(cx, row(g_pre_mlp[l]), modc[:, 3], modc[:, 4], modc[:, 5], row(g_post_mlp[l]), w1, w2, tm=L)
        x = x_new
    return x
```

```python
import functools
import math

import numpy as np
import jax
import jax.numpy as jnp
from jax import lax
from jax.experimental import pallas as pl
from jax.experimental.pallas import tpu as pltpu

F32 = jnp.float32
BF16 = jnp.bfloat16

LANES = 128
VMEM_LIMIT = 56 * 1024 * 1024
NEG = -1e30
EPS = 1e-6
GRID_W = 64
ROPE_BASE = 10000.0
HEAD_DIM = 64
QSCALE = HEAD_DIM ** -0.5
LOG2E = math.log2(math.e)
DA_HEADS = 4
SSM_HEADS = 8
SSM_INNER = 512
SSM_GROUPS = 2
SSM_STATE = 128
SSM_XBC = SSM_INNER + 2 * SSM_GROUPS * SSM_STATE
SSM_CONV = 5
CHUNK = 128
SW_KV = 2
SW_BLOCK = 128
NA_HEADS = 8
NA_ROWS = 8
NA_COLS = 16
N_BRANCH = 4
BRANCH_W = 512

TN = 512
R_AQ, R_AK, R_SQ, R_MISC = 0, 1, 2, 3
P_NQ, P_NK, P_NV, P_AV, P_BZ = 2, 3, 4, 5, 6


def _cparams(sem):
    return pltpu.CompilerParams(dimension_semantics=sem, vmem_limit_bytes=VMEM_LIMIT)


def _rms(t, g):
    return t * lax.rsqrt(jnp.mean(t * t, -1, keepdims=True) + EPS) * g


def _dot(a, b):
    return jnp.dot(a, b, preferred_element_type=F32)


def _dot_nt(a, b):
    return lax.dot_general(a, b, (((1,), (1,)), ((), ())), preferred_element_type=F32)


def _dot_tn(a, b):
    return lax.dot_general(a, b, (((0,), (0,)), ((), ())), preferred_element_type=F32)


def _half_mask(hh):
    lane = lax.broadcasted_iota(jnp.int32, (1, LANES), 1)
    return (lane // HEAD_DIM) == hh


def _v_with_ones(v, hh):
    lane = lax.broadcasted_iota(jnp.int32, (1, LANES), 1)
    ones_lane = jnp.where(lane == (1 - hh) * HEAD_DIM, 1.0, 0.0).astype(v.dtype)
    return jnp.where(_half_mask(hh), v, jnp.zeros_like(v)) + ones_lane


def _split_pv(o, hh):
    spare = (1 - hh) * HEAD_DIM
    return jnp.where(_half_mask(hh), o, 0.0), o[:, spare:spare + 1]


def _largest_divisor(n, cands):
    for c in cands:
        if n % c == 0:
            return c
    raise ValueError(f"no tile in {cands} divides {n}")


MOD_KB = 256


def _mod_kernel(c_ref, wa_ref, wb_ref, b_ref, o_ref):
    k = pl.program_id(1)
    c = c_ref[...]
    s = (c * jax.nn.sigmoid(c)).astype(BF16)
    half = MOD_KB // 2
    part = _dot(s[:, :half], wa_ref[0].astype(BF16)) + _dot(s[:, half:], wb_ref[0].astype(BF16))

    @pl.when(k == 0)
    def _():
        o_ref[0] = part + b_ref[0]

    @pl.when(k > 0)
    def _():
        o_ref[0] += part


def _modulation(cvec, w_mod, b_mod):
    depth, D, C = w_mod.shape
    return pl.pallas_call(
        _mod_kernel,
        grid=(depth, D // MOD_KB),
        in_specs=[pl.BlockSpec((8, MOD_KB), lambda l, k: (0, k)),
                  pl.BlockSpec((1, MOD_KB // 2, C), lambda l, k: (l, 2 * k, 0)),
                  pl.BlockSpec((1, MOD_KB // 2, C), lambda l, k: (l, 2 * k + 1, 0)),
                  pl.BlockSpec((1, 1, C), lambda l, k: (l, 0, 0))],
        out_specs=pl.BlockSpec((1, 8, C), lambda l, k: (l, 0, 0)),
        out_shape=jax.ShapeDtypeStruct((depth, 8, C), F32),
        compiler_params=_cparams(("arbitrary", "arbitrary")),
        name="modulation",
    )(cvec, w_mod, w_mod, b_mod.reshape(depth, 1, C))


NORM_BLK = 256


def _prenorm_kernel(x_ref, cx_ref, g_ref, sh_ref, sc_ref, o_ref, *, nlat):
    i = pl.program_id(1)

    def emit(t):
        o_ref[0] = (_rms(t, g_ref[...]) * (1.0 + sc_ref[0]) + sh_ref[0]).astype(BF16)

    @pl.when(i < nlat)
    def _():
        emit(x_ref[0])

    @pl.when(i >= nlat)
    def _():
        emit(cx_ref[0])


def _prenorm(x, cx, g, shift2, scale2):
    B, N, D = x.shape
    L = cx.shape[1]
    assert N % NORM_BLK == 0 and L % NORM_BLK == 0
    nlat, nctx = N // NORM_BLK, L // NORM_BLK
    mod_spec = pl.BlockSpec((1, 1, D), lambda b, i: (b * 2 + (i >= nlat).astype(jnp.int32), 0, 0))
    return pl.pallas_call(
        functools.partial(_prenorm_kernel, nlat=nlat),
        grid=(B, nlat + nctx),
        in_specs=[pl.BlockSpec((1, NORM_BLK, D), lambda b, i: (b, jnp.minimum(i, nlat - 1), 0)),
                  pl.BlockSpec((1, NORM_BLK, D), lambda b, i: (b, jnp.maximum(i - nlat, 0), 0)),
                  pl.BlockSpec((1, D), lambda b, i: (0, 0)),
                  mod_spec, mod_spec],
        out_specs=pl.BlockSpec((1, NORM_BLK, D), lambda b, i: (b, i, 0)),
        out_shape=jax.ShapeDtypeStruct((B, N + L, D), BF16),
        compiler_params=_cparams(("arbitrary", "arbitrary")),
        name="prenorm",
    )(x, cx, g, shift2, scale2)


def _rope(t, cos, s1, s2):
    w = t.shape[1]
    reps = w // LANES
    tile = lambda a: a if reps == 1 else jnp.concatenate([a] * reps, axis=1)
    return (t * tile(cos) + pltpu.roll(t, w - HEAD_DIM // 2, 1) * tile(s1)
            + pltpu.roll(t, HEAD_DIM // 2, 1) * tile(s2))


def _proj_rope_kernel(h_ref, cos_ref, s1_ref, s2_ref, w_ref, o_ref, odt_ref):
    j = pl.program_id(2)
    acc = _dot(h_ref[0], w_ref[...])
    rot = lambda t: _rope(t, cos_ref[...], s1_ref[...], s2_ref[...])

    @pl.when(j != R_MISC)
    def _():
        o_ref[0] = rot(acc).astype(BF16)

    @pl.when(j == R_MISC)
    def _():
        o_ref[0] = jnp.concatenate([rot(acc[:, :LANES]), acc[:, LANES:]], axis=1).astype(BF16)
        odt_ref[0] = acc[:, 2 * LANES:]


def _row_tile(T):
    return _largest_divisor(T, (1056, 1024, 768, 640, 512, 256, 128))


def _proj_rope(h, rope_tabs, w):
    B, T, D = h.shape
    C = w.shape[1]
    tm = _row_tile(T)
    cos, s1, s2 = rope_tabs
    tab_spec = pl.BlockSpec((tm, LANES), lambda b, i, j: (i, 0))
    return pl.pallas_call(
        _proj_rope_kernel,
        grid=(B, T // tm, C // TN),
        in_specs=[pl.BlockSpec((1, tm, D), lambda b, i, j: (b, i, 0)),
                  tab_spec, tab_spec, tab_spec,
                  pl.BlockSpec((D, TN), lambda b, i, j: (0, j))],
        out_specs=[pl.BlockSpec((1, tm, TN), lambda b, i, j: (b, i, j)),
                   pl.BlockSpec((1, tm, 2 * LANES), lambda b, i, j: (b, i, 0))],
        out_shape=[jax.ShapeDtypeStruct((B, T, C), BF16),
                   jax.ShapeDtypeStruct((B, T, 2 * LANES), F32)],
        compiler_params=_cparams(("arbitrary", "arbitrary", "arbitrary")),
        name="proj_rope",
    )(h, cos, s1, s2, w)


def _proj_kernel(h_ref, w_ref, o_ref, *, gate):
    acc = _dot(h_ref[0], w_ref[...])
    o_ref[0] = (jax.nn.sigmoid(acc) if gate else acc).astype(BF16)


def _proj(h, w, *, gate, tn):
    B, T, D = h.shape
    C = w.shape[1]
    tm = _row_tile(T)
    return pl.pallas_call(
        functools.partial(_proj_kernel, gate=gate),
        grid=(B, T // tm, C // tn),
        in_specs=[pl.BlockSpec((1, tm, D), lambda b, i, j: (b, i, 0)),
                  pl.BlockSpec((D, tn), lambda b, i, j: (0, j))],
        out_specs=pl.BlockSpec((1, tm, tn), lambda b, i, j: (b, i, j)),
        out_shape=jax.ShapeDtypeStruct((B, T, C), BF16),
        compiler_params=_cparams(("arbitrary", "arbitrary", "arbitrary")),
        name="proj_gate" if gate else "proj_plain",
    )(h, w)


VT_ROWS = LANES + 16


def _diff_attn_kernel(lam_ref, g_ref, q_ref, k_ref, v_ref, o_ref, s_scr, vt_scr, *, lam_init, tk):
    q = q_ref[0]
    tq = q.shape[0]
    nk = k_ref.shape[1] // tk
    q_m = [jnp.where(_half_mask(m), q, jnp.zeros_like(q)) for m in range(2)]

    @pl.when(pl.program_id(2) == 0)
    def _():
        ones_row = (lax.broadcasted_iota(jnp.int32, (VT_ROWS - LANES, tk), 0) == 0).astype(BF16)
        for c in range(nk):
            v_t = v_ref[0, c * tk:(c + 1) * tk, :].astype(F32).T.astype(BF16)
            vt_scr[c] = jnp.concatenate([v_t, ones_row], axis=0)

    def scores(c):
        kc = k_ref[0, c * tk:(c + 1) * tk, :]
        for m in range(2):
            s_scr[c % 2, m] = _dot_nt(kc, q_m[m])

    carry = [(jnp.full((1, tq), -jnp.inf, F32), jnp.zeros((VT_ROWS, tq), F32)) for _ in range(2)]
    scores(0)
    for c in range(nk):
        if c + 1 < nk:
            scores(c + 1)
        vc = vt_scr[c]
        for m in range(2):
            mx, acc = carry[m]
            m_new = jnp.maximum(mx, jnp.max(s_scr[c % 2, m], 0, keepdims=True))
            p = jnp.exp2((s_scr[c % 2, m] - m_new).astype(BF16))
            acc = jnp.exp2(mx - m_new) * acc + _dot(vc, p)
            carry[m] = (m_new, acc)
    (_, a0), (_, a1) = carry
    lv = lam_ref[...]
    lam = (jnp.exp(jnp.sum(lv[0:1] * lv[1:2], -1, keepdims=True))
           - jnp.exp(jnp.sum(lv[2:3] * lv[3:4], -1, keepdims=True)) + lam_init)
    o = a0[:LANES] / a0[LANES:LANES + 1] - lam * (a1[:LANES] / a1[LANES:LANES + 1])
    o = o * lax.rsqrt(jnp.mean(o * o, 0, keepdims=True) + EPS) * g_ref[...] * (1.0 - lam_init)
    o_ref[0] = o.T.astype(BF16)


def _diff_attn(p_rope, p_plain, lamv, g_col, lam_init, *, q_row0, n_q, k_row0, n_k, tq, tk):
    B = p_rope.shape[0]
    tq = min(tq, n_q)
    assert n_q % tq == 0 and q_row0 % tq == 0 and k_row0 % n_k == 0 and n_k % tk == 0
    qb, kb = q_row0 // tq, k_row0 // n_k
    return pl.pallas_call(
        functools.partial(_diff_attn_kernel, lam_init=lam_init, tk=tk),
        grid=(B, DA_HEADS, n_q // tq),
        in_specs=[pl.BlockSpec((4, HEAD_DIM), lambda b, h, i: (0, 0)),
                  pl.BlockSpec((LANES, 1), lambda b, h, i: (0, 0)),
                  pl.BlockSpec((1, tq, LANES), lambda b, h, i: (b, qb + i, R_AQ * 4 + h)),
                  pl.BlockSpec((1, n_k, LANES), lambda b, h, i: (b, kb, R_AK * 4 + h)),
                  pl.BlockSpec((1, n_k, LANES), lambda b, h, i: (b, kb, P_AV * 4 + h))],
        out_specs=pl.BlockSpec((1, tq, LANES), lambda b, h, i: (b, i, h)),
        out_shape=jax.ShapeDtypeStruct((B, n_q, DA_HEADS * LANES), BF16),
        scratch_shapes=[pltpu.VMEM((2, 2, tk, tq), F32),
                        pltpu.VMEM((n_k // tk, VT_ROWS, tk), BF16)],
        compiler_params=_cparams(("arbitrary",) * 3),
        name="diff_attn",
    )(lamv, g_col, p_rope, p_rope, p_plain)


def _swa_kernel(sink_ref, q_ref, kp_ref, kc_ref, kn_ref, vp_ref, vc_ref, vn_ref, kx_ref, vx_ref, o_ref):
    i = pl.program_id(1)
    nb = pl.num_programs(1)
    blk = SW_BLOCK
    kband = jnp.concatenate([kp_ref[0], kc_ref[0], kn_ref[0], kx_ref[0]], axis=0)
    vband = jnp.concatenate([vp_ref[0], vc_ref[0], vn_ref[0], vx_ref[0]], axis=0)
    nkeys = kband.shape[0]
    row = lax.broadcasted_iota(jnp.int32, (blk, nkeys), 0)
    col = lax.broadcasted_iota(jnp.int32, (blk, nkeys), 1)
    big = 4 * blk
    off_prev = jnp.where(i > 0, 0, big)
    off_next = jnp.where(i < nb - 1, 0, big)
    ok_prev = jnp.where(col >= row + off_prev, 0.0, NEG)
    ok_next = jnp.where(col - 2 * blk <= row - off_next, 0.0, NEG)
    bias = jnp.where(col < blk, ok_prev, jnp.where((col >= 2 * blk) & (col < 3 * blk), ok_next, 0.0))
    bias = jnp.concatenate([bias] * 4, axis=0)
    q = q_ref[0]
    outs = [None] * 4

    def scores(kv):
        qs = jnp.concatenate([jnp.where(_half_mask(kv), q[:, g * LANES:(g + 1) * LANES],
                                        jnp.zeros((blk, LANES), BF16)) for g in range(4)], axis=0)
        return _dot_nt(qs, kband) + bias

    s_all = [scores(kv) for kv in range(SW_KV)]
    for kv in range(SW_KV):
        s = s_all[kv]
        sink = jnp.concatenate([jnp.full((blk, 1), sink_ref[kv * 4 + g] * LOG2E, F32) for g in range(4)], axis=0)
        m = jnp.maximum(jnp.max(s, -1, keepdims=True), sink)
        p = jnp.exp2((s - m).astype(BF16))
        pv, l = _split_pv(_dot(p, _v_with_ones(vband, kv)), kv)
        o = pv / (l + jnp.exp2(sink - m))
        for g in range(4):
            part = o[g * blk:(g + 1) * blk]
            outs[g] = part if outs[g] is None else outs[g] + part
    o_ref[0] = jnp.concatenate(outs, axis=1).astype(BF16)


def _swa(p_rope, sink, N, L):
    B = p_rope.shape[0]
    nb = N // SW_BLOCK
    kcol, vcol = R_MISC * 4, R_MISC * 4 + 1
    kv_spec = lambda colblk, off: pl.BlockSpec(
        (1, SW_BLOCK, LANES), lambda b, i: (b, jnp.clip(i + off, 0, nb - 1), colblk))
    return pl.pallas_call(
        _swa_kernel,
        grid=(B, nb),
        in_specs=[pl.BlockSpec(memory_space=pltpu.SMEM),
                  pl.BlockSpec((1, SW_BLOCK, TN), lambda b, i: (b, i, R_SQ)),
                  kv_spec(kcol, -1), kv_spec(kcol, 0), kv_spec(kcol, 1),
                  kv_spec(vcol, -1), kv_spec(vcol, 0), kv_spec(vcol, 1),
                  pl.BlockSpec((1, L, LANES), lambda b, i: (b, N // L, kcol)),
                  pl.BlockSpec((1, L, LANES), lambda b, i: (b, N // L, vcol))],
        out_specs=pl.BlockSpec((1, SW_BLOCK, TN), lambda b, i: (b, i, 0)),
        out_shape=jax.ShapeDtypeStruct((B, N, TN), BF16),
        compiler_params=_cparams(("arbitrary", "arbitrary")),
        name="window_attn",
    )(sink, *([p_rope] * 9))


NA_QROWS = 2
NA_KBLKS = 5


def _na_kernel(bias_ref, q_ref, k0, k1, k2, k3, k4, v0, v1, v2, v3, v4, kx_ref, vx_ref, o_ref):
    kall = jnp.concatenate([k0[0], k1[0], k2[0], k3[0], k4[0], kx_ref[0]], axis=0)
    vall = jnp.concatenate([v0[0], v1[0], v2[0], v3[0], v4[0], vx_ref[0]], axis=0)
    q = q_ref[0]
    blocks = [slice(pb * LANES, (pb + 1) * LANES) for pb in range(4)]
    scores = [[_dot_nt(jnp.where(_half_mask(hh), q[:, sl], jnp.zeros((q.shape[0], LANES), BF16)), kall[:, sl])
               + bias_ref[0, pb * 2 + hh] for hh in range(2)] for pb, sl in enumerate(blocks)]
    outs = []
    for pb, sl in enumerate(blocks):
        vb = vall[:, sl]
        ob = None
        for hh in range(2):
            s = scores[pb][hh]
            p = jnp.exp2((s - jnp.max(s, -1, keepdims=True)).astype(BF16))
            pv, l = _split_pv(_dot(p, _v_with_ones(vb, hh)), hh)
            o = pv / l
            ob = o if ob is None else ob + o
        outs.append(ob)
    o_ref[0] = jnp.concatenate(outs, axis=1).astype(BF16)


def _col_bias_kernel(r_ref, sel_ref, neg_ref, o_ref):
    o_ref[...] = sum(_dot(part, sel_ref[...]) for part in reversed(_split3(r_ref[...]))) * LOG2E + neg_ref[...]


def _na_col_bias(rpb_all):
    W = GRID_W
    nco = 2 * NA_COLS - 1
    lead = rpb_all.shape[:3]
    rows = int(np.prod(lead))
    c = np.arange(W)
    cb = np.clip(c - NA_COLS // 2, 0, W - NA_COLS)
    dc = c[None, :] - cb[:, None]
    inside = (dc >= 0) & (dc < NA_COLS)
    co = c[None, :] - c[:, None] + NA_COLS - 1
    sel = ((co[None] == np.arange(32)[:, None, None]) & inside[None]).reshape(32, W * W)
    neg = np.where(inside, 0.0, NEG).reshape(1, W * W).astype(np.float32)
    table = jnp.pad(rpb_all.astype(F32).reshape(rows, nco), ((0, (-rows) % 8), (0, 32 - nco)))
    out = pl.pallas_call(
        _col_bias_kernel,
        out_shape=jax.ShapeDtypeStruct((table.shape[0], W * W), F32),
        name="na_col_bias",
    )(table, jnp.asarray(sel, BF16), jnp.asarray(neg))
    return out[:rows].reshape(lead + (W, W))


def _na_bias(t2, R, L):
    W = GRID_W
    depth, H = t2.shape[:2]
    clamp = lambda v, lo, hi: min(max(v, lo), hi)
    nblk = R // NA_QROWS
    slots = NA_KBLKS * NA_QROWS
    per = []
    for i in (0, 1, 2, nblk - 2, nblk - 1):
        base_blk = clamp(i - 2, 0, nblk - NA_KBLKS)
        for t in range(NA_QROWS):
            rq = NA_QROWS * i + t
            base = clamp(rq - NA_ROWS // 2, 0, R - NA_ROWS)
            s_lo = base - NA_QROWS * base_blk
            assert 0 <= s_lo and s_lo + NA_ROWS <= slots
            ro = base - rq + NA_ROWS - 1
            per.append(jnp.pad(t2[:, :, ro:ro + NA_ROWS],
                               ((0, 0), (0, 0), (s_lo, slots - NA_ROWS - s_lo), (0, 0), (0, 0)),
                               constant_values=NEG))
    b = jnp.stack(per, axis=2).reshape(depth, H, 5, NA_QROWS, slots, W, W)
    b = b.transpose(0, 2, 1, 3, 5, 4, 6).reshape(depth, 5, H, NA_QROWS * W, slots * W)
    return jnp.concatenate([b, jnp.zeros(b.shape[:4] + (L,), F32)], axis=-1)


def _na(p_plain, bias, N, L):
    B = p_plain.shape[0]
    rows = NA_QROWS * GRID_W
    nblk = N // rows
    assert nblk >= NA_KBLKS + 2
    cls = lambda i: jnp.where(i < 2, i, jnp.where(i >= nblk - 2, i - (nblk - 5), 2))
    kv_spec = lambda tile, s: pl.BlockSpec(
        (1, rows, TN), lambda b, i: (b, jnp.clip(i - 2, 0, nblk - NA_KBLKS) + s, tile))
    nkeys = NA_KBLKS * rows + L
    return pl.pallas_call(
        _na_kernel,
        grid=(B, nblk),
        in_specs=[pl.BlockSpec((1, NA_HEADS, rows, nkeys), lambda b, i: (cls(i), 0, 0, 0)),
                  pl.BlockSpec((1, rows, TN), lambda b, i: (b, i, P_NQ))]
                 + [kv_spec(P_NK, s) for s in range(NA_KBLKS)]
                 + [kv_spec(P_NV, s) for s in range(NA_KBLKS)]
                 + [pl.BlockSpec((1, L, TN), lambda b, i: (b, N // L, P_NK)),
                    pl.BlockSpec((1, L, TN), lambda b, i: (b, N // L, P_NV))],
        out_specs=pl.BlockSpec((1, rows, TN), lambda b, i: (b, i, 0)),
        out_shape=jax.ShapeDtypeStruct((B, N, TN), BF16),
        compiler_params=_cparams(("arbitrary", "arbitrary")),
        name="neighbourhood_attn",
    )(bias, *([p_plain] * (2 * NA_KBLKS + 3)))


def _ctx_attend_kernel(sink_ref, q_ref, k_ref, v_ref, o_ref, *, gqa):
    q = q_ref[0]
    k = k_ref[0]
    v = v_ref[0]
    outs = []
    for blk in range(4):
        sl = slice(blk * LANES, (blk + 1) * LANES)
        qb = q[:, sl]
        kb, vb = (k, v) if gqa else (k[:, sl], v[:, sl])
        ob = None
        for hh in range(2):
            head = hh * 4 + blk if gqa else blk * 2 + hh
            s = _dot_nt(jnp.where(_half_mask(hh), qb, jnp.zeros_like(qb)), kb)
            sink = sink_ref[head] * LOG2E
            m = jnp.maximum(jnp.max(s, -1, keepdims=True), sink)
            p = jnp.exp2((s - m).astype(BF16))
            pv, l = _split_pv(_dot(p, _v_with_ones(vb, hh)), hh)
            o = pv / (l + jnp.exp2(sink - m))
            ob = o if ob is None else ob + o
        outs.append(ob)
    o_ref[0] = jnp.concatenate(outs, axis=1).astype(BF16)


def _ctx_attend(p, sink, q_tile, k_blk, v_blk, N, L, *, gqa):
    B = p.shape[0]
    kw = LANES if gqa else TN
    rb = N // L
    return pl.pallas_call(
        functools.partial(_ctx_attend_kernel, gqa=gqa),
        grid=(B,),
        in_specs=[pl.BlockSpec(memory_space=pltpu.SMEM),
                  pl.BlockSpec((1, L, TN), lambda b: (b, rb, q_tile)),
                  pl.BlockSpec((1, L, kw), lambda b: (b, rb, k_blk)),
                  pl.BlockSpec((1, L, kw), lambda b: (b, rb, v_blk))],
        out_specs=pl.BlockSpec((1, L, TN), lambda b: (b, 0, 0)),
        out_shape=jax.ShapeDtypeStruct((B, L, TN), BF16),
        compiler_params=_cparams(("arbitrary",)),
        name="ctx_attend_gqa" if gqa else "ctx_attend",
    )(sink, p, p, p)


CONV_BLK = 256
HALO = 16


def _conv_kernel(prev_ref, cur_ref, next_ref, w_ref, b_ref, o_ref, *, lat_blocks):
    i = pl.program_id(1)
    last = pl.num_programs(1) - 1
    has_prev = (i != 0) & (i != lat_blocks)
    has_next = (i != lat_blocks - 1) & (i != last)
    prev = prev_ref[0].astype(F32) * has_prev.astype(F32)
    nxt = next_ref[0].astype(F32) * has_next.astype(F32)
    ext = jnp.concatenate([prev, cur_ref[0].astype(F32), nxt], axis=0)
    n = ext.shape[0]
    w = w_ref[...]
    acc = None
    for k in range(SSM_CONV):
        shift = (SSM_CONV // 2 - k) % n
        t = (ext if shift == 0 else pltpu.roll(ext, shift, 0))[HALO:HALO + CONV_BLK] * w[k:k + 1]
        acc = t if acc is None else acc + t
    acc = acc + b_ref[...]
    o_ref[0] = (acc * jax.nn.sigmoid(acc)).astype(BF16)


def _conv_silu(p_plain, conv_w, conv_b, N):
    B, T, _ = p_plain.shape
    C = SSM_XBC
    assert N % CONV_BLK == 0 and T % CONV_BLK == 0
    per = CONV_BLK // HALO
    nh = T // HALO
    w = jnp.zeros((8, C), F32).at[:SSM_CONV].set(conv_w.astype(F32))
    return pl.pallas_call(
        functools.partial(_conv_kernel, lat_blocks=N // CONV_BLK),
        grid=(B, T // CONV_BLK),
        in_specs=[pl.BlockSpec((1, HALO, C), lambda b, i: (b, jnp.maximum(i * per - 1, 0), 0)),
                  pl.BlockSpec((1, CONV_BLK, C), lambda b, i: (b, i, 0)),
                  pl.BlockSpec((1, HALO, C), lambda b, i: (b, jnp.minimum((i + 1) * per, nh - 1), 0)),
                  pl.BlockSpec((8, C), lambda b, i: (0, 0)),
                  pl.BlockSpec((1, C), lambda b, i: (0, 0))],
        out_specs=pl.BlockSpec((1, CONV_BLK, C), lambda b, i: (b, i, 0)),
        out_shape=jax.ShapeDtypeStruct((B, T, C), BF16),
        compiler_params=_cparams(("arbitrary", "arbitrary")),
        name="ssm_conv",
    )(p_plain, p_plain, p_plain, w, conv_b.reshape(1, C).astype(F32))


def _split3(t):
    hi = t.astype(BF16)
    r = t - hi.astype(F32)
    mid = r.astype(BF16)
    return hi, mid, (r - mid.astype(F32)).astype(BF16)


def _ssd_chain(u, dtr, dtb, alog, tri, expm, state, *, fwd):
    last = CHUNK - 1 if fwd else 0
    xs = u[:, :SSM_INNER].astype(F32)
    bm = u[:, SSM_INNER:SSM_INNER + SSM_GROUPS * SSM_STATE]
    cm = u[:, SSM_INNER + SSM_GROUPS * SSM_STATE:]
    dt = jax.nn.softplus(dtr + dtb)
    a = dt * (-jnp.exp(alog))
    acum = sum(_dot(tri, part) for part in reversed(_split3(a)))
    eac = jnp.exp(acum)
    dst = jnp.exp(acum[last:last + 1] - acum)
    e3 = sum(_dot(part, expm) for part in reversed(_split3(jnp.concatenate([dt, eac, dst], axis=0))))
    dt_e, eac_e, dst_e = e3[:CHUNK], e3[CHUNK:2 * CHUNK], e3[2 * CHUNK:]
    xg = xs * dt_e
    xg_b = xg.astype(BF16)
    xgd_b = (xg * dst_e).astype(BF16)
    acum_t = acum.T
    row = lax.broadcasted_iota(jnp.int32, (CHUNK, CHUNK), 0)
    col = lax.broadcasted_iota(jnp.int32, (CHUNK, CHUNK), 1)
    keep = (row >= col) if fwd else (row <= col)
    state_b = state.astype(BF16)
    gw = SSM_INNER // SSM_GROUPS
    ys, st_new = [], []
    for g in range(SSM_GROUPS):
        bmg = bm[:, g * SSM_STATE:(g + 1) * SSM_STATE]
        cmg = cm[:, g * SSM_STATE:(g + 1) * SSM_STATE]
        cb = _dot_nt(cmg, bmg)
        yoff = _dot(cmg, state_b[:, g * gw:(g + 1) * gw])
        st_new.append(_dot_tn(bmg, xgd_b[:, g * gw:(g + 1) * gw]))
        for pb in range(gw // LANES):
            blk = g * (gw // LANES) + pb
            sl = slice(blk * LANES, (blk + 1) * LANES)
            xgb = xg_b[:, sl]
            yd = None
            for hh in range(2):
                h = blk * 2 + hh
                seg = acum[:, h:h + 1] - acum_t[h:h + 1, :]
                ld = jnp.exp(jnp.where(keep, seg, -jnp.inf))
                t = _dot((cb * ld).astype(BF16), jnp.where(_half_mask(hh), xgb, jnp.zeros_like(xgb)))
                yd = t if yd is None else yd + t
            ys.append(yd + yoff[:, pb * LANES:(pb + 1) * LANES] * eac_e[:, sl])
    y = jnp.concatenate(ys, axis=1)
    return y, state * eac_e[last:last + 1] + jnp.concatenate(st_new, axis=1)


def _ssd_kernel(uf_ref, ub_ref, dtf_ref, dtb_ref, bias_ref, alog_ref, tri_ref, exp_ref, yf_ref, yb_ref, h_scr):
    @pl.when(pl.program_id(0) == 0)
    def _():
        h_scr[...] = jnp.zeros(h_scr.shape, F32)

    for b in range(uf_ref.shape[0]):
        for d, (u_ref, dt_ref, y_ref) in enumerate(((uf_ref, dtf_ref, yf_ref), (ub_ref, dtb_ref, yb_ref))):
            y, st = _ssd_chain(u_ref[b], dt_ref[b], bias_ref[d], alog_ref[d], tri_ref[d], exp_ref[...],
                               h_scr[b, d], fwd=(d == 0))
            y_ref[b] = y
            h_scr[b, d] = st


def _ssd(u, dtr, dt_bias, a_log, N):
    B, T, _ = u.shape
    nc, nlat = T // CHUNK, N // CHUNK
    ncx = nc - nlat
    chunk_f = lambda c: jnp.where(c < ncx, nlat + c, c - ncx)
    chunk_b = lambda c: jnp.where(c < ncx, nlat + ncx - 1 - c, nlat - 1 - (c - ncx))
    pad = lambda t: jnp.zeros((2, 1, LANES), F32).at[:, 0, :SSM_HEADS].set(t.astype(F32))
    li = np.arange(CHUNK)
    tri = np.stack([li[:, None] >= li[None, :], li[:, None] <= li[None, :]]).astype(np.float32)
    expm = np.zeros((LANES, SSM_INNER), np.float32)
    for h in range(SSM_HEADS):
        expm[h, h * 64:(h + 1) * 64] = 1.0
    full = lambda shape: pl.BlockSpec(shape, lambda c: (0,) * len(shape))
    y_shape = jax.ShapeDtypeStruct((B, T, SSM_INNER), F32)
    return pl.pallas_call(
        _ssd_kernel,
        grid=(nc,),
        in_specs=[pl.BlockSpec((B, CHUNK, SSM_XBC), lambda c: (0, chunk_f(c), 0)),
                  pl.BlockSpec((B, CHUNK, SSM_XBC), lambda c: (0, chunk_b(c), 0)),
                  pl.BlockSpec((B, CHUNK, LANES), lambda c: (0, chunk_f(c), 0)),
                  pl.BlockSpec((B, CHUNK, LANES), lambda c: (0, chunk_b(c), 1)),
                  full((2, 1, LANES)), full((2, 1, LANES)), full((2, CHUNK, CHUNK)), full((LANES, SSM_INNER))],
        out_specs=[pl.BlockSpec((B, CHUNK, SSM_INNER), lambda c: (0, chunk_f(c), 0)),
                   pl.BlockSpec((B, CHUNK, SSM_INNER), lambda c: (0, chunk_b(c), 0))],
        out_shape=[y_shape, y_shape],
        scratch_shapes=[pltpu.VMEM((B, 2, SSM_STATE, SSM_INNER), F32)],
        compiler_params=_cparams(("arbitrary",)),
        name="ssd_scan",
    )(u, u, dtr, dtr, pad(dt_bias), pad(a_log), jnp.asarray(tri, BF16), jnp.asarray(expm, BF16))


def _ssm_finish_kernel(yf_ref, yb_ref, u_ref, z_ref, dsk_ref, g_ref, o_ref):
    y = yf_ref[0] + yb_ref[0] + dsk_ref[...] * u_ref[0].astype(F32)
    z = z_ref[0].astype(F32)
    o_ref[0] = _rms(y * (z * jax.nn.sigmoid(z)), g_ref[...]).astype(BF16)


def _ssm_finish(y_f, y_b, u, p_plain, d_skip, g_ssm, *, tm=256):
    B, T, W = y_f.shape
    dsk = jnp.repeat(d_skip.astype(F32), W // SSM_HEADS).reshape(1, W)
    row_spec = pl.BlockSpec((1, tm, W), lambda b, i: (b, i, 0))
    return pl.pallas_call(
        _ssm_finish_kernel,
        grid=(B, T // tm),
        in_specs=[row_spec, row_spec, row_spec,
                  pl.BlockSpec((1, tm, W), lambda b, i: (b, i, P_BZ)),
                  pl.BlockSpec((1, W), lambda b, i: (0, 0)),
                  pl.BlockSpec((1, W), lambda b, i: (0, 0))],
        out_specs=row_spec,
        out_shape=jax.ShapeDtypeStruct((B, T, W), BF16),
        compiler_params=_cparams(("arbitrary", "arbitrary")),
        name="ssm_finish",
    )(y_f, y_b, u, p_plain, dsk, g_ssm.reshape(1, W).astype(F32))


def _merge_kernel(y0, y1, y2, y3, s0, s1, s2, s3, wb_ref, wo_ref, x_ref, gate_ref, g_ref, o_ref):
    m = None
    for k, (y, s) in enumerate(((y0, s0), (y1, s1), (y2, s2), (y3, s3))):
        t = _dot(y[0], wb_ref[k]) * s[0].astype(F32)
        m = t if m is None else m + t
    o = _dot(m.astype(BF16), wo_ref[...])
    o_ref[0] = x_ref[0] + gate_ref[0] * _rms(o, g_ref[...])


MERGE_TM = 256


def _merge(ya, yb_all, ys, yn, p_gate, w_branch, w_out, x, gate, g_post, *, row_off):
    B, Tx, D = x.shape
    tm = MERGE_TM
    assert Tx % tm == 0 and row_off % tm == 0
    off = row_off // tm
    y_spec = pl.BlockSpec((1, tm, BRANCH_W), lambda b, i: (b, i, 0))
    yb_spec = pl.BlockSpec((1, tm, BRANCH_W), lambda b, i: (b, off + i, 0))
    s_spec = lambda k: pl.BlockSpec((1, tm, D), lambda b, i: (b, off + i, k))
    resident = lambda shape: pl.BlockSpec(shape, lambda b, i: (0,) * len(shape), pipeline_mode=pl.Buffered(1))
    return pl.pallas_call(
        _merge_kernel,
        grid=(B, Tx // tm),
        in_specs=[y_spec, yb_spec, y_spec, y_spec] + [s_spec(k) for k in range(N_BRANCH)]
                 + [resident((N_BRANCH, BRANCH_W, D)), resident((D, D)),
                    pl.BlockSpec((1, tm, D), lambda b, i: (b, i, 0)),
                    pl.BlockSpec((1, 1, D), lambda b, i: (b, 0, 0)),
                    pl.BlockSpec((1, D), lambda b, i: (0, 0))],
        out_specs=pl.BlockSpec((1, tm, D), lambda b, i: (b, i, 0)),
        out_shape=jax.ShapeDtypeStruct((B, Tx, D), F32),
        compiler_params=_cparams(("arbitrary", "arbitrary")),
        name="merge_out_proj",
    )(ya, yb_all, ys, yn, p_gate, p_gate, p_gate, p_gate, w_branch, w_out, x, gate, g_post)


def _ffn_kernel(x_ref, g1_ref, sh_ref, sc_ref, gate_ref, g2_ref, w1_ref, w2_ref, o_ref, h_scr, acc):
    f = pl.program_id(2)

    @pl.when(f == 0)
    def _():
        h = _rms(x_ref[0], g1_ref[...]) * (1.0 + sc_ref[0]) + sh_ref[0]
        h_scr[...] = h.astype(BF16)
        acc[...] = jnp.zeros(acc.shape, F32)

    a = jnp.square(jnp.maximum(_dot(h_scr[...], w1_ref[...]), 0.0)).astype(BF16)
    acc[...] += _dot(a, w2_ref[...])

    @pl.when(f == pl.num_programs(2) - 1)
    def _():
        o_ref[0] = x_ref[0] + gate_ref[0] * _rms(acc[...], g2_ref[...])


def _ffn(x, g_pre, shift, scale, gate, g_post, w1, w2, *, tm, tf=1024):
    B, T, D = x.shape
    F = w1.shape[1]
    tm = min(tm, T)
    mod_spec = pl.BlockSpec((1, 1, D), lambda b, i, f: (b, 0, 0))
    vec_spec = pl.BlockSpec((1, D), lambda b, i, f: (0, 0))
    return pl.pallas_call(
        _ffn_kernel,
        grid=(B, T // tm, F // tf),
        in_specs=[pl.BlockSpec((1, tm, D), lambda b, i, f: (b, i, 0)),
                  vec_spec, mod_spec, mod_spec, mod_spec, vec_spec,
                  pl.BlockSpec((D, tf), lambda b, i, f: (0, f)),
                  pl.BlockSpec((tf, D), lambda b, i, f: (f, 0))],
        out_specs=pl.BlockSpec((1, tm, D), lambda b, i, f: (b, i, 0)),
        out_shape=jax.ShapeDtypeStruct((B, T, D), F32),
        scratch_shapes=[pltpu.VMEM((tm, D), BF16), pltpu.VMEM((tm, D), F32)],
        compiler_params=_cparams(("arbitrary", "arbitrary", "arbitrary")),
        name="mlp",
    )(x, g_pre, shift, scale, gate, g_post, w1, w2)


def _layout_w_in(w):
    D = w.shape[0]
    o = 0
    parts = {}
    for name, width in (("aq", 512), ("ak", 512), ("av", 512), ("bz", 512), ("bx", SSM_XBC), ("bdt", 16),
                        ("sq", 512), ("sk", 128), ("sv", 128), ("nq", 512), ("nk", 512), ("nv", 512),
                        ("gate", N_BRANCH * D)):
        parts[name] = w[:, o:o + width]
        o += width
    assert o == w.shape[1]
    sq = parts["sq"].reshape(D, SW_KV, 4, HEAD_DIM).transpose(0, 2, 1, 3).reshape(D, 512)
    zpad = jnp.zeros((D, LANES - SSM_HEADS), w.dtype)
    misc = jnp.concatenate([parts["sk"], parts["sv"], parts["bdt"][:, :SSM_HEADS], zpad,
                            parts["bdt"][:, SSM_HEADS:], zpad], axis=1)
    qs = QSCALE * LOG2E
    w_rope = jnp.concatenate([parts["aq"] * qs, parts["ak"], sq * qs, misc], axis=1)
    w_plain = jnp.concatenate([parts["bx"], parts["nq"] * qs, parts["nk"], parts["nv"], parts["av"],
                               parts["bz"]], axis=1)
    return w_rope.astype(BF16), w_plain.astype(BF16), parts["gate"].astype(BF16)


def _rope_tables(n, n_ctx):
    t = np.arange(n)
    nf = HEAD_DIM // 4
    inv = ROPE_BASE ** (-np.arange(nf, dtype=np.float32) / nf)
    row = (t // GRID_W).astype(np.float32)
    col = (t % GRID_W).astype(np.float32)
    ang = jnp.asarray(np.concatenate([row[:, None] * inv, col[:, None] * inv], -1).astype(np.float32))
    cos, sin = jnp.cos(ang), jnp.sin(ang)
    cos = jnp.concatenate([cos, jnp.ones((n_ctx, cos.shape[1]), F32)], axis=0)
    sin = jnp.concatenate([sin, jnp.zeros((n_ctx, sin.shape[1]), F32)], axis=0)
    z = jnp.zeros_like(sin)
    rep = LANES // HEAD_DIM
    cos_t = jnp.tile(jnp.concatenate([cos, cos], -1), (1, rep))
    s1 = jnp.tile(jnp.concatenate([-sin, z], -1), (1, rep))
    s2 = jnp.tile(jnp.concatenate([z, sin], -1), (1, rep))
    return cos_t, s1, s2


def kernel(x, c, ctx, c_ctx, w_mod, b_mod, g_pre_mix, g_post_mix, g_pre_mlp, g_post_mlp, w_in, lam_q1, lam_k1, lam_q2, lam_k2, g_subln, conv_w, conv_b, dt_bias, a_log, d_skip, g_ssm, sink, rpb, w_branch, w_out, w_ff1, w_ff2):
    B, N, D = x.shape
    L = ctx.shape[1]
    T = N + L
    depth = w_mod.shape[0]
    assert B + 1 <= 8 and N % GRID_W == 0 and N % L == 0
    R = N // GRID_W

    cvec = jnp.zeros((8, D), F32).at[:B].set(c).at[B].set(c_ctx)
    mods = _modulation(cvec, w_mod, b_mod)
    rope_tabs = _rope_tables(N, L)
    na_bias = _na_bias(_na_col_bias(rpb), R, L)
    no_sink = jnp.full((NA_HEADS,), NEG, F32)
    row = lambda v: v.reshape(1, -1).astype(F32)
    tk_lat = _largest_divisor(T, (768, 512, 256, 128))

    cx = ctx
    for l in range(depth):
        need_ctx = l < depth - 1
        lam_init = 0.8 - 0.6 * math.exp(-0.3 * l)
        mod = mods[l, :B].reshape(B, 6, 1, D)
        modc = jnp.broadcast_to(mods[l, B].reshape(1, 6, 1, D), (B, 6, 1, D))
        both = lambda k: jnp.stack([mod[:, k], modc[:, k]], axis=1).reshape(B * 2, 1, D)
        w_rope, w_plain, w_gate = _layout_w_in(w_in[l])

        h = _prenorm(x, cx, row(g_pre_mix[l]), both(0), both(1))
        p_rope, dtr = _proj_rope(h, rope_tabs, w_rope)
        p_plain = _proj(h, w_plain, gate=False, tn=TN)
        p_gate = _proj(h, w_gate, gate=True, tn=2048)

        lamv = jnp.stack([lam_q1[l], lam_k1[l], lam_q2[l], lam_k2[l]]).astype(F32)
        g_col = g_subln[l].reshape(LANES, 1).astype(F32)
        ya = _diff_attn(p_rope, p_plain, lamv, g_col, lam_init, q_row0=0, n_q=N, k_row0=0, n_k=T, tq=512, tk=tk_lat)

        u = _conv_silu(p_plain, conv_w[l], conv_b[l], N)
        y_f, y_b = _ssd(u, dtr, dt_bias[l], a_log[l], N)
        yb_all = _ssm_finish(y_f, y_b, u, p_plain, d_skip[l], g_ssm[l])

        ys = _swa(p_rope, sink[l].astype(F32), N, L)
        yn = _na(p_plain, na_bias[l], N, L)

        wb = w_branch[l]
        wb = wb.at[2].set(wb[2].reshape(SW_KV, 4, HEAD_DIM, D).transpose(1, 0, 2, 3).reshape(BRANCH_W, D))
        wb = wb.astype(BF16)
        wo = w_out[l].astype(BF16)
        w1 = w_ff1[l].astype(BF16)
        w2 = w_ff2[l].astype(BF16)
        x_new = _merge(ya, yb_all, ys, yn, p_gate, wb, wo, x, mod[:, 2], row(g_post_mix[l]), row_off=0)
        x_new = _ffn(x_new, row(g_pre_mlp[l]), mod[:, 3], mod[:, 4], mod[:, 5], row(g_post_mlp[l]), w1, w2, tm=512)
        if need_ctx:
            ya_c = _diff_attn(p_rope, p_plain, lamv, g_col, lam_init, q_row0=N, n_q=L, k_row0=N, n_k=L, tq=L, tk=L)
            ys_c = _ctx_attend(p_rope, sink[l].astype(F32), R_SQ, R_MISC * 4, R_MISC * 4 + 1, N, L, gqa=True)
            yn_c = _ctx_attend(p_plain, no_sink, P_NQ, P_NK, P_NV, N, L, gqa=False)
            cx = _merge(ya_c, yb_all, ys_c, yn_c, p_gate, wb, wo, cx, modc[:, 2], row(g_post_mix[l]), row_off=N)
            cx = _ffn(cx, row(g_pre_mlp[l]), modc[:, 3], modc[:, 4], modc[:, 5], row(g_post_mlp[l]), w1, w2, tm=L)
        x = x_new
    return x
```

```python
import functools
import math

import numpy as np
import jax
import jax.numpy as jnp
from jax import lax
from jax.experimental import pallas as pl
from jax.experimental.pallas import tpu as pltpu

F32 = jnp.float32
BF16 = jnp.bfloat16

LANES = 128
VMEM_LIMIT = 56 * 1024 * 1024
NEG = -1e30
EPS = 1e-6
GRID_W = 64
ROPE_BASE = 10000.0
HEAD_DIM = 64
QSCALE = HEAD_DIM ** -0.5
LOG2E = math.log2(math.e)
DA_HEADS = 4
SSM_HEADS = 8
SSM_INNER = 512
SSM_GROUPS = 2
SSM_STATE = 128
SSM_XBC = SSM_INNER + 2 * SSM_GROUPS * SSM_STATE
SSM_CONV = 5
CHUNK = 128
SW_KV = 2
SW_BLOCK = 128
NA_HEADS = 8
NA_ROWS = 8
NA_COLS = 16
N_BRANCH = 4
BRANCH_W = 512

TN = 512
R_AQ, R_AK, R_SQ, R_MISC = 0, 1, 2, 3
P_NQ, P_NK, P_NV, P_AV, P_BZ = 2, 3, 4, 5, 6


def _cparams(sem):
    return pltpu.CompilerParams(dimension_semantics=sem, vmem_limit_bytes=VMEM_LIMIT)


def _rms(t, g):
    return t * lax.rsqrt(jnp.mean(t * t, -1, keepdims=True) + EPS) * g


def _dot(a, b):
    return jnp.dot(a, b, preferred_element_type=F32)


def _dot_nt(a, b):
    return lax.dot_general(a, b, (((1,), (1,)), ((), ())), preferred_element_type=F32)


def _dot_tn(a, b):
    return lax.dot_general(a, b, (((0,), (0,)), ((), ())), preferred_element_type=F32)


def _half_mask(hh):
    lane = lax.broadcasted_iota(jnp.int32, (1, LANES), 1)
    return (lane // HEAD_DIM) == hh


def _v_with_ones(v, hh):
    lane = lax.broadcasted_iota(jnp.int32, (1, LANES), 1)
    ones_lane = jnp.where(lane == (1 - hh) * HEAD_DIM, 1.0, 0.0).astype(v.dtype)
    return jnp.where(_half_mask(hh), v, jnp.zeros_like(v)) + ones_lane


def _split_pv(o, hh):
    spare = (1 - hh) * HEAD_DIM
    return jnp.where(_half_mask(hh), o, 0.0), o[:, spare:spare + 1]


def _largest_divisor(n, cands):
    for c in cands:
        if n % c == 0:
            return c
    raise ValueError(f"no tile in {cands} divides {n}")


MOD_KB = 256


def _mod_kernel(c_ref, wa_ref, wb_ref, b_ref, o_ref):
    k = pl.program_id(1)
    c = c_ref[...]
    s = (c * jax.nn.sigmoid(c)).astype(BF16)
    half = MOD_KB // 2
    part = _dot(s[:, :half], wa_ref[0].astype(BF16)) + _dot(s[:, half:], wb_ref[0].astype(BF16))

    @pl.when(k == 0)
    def _():
        o_ref[0] = part + b_ref[0]

    @pl.when(k > 0)
    def _():
        o_ref[0] += part


def _modulation(cvec, w_mod, b_mod):
    depth, D, C = w_mod.shape
    return pl.pallas_call(
        _mod_kernel,
        grid=(depth, D // MOD_KB),
        in_specs=[pl.BlockSpec((8, MOD_KB), lambda l, k: (0, k)),
                  pl.BlockSpec((1, MOD_KB // 2, C), lambda l, k: (l, 2 * k, 0)),
                  pl.BlockSpec((1, MOD_KB // 2, C), lambda l, k: (l, 2 * k + 1, 0)),
                  pl.BlockSpec((1, 1, C), lambda l, k: (l, 0, 0))],
        out_specs=pl.BlockSpec((1, 8, C), lambda l, k: (l, 0, 0)),
        out_shape=jax.ShapeDtypeStruct((depth, 8, C), F32),
        compiler_params=_cparams(("arbitrary", "arbitrary")),
        name="modulation",
    )(cvec, w_mod, w_mod, b_mod.reshape(depth, 1, C))


NORM_BLK = 256


def _prenorm_kernel(x_ref, cx_ref, g_ref, sh_ref, sc_ref, o_ref, *, nlat):
    i = pl.program_id(1)

    def emit(t):
        o_ref[0] = (_rms(t, g_ref[...]) * (1.0 + sc_ref[0]) + sh_ref[0]).astype(BF16)

    @pl.when(i < nlat)
    def _():
        emit(x_ref[0])

    @pl.when(i >= nlat)
    def _():
        emit(cx_ref[0])


def _prenorm(x, cx, g, shift2, scale2):
    B, N, D = x.shape
    L = cx.shape[1]
    assert N % NORM_BLK == 0 and L % NORM_BLK == 0
    nlat, nctx = N // NORM_BLK, L // NORM_BLK
    mod_spec = pl.BlockSpec((1, 1, D), lambda b, i: (b * 2 + (i >= nlat).astype(jnp.int32), 0, 0))
    return pl.pallas_call(
        functools.partial(_prenorm_kernel, nlat=nlat),
        grid=(B, nlat + nctx),
        in_specs=[pl.BlockSpec((1, NORM_BLK, D), lambda b, i: (b, jnp.minimum(i, nlat - 1), 0)),
                  pl.BlockSpec((1, NORM_BLK, D), lambda b, i: (b, jnp.maximum(i - nlat, 0), 0)),
                  pl.BlockSpec((1, D), lambda b, i: (0, 0)),
                  mod_spec, mod_spec],
        out_specs=pl.BlockSpec((1, NORM_BLK, D), lambda b, i: (b, i, 0)),
        out_shape=jax.ShapeDtypeStruct((B, N + L, D), BF16),
        compiler_params=_cparams(("arbitrary", "arbitrary")),
        name="prenorm",
    )(x, cx, g, shift2, scale2)


def _rope(t, cos, s1, s2):
    w = t.shape[1]
    reps = w // LANES
    tile = lambda a: a if reps == 1 else jnp.concatenate([a] * reps, axis=1)
    return (t * tile(cos) + pltpu.roll(t, w - HEAD_DIM // 2, 1) * tile(s1)
            + pltpu.roll(t, HEAD_DIM // 2, 1) * tile(s2))


def _proj_rope_kernel(h_ref, cos_ref, s1_ref, s2_ref, w_ref, o_ref, odt_ref):
    j = pl.program_id(2)
    acc = _dot(h_ref[0], w_ref[...])
    rot = lambda t: _rope(t, cos_ref[...], s1_ref[...], s2_ref[...])

    @pl.when(j != R_MISC)
    def _():
        o_ref[0] = rot(acc).astype(BF16)

    @pl.when(j == R_MISC)
    def _():
        o_ref[0] = jnp.concatenate([rot(acc[:, :LANES]), acc[:, LANES:]], axis=1).astype(BF16)
        odt_ref[0] = acc[:, 2 * LANES:]


def _row_tile(T):
    return _largest_divisor(T, (1056, 1024, 768, 640, 512, 256, 128))


def _proj_rope(h, rope_tabs, w, l):
    B, T, D = h.shape
    C = w.shape[2]
    tm = _row_tile(T)
    cos, s1, s2 = rope_tabs
    tab_spec = pl.BlockSpec((tm, LANES), lambda b, i, j: (i, 0))
    return pl.pallas_call(
        _proj_rope_kernel,
        grid=(B, T // tm, C // TN),
        in_specs=[pl.BlockSpec((1, tm, D), lambda b, i, j: (b, i, 0)),
                  tab_spec, tab_spec, tab_spec,
                  pl.BlockSpec((None, D, TN), lambda b, i, j: (l, 0, j))],
        out_specs=[pl.BlockSpec((1, tm, TN), lambda b, i, j: (b, i, j)),
                   pl.BlockSpec((1, tm, 2 * LANES), lambda b, i, j: (b, i, 0))],
        out_shape=[jax.ShapeDtypeStruct((B, T, C), BF16),
                   jax.ShapeDtypeStruct((B, T, 2 * LANES), F32)],
        compiler_params=_cparams(("arbitrary", "arbitrary", "arbitrary")),
        name="proj_rope",
    )(h, cos, s1, s2, w)


def _proj_kernel(h_ref, w_ref, o_ref, *, gate):
    acc = _dot(h_ref[0], w_ref[...])
    o_ref[0] = (jax.nn.sigmoid(acc) if gate else acc).astype(BF16)


def _proj(h, w, l, *, gate, tn):
    B, T, D = h.shape
    C = w.shape[2]
    tm = _row_tile(T)
    return pl.pallas_call(
        functools.partial(_proj_kernel, gate=gate),
        grid=(B, T // tm, C // tn),
        in_specs=[pl.BlockSpec((1, tm, D), lambda b, i, j: (b, i, 0)),
                  pl.BlockSpec((None, D, tn), lambda b, i, j: (l, 0, j))],
        out_specs=pl.BlockSpec((1, tm, tn), lambda b, i, j: (b, i, j)),
        out_shape=jax.ShapeDtypeStruct((B, T, C), BF16),
        compiler_params=_cparams(("arbitrary", "arbitrary", "arbitrary")),
        name="proj_gate" if gate else "proj_plain",
    )(h, w)


VT_ROWS = LANES + 16


def _diff_attn_kernel(lam_ref, g_ref, q_ref, k_ref, v_ref, o_ref, s_scr, vt_scr, *, lam_init, tk):
    q = q_ref[0]
    tq = q.shape[0]
    nk = k_ref.shape[1] // tk
    q_m = [jnp.where(_half_mask(m), q, jnp.zeros_like(q)) for m in range(2)]

    @pl.when(pl.program_id(2) == 0)
    def _():
        ones_row = (lax.broadcasted_iota(jnp.int32, (VT_ROWS - LANES, tk), 0) == 0).astype(BF16)
        for c in range(nk):
            v_t = v_ref[0, c * tk:(c + 1) * tk, :].astype(F32).T.astype(BF16)
            vt_scr[c] = jnp.concatenate([v_t, ones_row], axis=0)

    def scores(c):
        kc = k_ref[0, c * tk:(c + 1) * tk, :]
        for m in range(2):
            s_scr[c % 2, m] = _dot_nt(kc, q_m[m])

    carry = [(jnp.full((1, tq), -jnp.inf, F32), jnp.zeros((VT_ROWS, tq), F32)) for _ in range(2)]
    scores(0)
    for c in range(nk):
        if c + 1 < nk:
            scores(c + 1)
        vc = vt_scr[c]
        for m in range(2):
            mx, acc = carry[m]
            m_new = jnp.maximum(mx, jnp.max(s_scr[c % 2, m], 0, keepdims=True))
            p = jnp.exp2((s_scr[c % 2, m] - m_new).astype(BF16))
            acc = jnp.exp2(mx - m_new) * acc + _dot(vc, p)
            carry[m] = (m_new, acc)
    (_, a0), (_, a1) = carry
    lv = lam_ref[...]
    lam = (jnp.exp(jnp.sum(lv[0:1] * lv[1:2], -1, keepdims=True))
           - jnp.exp(jnp.sum(lv[2:3] * lv[3:4], -1, keepdims=True)) + lam_init)
    o = a0[:LANES] / a0[LANES:LANES + 1] - lam * (a1[:LANES] / a1[LANES:LANES + 1])
    o = o * lax.rsqrt(jnp.mean(o * o, 0, keepdims=True) + EPS) * g_ref[...] * (1.0 - lam_init)
    o_ref[0] = o.T.astype(BF16)


def _diff_attn(p_rope, p_plain, lamv, g_col, lam_init, *, q_row0, n_q, k_row0, n_k, tq, tk):
    B = p_rope.shape[0]
    tq = min(tq, n_q)
    assert n_q % tq == 0 and q_row0 % tq == 0 and k_row0 % n_k == 0 and n_k % tk == 0
    qb, kb = q_row0 // tq, k_row0 // n_k
    return pl.pallas_call(
        functools.partial(_diff_attn_kernel, lam_init=lam_init, tk=tk),
        grid=(B, DA_HEADS, n_q // tq),
        in_specs=[pl.BlockSpec((4, HEAD_DIM), lambda b, h, i: (0, 0)),
                  pl.BlockSpec((LANES, 1), lambda b, h, i: (0, 0)),
                  pl.BlockSpec((1, tq, LANES), lambda b, h, i: (b, qb + i, R_AQ * 4 + h)),
                  pl.BlockSpec((1, n_k, LANES), lambda b, h, i: (b, kb, R_AK * 4 + h)),
                  pl.BlockSpec((1, n_k, LANES), lambda b, h, i: (b, kb, P_AV * 4 + h))],
        out_specs=pl.BlockSpec((1, tq, LANES), lambda b, h, i: (b, i, h)),
        out_shape=jax.ShapeDtypeStruct((B, n_q, DA_HEADS * LANES), BF16),
        scratch_shapes=[pltpu.VMEM((2, 2, tk, tq), F32),
                        pltpu.VMEM((n_k // tk, VT_ROWS, tk), BF16)],
        compiler_params=_cparams(("arbitrary",) * 3),
        name="diff_attn",
    )(lamv, g_col, p_rope, p_rope, p_plain)


def _swa_kernel(sink_ref, q_ref, kp_ref, kc_ref, kn_ref, vp_ref, vc_ref, vn_ref, kx_ref, vx_ref, o_ref):
    i = pl.program_id(1)
    nb = pl.num_programs(1)
    blk = SW_BLOCK
    kband = jnp.concatenate([kp_ref[0], kc_ref[0], kn_ref[0], kx_ref[0]], axis=0)
    vband = jnp.concatenate([vp_ref[0], vc_ref[0], vn_ref[0], vx_ref[0]], axis=0)
    nkeys = kband.shape[0]
    row = lax.broadcasted_iota(jnp.int32, (blk, nkeys), 0)
    col = lax.broadcasted_iota(jnp.int32, (blk, nkeys), 1)
    big = 4 * blk
    off_prev = jnp.where(i > 0, 0, big)
    off_next = jnp.where(i < nb - 1, 0, big)
    ok_prev = jnp.where(col >= row + off_prev, 0.0, NEG)
    ok_next = jnp.where(col - 2 * blk <= row - off_next, 0.0, NEG)
    bias = jnp.where(col < blk, ok_prev, jnp.where((col >= 2 * blk) & (col < 3 * blk), ok_next, 0.0))
    bias = jnp.concatenate([bias] * 4, axis=0)
    q = q_ref[0]
    outs = [None] * 4

    def scores(kv):
        qs = jnp.concatenate([jnp.where(_half_mask(kv), q[:, g * LANES:(g + 1) * LANES],
                                        jnp.zeros((blk, LANES), BF16)) for g in range(4)], axis=0)
        return _dot_nt(qs, kband) + bias

    s_all = [scores(kv) for kv in range(SW_KV)]
    for kv in range(SW_KV):
        s = s_all[kv]
        sink = jnp.concatenate([jnp.full((blk, 1), sink_ref[kv * 4 + g] * LOG2E, F32) for g in range(4)], axis=0)
        m = jnp.maximum(jnp.max(s, -1, keepdims=True), sink)
        p = jnp.exp2((s - m).astype(BF16))
        pv, l = _split_pv(_dot(p, _v_with_ones(vband, kv)), kv)
        o = pv / (l + jnp.exp2(sink - m))
        for g in range(4):
            part = o[g * blk:(g + 1) * blk]
            outs[g] = part if outs[g] is None else outs[g] + part
    o_ref[0] = jnp.concatenate(outs, axis=1).astype(BF16)


def _swa(p_rope, sink, N, L):
    B = p_rope.shape[0]
    nb = N // SW_BLOCK
    kcol, vcol = R_MISC * 4, R_MISC * 4 + 1
    kv_spec = lambda colblk, off: pl.BlockSpec(
        (1, SW_BLOCK, LANES), lambda b, i: (b, jnp.clip(i + off, 0, nb - 1), colblk))
    return pl.pallas_call(
        _swa_kernel,
        grid=(B, nb),
        in_specs=[pl.BlockSpec(memory_space=pltpu.SMEM),
                  pl.BlockSpec((1, SW_BLOCK, TN), lambda b, i: (b, i, R_SQ)),
                  kv_spec(kcol, -1), kv_spec(kcol, 0), kv_spec(kcol, 1),
                  kv_spec(vcol, -1), kv_spec(vcol, 0), kv_spec(vcol, 1),
                  pl.BlockSpec((1, L, LANES), lambda b, i: (b, N // L, kcol)),
                  pl.BlockSpec((1, L, LANES), lambda b, i: (b, N // L, vcol))],
        out_specs=pl.BlockSpec((1, SW_BLOCK, TN), lambda b, i: (b, i, 0)),
        out_shape=jax.ShapeDtypeStruct((B, N, TN), BF16),
        compiler_params=_cparams(("arbitrary", "arbitrary")),
        name="window_attn",
    )(sink, *([p_rope] * 9))


NA_QROWS = 2
NA_KBLKS = 5


def _na_kernel(bias_ref, q_ref, k0, k1, k2, k3, k4, v0, v1, v2, v3, v4, kx_ref, vx_ref, o_ref):
    kall = jnp.concatenate([k0[0], k1[0], k2[0], k3[0], k4[0], kx_ref[0]], axis=0)
    vall = jnp.concatenate([v0[0], v1[0], v2[0], v3[0], v4[0], vx_ref[0]], axis=0)
    q = q_ref[0]
    blocks = [slice(pb * LANES, (pb + 1) * LANES) for pb in range(4)]
    scores = [[_dot_nt(jnp.where(_half_mask(hh), q[:, sl], jnp.zeros((q.shape[0], LANES), BF16)), kall[:, sl])
               + bias_ref[0, pb * 2 + hh] for hh in range(2)] for pb, sl in enumerate(blocks)]
    outs = []
    for pb, sl in enumerate(blocks):
        vb = vall[:, sl]
        ob = None
        for hh in range(2):
            s = scores[pb][hh]
            p = jnp.exp2((s - jnp.max(s, -1, keepdims=True)).astype(BF16))
            pv, l = _split_pv(_dot(p, _v_with_ones(vb, hh)), hh)
            o = pv / l
            ob = o if ob is None else ob + o
        outs.append(ob)
    o_ref[0] = jnp.concatenate(outs, axis=1).astype(BF16)


def _col_bias_kernel(r_ref, sel_ref, neg_ref, o_ref):
    o_ref[...] = sum(_dot(part, sel_ref[...]) for part in reversed(_split3(r_ref[...]))) * LOG2E + neg_ref[...]


def _na_col_bias(rpb_all):
    W = GRID_W
    nco = 2 * NA_COLS - 1
    lead = rpb_all.shape[:3]
    rows = int(np.prod(lead))
    c = np.arange(W)
    cb = np.clip(c - NA_COLS // 2, 0, W - NA_COLS)
    dc = c[None, :] - cb[:, None]
    inside = (dc >= 0) & (dc < NA_COLS)
    co = c[None, :] - c[:, None] + NA_COLS - 1
    sel = ((co[None] == np.arange(32)[:, None, None]) & inside[None]).reshape(32, W * W)
    neg = np.where(inside, 0.0, NEG).reshape(1, W * W).astype(np.float32)
    table = jnp.pad(rpb_all.astype(F32).reshape(rows, nco), ((0, (-rows) % 8), (0, 32 - nco)))
    out = pl.pallas_call(
        _col_bias_kernel,
        out_shape=jax.ShapeDtypeStruct((table.shape[0], W * W), F32),
        name="na_col_bias",
    )(table, jnp.asarray(sel, BF16), jnp.asarray(neg))
    return out[:rows].reshape(lead + (W, W))


def _na_bias(t2, R, L):
    W = GRID_W
    depth, H = t2.shape[:2]
    clamp = lambda v, lo, hi: min(max(v, lo), hi)
    nblk = R // NA_QROWS
    slots = NA_KBLKS * NA_QROWS
    per = []
    for i in (0, 1, 2, nblk - 2, nblk - 1):
        base_blk = clamp(i - 2, 0, nblk - NA_KBLKS)
        for t in range(NA_QROWS):
            rq = NA_QROWS * i + t
            base = clamp(rq - NA_ROWS // 2, 0, R - NA_ROWS)
            s_lo = base - NA_QROWS * base_blk
            assert 0 <= s_lo and s_lo + NA_ROWS <= slots
            ro = base - rq + NA_ROWS - 1
            per.append(jnp.pad(t2[:, :, ro:ro + NA_ROWS],
                               ((0, 0), (0, 0), (s_lo, slots - NA_ROWS - s_lo), (0, 0), (0, 0)),
                               constant_values=NEG))
    b = jnp.stack(per, axis=2).reshape(depth, H, 5, NA_QROWS, slots, W, W)
    b = b.transpose(0, 2, 1, 3, 5, 4, 6).reshape(depth, 5, H, NA_QROWS * W, slots * W)
    return jnp.concatenate([b, jnp.zeros(b.shape[:4] + (L,), F32)], axis=-1)


def _na(p_plain, bias, N, L):
    B = p_plain.shape[0]
    rows = NA_QROWS * GRID_W
    nblk = N // rows
    assert nblk >= NA_KBLKS + 2
    cls = lambda i: jnp.where(i < 2, i, jnp.where(i >= nblk - 2, i - (nblk - 5), 2))
    kv_spec = lambda tile, s: pl.BlockSpec(
        (1, rows, TN), lambda b, i: (b, jnp.clip(i - 2, 0, nblk - NA_KBLKS) + s, tile))
    nkeys = NA_KBLKS * rows + L
    return pl.pallas_call(
        _na_kernel,
        grid=(B, nblk),
        in_specs=[pl.BlockSpec((1, NA_HEADS, rows, nkeys), lambda b, i: (cls(i), 0, 0, 0)),
                  pl.BlockSpec((1, rows, TN), lambda b, i: (b, i, P_NQ))]
                 + [kv_spec(P_NK, s) for s in range(NA_KBLKS)]
                 + [kv_spec(P_NV, s) for s in range(NA_KBLKS)]
                 + [pl.BlockSpec((1, L, TN), lambda b, i: (b, N // L, P_NK)),
                    pl.BlockSpec((1, L, TN), lambda b, i: (b, N // L, P_NV))],
        out_specs=pl.BlockSpec((1, rows, TN), lambda b, i: (b, i, 0)),
        out_shape=jax.ShapeDtypeStruct((B, N, TN), BF16),
        compiler_params=_cparams(("arbitrary", "arbitrary")),
        name="neighbourhood_attn",
    )(bias, *([p_plain] * (2 * NA_KBLKS + 3)))


def _ctx_attend_kernel(sink_ref, q_ref, k_ref, v_ref, o_ref, *, gqa):
    q = q_ref[0]
    k = k_ref[0]
    v = v_ref[0]
    outs = []
    for blk in range(4):
        sl = slice(blk * LANES, (blk + 1) * LANES)
        qb = q[:, sl]
        kb, vb = (k, v) if gqa else (k[:, sl], v[:, sl])
        ob = None
        for hh in range(2):
            head = hh * 4 + blk if gqa else blk * 2 + hh
            s = _dot_nt(jnp.where(_half_mask(hh), qb, jnp.zeros_like(qb)), kb)
            sink = sink_ref[head] * LOG2E
            m = jnp.maximum(jnp.max(s, -1, keepdims=True), sink)
            p = jnp.exp2((s - m).astype(BF16))
            pv, l = _split_pv(_dot(p, _v_with_ones(vb, hh)), hh)
            o = pv / (l + jnp.exp2(sink - m))
            ob = o if ob is None else ob + o
        outs.append(ob)
    o_ref[0] = jnp.concatenate(outs, axis=1).astype(BF16)


def _ctx_attend(p, sink, q_tile, k_blk, v_blk, N, L, *, gqa):
    B = p.shape[0]
    kw = LANES if gqa else TN
    rb = N // L
    return pl.pallas_call(
        functools.partial(_ctx_attend_kernel, gqa=gqa),
        grid=(B,),
        in_specs=[pl.BlockSpec(memory_space=pltpu.SMEM),
                  pl.BlockSpec((1, L, TN), lambda b: (b, rb, q_tile)),
                  pl.BlockSpec((1, L, kw), lambda b: (b, rb, k_blk)),
                  pl.BlockSpec((1, L, kw), lambda b: (b, rb, v_blk))],
        out_specs=pl.BlockSpec((1, L, TN), lambda b: (b, 0, 0)),
        out_shape=jax.ShapeDtypeStruct((B, L, TN), BF16),
        compiler_params=_cparams(("arbitrary",)),
        name="ctx_attend_gqa" if gqa else "ctx_attend",
    )(sink, p, p, p)


CONV_BLK = 256
HALO = 16


def _conv_kernel(prev_ref, cur_ref, next_ref, w_ref, b_ref, o_ref, *, lat_blocks):
    i = pl.program_id(1)
    last = pl.num_programs(1) - 1
    has_prev = (i != 0) & (i != lat_blocks)
    has_next = (i != lat_blocks - 1) & (i != last)
    prev = prev_ref[0].astype(F32) * has_prev.astype(F32)
    nxt = next_ref[0].astype(F32) * has_next.astype(F32)
    ext = jnp.concatenate([prev, cur_ref[0].astype(F32), nxt], axis=0)
    n = ext.shape[0]
    w = w_ref[...]
    acc = None
    for k in range(SSM_CONV):
        shift = (SSM_CONV // 2 - k) % n
        t = (ext if shift == 0 else pltpu.roll(ext, shift, 0))[HALO:HALO + CONV_BLK] * w[k:k + 1]
        acc = t if acc is None else acc + t
    acc = acc + b_ref[...]
    o_ref[0] = (acc * jax.nn.sigmoid(acc)).astype(BF16)


def _conv_silu(p_plain, conv_w, conv_b, N):
    B, T, _ = p_plain.shape
    C = SSM_XBC
    assert N % CONV_BLK == 0 and T % CONV_BLK == 0
    per = CONV_BLK // HALO
    nh = T // HALO
    w = jnp.zeros((8, C), F32).at[:SSM_CONV].set(conv_w.astype(F32))
    return pl.pallas_call(
        functools.partial(_conv_kernel, lat_blocks=N // CONV_BLK),
        grid=(B, T // CONV_BLK),
        in_specs=[pl.BlockSpec((1, HALO, C), lambda b, i: (b, jnp.maximum(i * per - 1, 0), 0)),
                  pl.BlockSpec((1, CONV_BLK, C), lambda b, i: (b, i, 0)),
                  pl.BlockSpec((1, HALO, C), lambda b, i: (b, jnp.minimum((i + 1) * per, nh - 1), 0)),
                  pl.BlockSpec((8, C), lambda b, i: (0, 0)),
                  pl.BlockSpec((1, C), lambda b, i: (0, 0))],
        out_specs=pl.BlockSpec((1, CONV_BLK, C), lambda b, i: (b, i, 0)),
        out_shape=jax.ShapeDtypeStruct((B, T, C), BF16),
        compiler_params=_cparams(("arbitrary", "arbitrary")),
        name="ssm_conv",
    )(p_plain, p_plain, p_plain, w, conv_b.reshape(1, C).astype(F32))


def _split3(t):
    hi = t.astype(BF16)
    r = t - hi.astype(F32)
    mid = r.astype(BF16)
    return hi, mid, (r - mid.astype(F32)).astype(BF16)


def _ssd_chain(u, dtr, dtb, alog, tri, expm, state, *, fwd):
    last = CHUNK - 1 if fwd else 0
    xs = u[:, :SSM_INNER].astype(F32)
    bm = u[:, SSM_INNER:SSM_INNER + SSM_GROUPS * SSM_STATE]
    cm = u[:, SSM_INNER + SSM_GROUPS * SSM_STATE:]
    dt = jax.nn.softplus(dtr + dtb)
    a = dt * (-jnp.exp(alog))
    acum = sum(_dot(tri, part) for part in reversed(_split3(a)))
    eac = jnp.exp(acum)
    dst = jnp.exp(acum[last:last + 1] - acum)
    e3 = sum(_dot(part, expm) for part in reversed(_split3(jnp.concatenate([dt, eac, dst], axis=0))))
    dt_e, eac_e, dst_e = e3[:CHUNK], e3[CHUNK:2 * CHUNK], e3[2 * CHUNK:]
    xg = xs * dt_e
    xg_b = xg.astype(BF16)
    xgd_b = (xg * dst_e).astype(BF16)
    acum_t = acum.T
    row = lax.broadcasted_iota(jnp.int32, (CHUNK, CHUNK), 0)
    col = lax.broadcasted_iota(jnp.int32, (CHUNK, CHUNK), 1)
    keep = (row >= col) if fwd else (row <= col)
    state_b = state.astype(BF16)
    gw = SSM_INNER // SSM_GROUPS
    ys, st_new = [], []
    for g in range(SSM_GROUPS):
        bmg = bm[:, g * SSM_STATE:(g + 1) * SSM_STATE]
        cmg = cm[:, g * SSM_STATE:(g + 1) * SSM_STATE]
        cb = _dot_nt(cmg, bmg)
        yoff = _dot(cmg, state_b[:, g * gw:(g + 1) * gw])
        st_new.append(_dot_tn(bmg, xgd_b[:, g * gw:(g + 1) * gw]))
        for pb in range(gw // LANES):
            blk = g * (gw // LANES) + pb
            sl = slice(blk * LANES, (blk + 1) * LANES)
            xgb = xg_b[:, sl]
            yd = None
            for hh in range(2):
                h = blk * 2 + hh
                seg = acum[:, h:h + 1] - acum_t[h:h + 1, :]
                ld = jnp.exp(jnp.where(keep, seg, -jnp.inf))
                t = _dot((cb * ld).astype(BF16), jnp.where(_half_mask(hh), xgb, jnp.zeros_like(xgb)))
                yd = t if yd is None else yd + t
            ys.append(yd + yoff[:, pb * LANES:(pb + 1) * LANES] * eac_e[:, sl])
    y = jnp.concatenate(ys, axis=1)
    return y, state * eac_e[last:last + 1] + jnp.concatenate(st_new, axis=1)


def _ssd_kernel(uf_ref, ub_ref, dtf_ref, dtb_ref, bias_ref, alog_ref, tri_ref, exp_ref, yf_ref, yb_ref, h_scr):
    @pl.when(pl.program_id(0) == 0)
    def _():
        h_scr[...] = jnp.zeros(h_scr.shape, F32)

    for b in range(uf_ref.shape[0]):
        for d, (u_ref, dt_ref, y_ref) in enumerate(((uf_ref, dtf_ref, yf_ref), (ub_ref, dtb_ref, yb_ref))):
            y, st = _ssd_chain(u_ref[b], dt_ref[b], bias_ref[d], alog_ref[d], tri_ref[d], exp_ref[...],
                               h_scr[b, d], fwd=(d == 0))
            y_ref[b] = y
            h_scr[b, d] = st


def _ssd(u, dtr, dt_bias, a_log, N):
    B, T, _ = u.shape
    nc, nlat = T // CHUNK, N // CHUNK
    ncx = nc - nlat
    chunk_f = lambda c: jnp.where(c < ncx, nlat + c, c - ncx)
    chunk_b = lambda c: jnp.where(c < ncx, nlat + ncx - 1 - c, nlat - 1 - (c - ncx))
    pad = lambda t: jnp.zeros((2, 1, LANES), F32).at[:, 0, :SSM_HEADS].set(t.astype(F32))
    li = np.arange(CHUNK)
    tri = np.stack([li[:, None] >= li[None, :], li[:, None] <= li[None, :]]).astype(np.float32)
    expm = np.zeros((LANES, SSM_INNER), np.float32)
    for h in range(SSM_HEADS):
        expm[h, h * 64:(h + 1) * 64] = 1.0
    full = lambda shape: pl.BlockSpec(shape, lambda c: (0,) * len(shape))
    y_shape = jax.ShapeDtypeStruct((B, T, SSM_INNER), F32)
    return pl.pallas_call(
        _ssd_kernel,
        grid=(nc,),
        in_specs=[pl.BlockSpec((B, CHUNK, SSM_XBC), lambda c: (0, chunk_f(c), 0)),
                  pl.BlockSpec((B, CHUNK, SSM_XBC), lambda c: (0, chunk_b(c), 0)),
                  pl.BlockSpec((B, CHUNK, LANES), lambda c: (0, chunk_f(c), 0)),
                  pl.BlockSpec((B, CHUNK, LANES), lambda c: (0, chunk_b(c), 1)),
                  full((2, 1, LANES)), full((2, 1, LANES)), full((2, CHUNK, CHUNK)), full((LANES, SSM_INNER))],
        out_specs=[pl.BlockSpec((B, CHUNK, SSM_INNER), lambda c: (0, chunk_f(c), 0)),
                   pl.BlockSpec((B, CHUNK, SSM_INNER), lambda c: (0, chunk_b(c), 0))],
        out_shape=[y_shape, y_shape],
        scratch_shapes=[pltpu.VMEM((B, 2, SSM_STATE, SSM_INNER), F32)],
        compiler_params=_cparams(("arbitrary",)),
        name="ssd_scan",
    )(u, u, dtr, dtr, pad(dt_bias), pad(a_log), jnp.asarray(tri, BF16), jnp.asarray(expm, BF16))


def _ssm_finish_kernel(yf_ref, yb_ref, u_ref, z_ref, dsk_ref, g_ref, o_ref):
    y = yf_ref[0] + yb_ref[0] + dsk_ref[...] * u_ref[0].astype(F32)
    z = z_ref[0].astype(F32)
    o_ref[0] = _rms(y * (z * jax.nn.sigmoid(z)), g_ref[...]).astype(BF16)


def _ssm_finish(y_f, y_b, u, p_plain, d_skip, g_ssm, *, tm=256):
    B, T, W = y_f.shape
    dsk = jnp.repeat(d_skip.astype(F32), W // SSM_HEADS).reshape(1, W)
    row_spec = pl.BlockSpec((1, tm, W), lambda b, i: (b, i, 0))
    return pl.pallas_call(
        _ssm_finish_kernel,
        grid=(B, T // tm),
        in_specs=[row_spec, row_spec, row_spec,
                  pl.BlockSpec((1, tm, W), lambda b, i: (b, i, P_BZ)),
                  pl.BlockSpec((1, W), lambda b, i: (0, 0)),
                  pl.BlockSpec((1, W), lambda b, i: (0, 0))],
        out_specs=row_spec,
        out_shape=jax.ShapeDtypeStruct((B, T, W), BF16),
        compiler_params=_cparams(("arbitrary", "arbitrary")),
        name="ssm_finish",
    )(y_f, y_b, u, p_plain, dsk, g_ssm.reshape(1, W).astype(F32))


def _merge_kernel(y0, y1, y2, y3, s0, s1, s2, s3, wb_ref, wo_ref, x_ref, gate_ref, g_ref, o_ref):
    m = None
    for k, (y, s) in enumerate(((y0, s0), (y1, s1), (y2, s2), (y3, s3))):
        t = _dot(y[0], wb_ref[k]) * s[0].astype(F32)
        m = t if m is None else m + t
    o = _dot(m.astype(BF16), wo_ref[...])
    o_ref[0] = x_ref[0] + gate_ref[0] * _rms(o, g_ref[...])


MERGE_TM = 256


def _merge(ya, yb_all, ys, yn, p_gate, w_branch, w_out, l, x, gate, g_post, *, row_off):
    B, Tx, D = x.shape
    tm = MERGE_TM
    assert Tx % tm == 0 and row_off % tm == 0
    off = row_off // tm
    y_spec = pl.BlockSpec((1, tm, BRANCH_W), lambda b, i: (b, i, 0))
    yb_spec = pl.BlockSpec((1, tm, BRANCH_W), lambda b, i: (b, off + i, 0))
    s_spec = lambda k: pl.BlockSpec((1, tm, D), lambda b, i: (b, off + i, k))
    resident = lambda shape: pl.BlockSpec((None,) + shape, lambda b, i: (l,) + (0,) * len(shape),
                                          pipeline_mode=pl.Buffered(1))
    return pl.pallas_call(
        _merge_kernel,
        grid=(B, Tx // tm),
        in_specs=[y_spec, yb_spec, y_spec, y_spec] + [s_spec(k) for k in range(N_BRANCH)]
                 + [resident((N_BRANCH, BRANCH_W, D)), resident((D, D)),
                    pl.BlockSpec((1, tm, D), lambda b, i: (b, i, 0)),
                    pl.BlockSpec((1, 1, D), lambda b, i: (b, 0, 0)),
                    pl.BlockSpec((1, D), lambda b, i: (0, 0))],
        out_specs=pl.BlockSpec((1, tm, D), lambda b, i: (b, i, 0)),
        out_shape=jax.ShapeDtypeStruct((B, Tx, D), F32),
        compiler_params=_cparams(("arbitrary", "arbitrary")),
        name="merge_out_proj",
    )(ya, yb_all, ys, yn, p_gate, p_gate, p_gate, p_gate, w_branch, w_out, x, gate, g_post)


def _ffn_kernel(x_ref, g1_ref, sh_ref, sc_ref, gate_ref, g2_ref, w1_ref, w2_ref, o_ref, h_scr, acc):
    f = pl.program_id(2)

    @pl.when(f == 0)
    def _():
        h = _rms(x_ref[0], g1_ref[...]) * (1.0 + sc_ref[0]) + sh_ref[0]
        h_scr[...] = h.astype(BF16)
        acc[...] = jnp.zeros(acc.shape, F32)

    a = jnp.square(jnp.maximum(_dot(h_scr[...], w1_ref[...]), 0.0)).astype(BF16)
    acc[...] += _dot(a, w2_ref[...])

    @pl.when(f == pl.num_programs(2) - 1)
    def _():
        o_ref[0] = x_ref[0] + gate_ref[0] * _rms(acc[...], g2_ref[...])


def _ffn(x, g_pre, shift, scale, gate, g_post, w1, w2, l, *, tm, tf=1024):
    B, T, D = x.shape
    F = w1.shape[2]
    tm = min(tm, T)
    mod_spec = pl.BlockSpec((1, 1, D), lambda b, i, f: (b, 0, 0))
    vec_spec = pl.BlockSpec((1, D), lambda b, i, f: (0, 0))
    return pl.pallas_call(
        _ffn_kernel,
        grid=(B, T // tm, F // tf),
        in_specs=[pl.BlockSpec((1, tm, D), lambda b, i, f: (b, i, 0)),
                  vec_spec, mod_spec, mod_spec, mod_spec, vec_spec,
                  pl.BlockSpec((None, D, tf), lambda b, i, f: (l, 0, f)),
                  pl.BlockSpec((None, tf, D), lambda b, i, f: (l, f, 0))],
        out_specs=pl.BlockSpec((1, tm, D), lambda b, i, f: (b, i, 0)),
        out_shape=jax.ShapeDtypeStruct((B, T, D), F32),
        scratch_shapes=[pltpu.VMEM((tm, D), BF16), pltpu.VMEM((tm, D), F32)],
        compiler_params=_cparams(("arbitrary", "arbitrary", "arbitrary")),
        name="mlp",
    )(x, g_pre, shift, scale, gate, g_post, w1, w2)


def _layout_w_in(w):
    depth, D, _ = w.shape
    bounds, o = {}, 0
    for name, width in (("aq", 512), ("ak", 512), ("av", 512), ("bz", 512), ("bx", SSM_XBC), ("bdt", 16),
                        ("sq", 512), ("sk", 128), ("sv", 128), ("nq", 512), ("nk", 512), ("nv", 512),
                        ("gate", N_BRANCH * D)):
        bounds[name] = (o, o + width)
        o += width
    assert o == w.shape[2]
    wb = w.astype(BF16)
    part = lambda name: wb[:, :, bounds[name][0]:bounds[name][1]]
    qpart = lambda name: w[:, :, bounds[name][0]:bounds[name][1]] * (QSCALE * LOG2E)
    sq = qpart("sq").reshape(depth, D, SW_KV, 4, HEAD_DIM).transpose(0, 1, 3, 2, 4).reshape(depth, D, 512)
    zpad = jnp.zeros((depth, D, LANES - SSM_HEADS), BF16)
    bdt = part("bdt")
    misc = jnp.concatenate([part("sk"), part("sv"), bdt[:, :, :SSM_HEADS], zpad, bdt[:, :, SSM_HEADS:], zpad], axis=2)
    w_rope = jnp.concatenate([qpart("aq").astype(BF16), part("ak"), sq.astype(BF16), misc], axis=2)
    w_plain = jnp.concatenate([part("bx"), qpart("nq").astype(BF16), part("nk"), part("nv"), part("av"),
                               part("bz")], axis=2)
    return w_rope, w_plain, part("gate")


def _rope_tables(n, n_ctx):
    t = np.arange(n)
    nf = HEAD_DIM // 4
    inv = ROPE_BASE ** (-np.arange(nf, dtype=np.float32) / nf)
    row = (t // GRID_W).astype(np.float32)
    col = (t % GRID_W).astype(np.float32)
    ang = jnp.asarray(np.concatenate([row[:, None] * inv, col[:, None] * inv], -1).astype(np.float32))
    cos, sin = jnp.cos(ang), jnp.sin(ang)
    cos = jnp.concatenate([cos, jnp.ones((n_ctx, cos.shape[1]), F32)], axis=0)
    sin = jnp.concatenate([sin, jnp.zeros((n_ctx, sin.shape[1]), F32)], axis=0)
    z = jnp.zeros_like(sin)
    rep = LANES // HEAD_DIM
    cos_t = jnp.tile(jnp.concatenate([cos, cos], -1), (1, rep))
    s1 = jnp.tile(jnp.concatenate([-sin, z], -1), (1, rep))
    s2 = jnp.tile(jnp.concatenate([z, sin], -1), (1, rep))
    return cos_t, s1, s2


def kernel(x, c, ctx, c_ctx, w_mod, b_mod, g_pre_mix, g_post_mix, g_pre_mlp, g_post_mlp, w_in, lam_q1, lam_k1, lam_q2, lam_k2, g_subln, conv_w, conv_b, dt_bias, a_log, d_skip, g_ssm, sink, rpb, w_branch, w_out, w_ff1, w_ff2):
    B, N, D = x.shape
    L = ctx.shape[1]
    T = N + L
    depth = w_mod.shape[0]
    assert B + 1 <= 8 and N % GRID_W == 0 and N % L == 0
    R = N // GRID_W

    cvec = jnp.zeros((8, D), F32).at[:B].set(c).at[B].set(c_ctx)
    mods = _modulation(cvec, w_mod, b_mod)
    rope_tabs = _rope_tables(N, L)
    na_bias = _na_bias(_na_col_bias(rpb), R, L)
    no_sink = jnp.full((NA_HEADS,), NEG, F32)
    row = lambda v: v.reshape(1, -1).astype(F32)
    tk_lat = _largest_divisor(T, (768, 512, 256, 128))
    w_rope, w_plain, w_gate = _layout_w_in(w_in)
    wb = w_branch.at[:, 2].set(w_branch[:, 2].reshape(depth, SW_KV, 4, HEAD_DIM, D).transpose(0, 2, 1, 3, 4)
                               .reshape(depth, BRANCH_W, D)).astype(BF16)
    wo, w1, w2 = w_out.astype(BF16), w_ff1.astype(BF16), w_ff2.astype(BF16)

    cx = ctx
    for l in range(depth):
        need_ctx = l < depth - 1
        lam_init = 0.8 - 0.6 * math.exp(-0.3 * l)
        mod = mods[l, :B].reshape(B, 6, 1, D)
        modc = jnp.broadcast_to(mods[l, B].reshape(1, 6, 1, D), (B, 6, 1, D))
        both = lambda k: jnp.stack([mod[:, k], modc[:, k]], axis=1).reshape(B * 2, 1, D)

        h = _prenorm(x, cx, row(g_pre_mix[l]), both(0), both(1))
        p_rope, dtr = _proj_rope(h, rope_tabs, w_rope, l)
        p_plain = _proj(h, w_plain, l, gate=False, tn=TN)
        p_gate = _proj(h, w_gate, l, gate=True, tn=2048)

        lamv = jnp.stack([lam_q1[l], lam_k1[l], lam_q2[l], lam_k2[l]]).astype(F32)
        g_col = g_subln[l].reshape(LANES, 1).astype(F32)
        ya = _diff_attn(p_rope, p_plain, lamv, g_col, lam_init, q_row0=0, n_q=N, k_row0=0, n_k=T, tq=512, tk=tk_lat)

        u = _conv_silu(p_plain, conv_w[l], conv_b[l], N)
        y_f, y_b = _ssd(u, dtr, dt_bias[l], a_log[l], N)
        yb_all = _ssm_finish(y_f, y_b, u, p_plain, d_skip[l], g_ssm[l])

        ys = _swa(p_rope, sink[l].astype(F32), N, L)
        yn = _na(p_plain, na_bias[l], N, L)

        x_new = _merge(ya, yb_all, ys, yn, p_gate, wb, wo, l, x, mod[:, 2], row(g_post_mix[l]), row_off=0)
        x_new = _ffn(x_new, row(g_pre_mlp[l]), mod[:, 3], mod[:, 4], mod[:, 5], row(g_post_mlp[l]), w1, w2, l,
                     tm=512)
        if need_ctx:
            ya_c = _diff_attn(p_rope, p_plain, lamv, g_col, lam_init, q_row0=N, n_q=L, k_row0=N, n_k=L, tq=L, tk=L)
            ys_c = _ctx_attend(p_rope, sink[l].astype(F32), R_SQ, R_MISC * 4, R_MISC * 4 + 1, N, L, gqa=True)
            yn_c = _ctx_attend(p_plain, no_sink, P_NQ, P_NK, P_NV, N, L, gqa=False)
            cx = _merge(ya_c, yb_all, ys_c, yn_c, p_gate, wb, wo, l, cx, modc[:, 2], row(g_post_mix[l]), row_off=N)
            cx = _ffn(cx, row(g_pre_mlp[l]), modc[:, 3], modc[:, 4], modc[:, 5], row(g_post_mlp[l]), w1, w2, l,
                      tm=L)
        x = x_new
    return x
```

```python
import functools
import math

import numpy as np
import jax
import jax.numpy as jnp
from jax import lax
from jax.experimental import pallas as pl
from jax.experimental.pallas import tpu as pltpu

F32 = jnp.float32
BF16 = jnp.bfloat16

LANES = 128
VMEM_LIMIT = 56 * 1024 * 1024
NEG = -1e30
EPS = 1e-6
GRID_W = 64
ROPE_BASE = 10000.0
HEAD_DIM = 64
QSCALE = HEAD_DIM ** -0.5
LOG2E = math.log2(math.e)
DA_HEADS = 4
SSM_HEADS = 8
SSM_INNER = 512
SSM_GROUPS = 2
SSM_STATE = 128
SSM_XBC = SSM_INNER + 2 * SSM_GROUPS * SSM_STATE
SSM_CONV = 5
CHUNK = 128
SW_KV = 2
SW_BLOCK = 128
NA_HEADS = 8
NA_ROWS = 8
NA_COLS = 16
N_BRANCH = 4
BRANCH_W = 512

TN = 512
R_AQ, R_AK, R_SQ, R_MISC = 0, 1, 2, 3
P_NQ, P_NK, P_NV, P_AV, P_BZ = 2, 3, 4, 5, 6


def _cparams(sem):
    return pltpu.CompilerParams(dimension_semantics=sem, vmem_limit_bytes=VMEM_LIMIT)


def _rms(t, g):
    return t * lax.rsqrt(jnp.mean(t * t, -1, keepdims=True) + EPS) * g


def _dot(a, b):
    return jnp.dot(a, b, preferred_element_type=F32)


def _dot_nt(a, b):
    return lax.dot_general(a, b, (((1,), (1,)), ((), ())), preferred_element_type=F32)


def _dot_tn(a, b):
    return lax.dot_general(a, b, (((0,), (0,)), ((), ())), preferred_element_type=F32)


def _half_mask(hh):
    lane = lax.broadcasted_iota(jnp.int32, (1, LANES), 1)
    return (lane // HEAD_DIM) == hh


def _v_with_ones(v, hh):
    lane = lax.broadcasted_iota(jnp.int32, (1, LANES), 1)
    ones_lane = jnp.where(lane == (1 - hh) * HEAD_DIM, 1.0, 0.0).astype(v.dtype)
    return jnp.where(_half_mask(hh), v, jnp.zeros_like(v)) + ones_lane


def _split_pv(o, hh):
    spare = (1 - hh) * HEAD_DIM
    return jnp.where(_half_mask(hh), o, 0.0), o[:, spare:spare + 1]


def _largest_divisor(n, cands):
    for c in cands:
        if n % c == 0:
            return c
    raise ValueError(f"no tile in {cands} divides {n}")


MOD_KB = 256


def _mod_kernel(c_ref, wa_ref, wb_ref, b_ref, o_ref):
    k = pl.program_id(1)
    c = c_ref[...]
    s = (c * jax.nn.sigmoid(c)).astype(BF16)
    half = MOD_KB // 2
    part = _dot(s[:, :half], wa_ref[0].astype(BF16)) + _dot(s[:, half:], wb_ref[0].astype(BF16))

    @pl.when(k == 0)
    def _():
        o_ref[0] = part + b_ref[0]

    @pl.when(k > 0)
    def _():
        o_ref[0] += part


def _modulation(cvec, w_mod, b_mod):
    depth, D, C = w_mod.shape
    return pl.pallas_call(
        _mod_kernel,
        grid=(depth, D // MOD_KB),
        in_specs=[pl.BlockSpec((8, MOD_KB), lambda l, k: (0, k)),
                  pl.BlockSpec((1, MOD_KB // 2, C), lambda l, k: (l, 2 * k, 0)),
                  pl.BlockSpec((1, MOD_KB // 2, C), lambda l, k: (l, 2 * k + 1, 0)),
                  pl.BlockSpec((1, 1, C), lambda l, k: (l, 0, 0))],
        out_specs=pl.BlockSpec((1, 8, C), lambda l, k: (l, 0, 0)),
        out_shape=jax.ShapeDtypeStruct((depth, 8, C), F32),
        compiler_params=_cparams(("arbitrary", "arbitrary")),
        name="modulation",
    )(cvec, w_mod, w_mod, b_mod.reshape(depth, 1, C))


NORM_BLK = 256


def _prenorm_kernel(x_ref, cx_ref, g_ref, sh_ref, sc_ref, o_ref, *, nlat):
    i = pl.program_id(1)

    def emit(t):
        o_ref[0] = (_rms(t, g_ref[...]) * (1.0 + sc_ref[0]) + sh_ref[0]).astype(BF16)

    @pl.when(i < nlat)
    def _():
        emit(x_ref[0])

    @pl.when(i >= nlat)
    def _():
        emit(cx_ref[0])


def _prenorm(x, cx, g, shift2, scale2):
    B, N, D = x.shape
    L = cx.shape[1]
    assert N % NORM_BLK == 0 and L % NORM_BLK == 0
    nlat, nctx = N // NORM_BLK, L // NORM_BLK
    mod_spec = pl.BlockSpec((1, 1, D), lambda b, i: (b * 2 + (i >= nlat).astype(jnp.int32), 0, 0))
    return pl.pallas_call(
        functools.partial(_prenorm_kernel, nlat=nlat),
        grid=(B, nlat + nctx),
        in_specs=[pl.BlockSpec((1, NORM_BLK, D), lambda b, i: (b, jnp.minimum(i, nlat - 1), 0)),
                  pl.BlockSpec((1, NORM_BLK, D), lambda b, i: (b, jnp.maximum(i - nlat, 0), 0)),
                  pl.BlockSpec((1, D), lambda b, i: (0, 0)),
                  mod_spec, mod_spec],
        out_specs=pl.BlockSpec((1, NORM_BLK, D), lambda b, i: (b, i, 0)),
        out_shape=jax.ShapeDtypeStruct((B, N + L, D), BF16),
        compiler_params=_cparams(("arbitrary", "arbitrary")),
        name="prenorm",
    )(x, cx, g, shift2, scale2)


def _rope(t, cos, s1, s2):
    w = t.shape[1]
    reps = w // LANES
    tile = lambda a: a if reps == 1 else jnp.concatenate([a] * reps, axis=1)
    return (t * tile(cos) + pltpu.roll(t, w - HEAD_DIM // 2, 1) * tile(s1)
            + pltpu.roll(t, HEAD_DIM // 2, 1) * tile(s2))


def _proj_rope_kernel(h_ref, cos_ref, s1_ref, s2_ref, w_ref, o_ref, odt_ref):
    j = pl.program_id(2)
    acc = _dot(h_ref[0], w_ref[...])
    rot = lambda t: _rope(t, cos_ref[...], s1_ref[...], s2_ref[...])

    @pl.when(j != R_MISC)
    def _():
        o_ref[0] = rot(acc).astype(BF16)

    @pl.when(j == R_MISC)
    def _():
        o_ref[0] = jnp.concatenate([rot(acc[:, :LANES]), acc[:, LANES:]], axis=1).astype(BF16)
        odt_ref[0] = acc[:, 2 * LANES:]


def _row_tile(T):
    return _largest_divisor(T, (1056, 1024, 768, 640, 512, 256, 128))


def _proj_rope(h, rope_tabs, w, l):
    B, T, D = h.shape
    C = w.shape[2]
    tm = _row_tile(T)
    cos, s1, s2 = rope_tabs
    tab_spec = pl.BlockSpec((tm, LANES), lambda b, i, j: (i, 0))
    return pl.pallas_call(
        _proj_rope_kernel,
        grid=(B, T // tm, C // TN),
        in_specs=[pl.BlockSpec((1, tm, D), lambda b, i, j: (b, i, 0)),
                  tab_spec, tab_spec, tab_spec,
                  pl.BlockSpec((None, D, TN), lambda b, i, j: (l, 0, j))],
        out_specs=[pl.BlockSpec((1, tm, TN), lambda b, i, j: (b, i, j)),
                   pl.BlockSpec((1, tm, 2 * LANES), lambda b, i, j: (b, i, 0))],
        out_shape=[jax.ShapeDtypeStruct((B, T, C), BF16),
                   jax.ShapeDtypeStruct((B, T, 2 * LANES), F32)],
        compiler_params=_cparams(("arbitrary", "arbitrary", "arbitrary")),
        name="proj_rope",
    )(h, cos, s1, s2, w)


def _proj_kernel(h_ref, w_ref, o_ref, *, gate):
    acc = _dot(h_ref[0], w_ref[...])
    o_ref[0] = (jax.nn.sigmoid(acc) if gate else acc).astype(BF16)


def _proj(h, w, l, *, gate, tn):
    B, T, D = h.shape
    C = w.shape[2]
    tm = _row_tile(T)
    return pl.pallas_call(
        functools.partial(_proj_kernel, gate=gate),
        grid=(B, T // tm, C // tn),
        in_specs=[pl.BlockSpec((1, tm, D), lambda b, i, j: (b, i, 0)),
                  pl.BlockSpec((None, D, tn), lambda b, i, j: (l, 0, j))],
        out_specs=pl.BlockSpec((1, tm, tn), lambda b, i, j: (b, i, j)),
        out_shape=jax.ShapeDtypeStruct((B, T, C), BF16),
        compiler_params=_cparams(("arbitrary", "arbitrary", "arbitrary")),
        name="proj_gate" if gate else "proj_plain",
    )(h, w)


VT_ROWS = LANES + 16


def _diff_attn_kernel(lam_ref, g_ref, q_ref, k_ref, v_ref, o_ref, s_scr, vt_scr, *, lam_init, tk):
    q = q_ref[0]
    tq = q.shape[0]
    nk = k_ref.shape[1] // tk
    q_m = [jnp.where(_half_mask(m), q, jnp.zeros_like(q)) for m in range(2)]

    @pl.when(pl.program_id(2) == 0)
    def _():
        ones_row = (lax.broadcasted_iota(jnp.int32, (VT_ROWS - LANES, tk), 0) == 0).astype(BF16)
        for c in range(nk):
            v_t = v_ref[0, c * tk:(c + 1) * tk, :].astype(F32).T.astype(BF16)
            vt_scr[c] = jnp.concatenate([v_t, ones_row], axis=0)

    def scores(c):
        kc = k_ref[0, c * tk:(c + 1) * tk, :]
        for m in range(2):
            s_scr[c % 2, m] = _dot_nt(kc, q_m[m])

    carry = [(jnp.full((1, tq), -jnp.inf, F32), jnp.zeros((VT_ROWS, tq), F32)) for _ in range(2)]
    scores(0)
    for c in range(nk):
        if c + 1 < nk:
            scores(c + 1)
        vc = vt_scr[c]
        for m in range(2):
            mx, acc = carry[m]
            m_new = jnp.maximum(mx, jnp.max(s_scr[c % 2, m], 0, keepdims=True))
            p = jnp.exp2((s_scr[c % 2, m] - m_new).astype(BF16))
            acc = jnp.exp2(mx - m_new) * acc + _dot(vc, p)
            carry[m] = (m_new, acc)
    (_, a0), (_, a1) = carry
    lv = lam_ref[...]
    lam = (jnp.exp(jnp.sum(lv[0:1] * lv[1:2], -1, keepdims=True))
           - jnp.exp(jnp.sum(lv[2:3] * lv[3:4], -1, keepdims=True)) + lam_init)
    o = a0[:LANES] / a0[LANES:LANES + 1] - lam * (a1[:LANES] / a1[LANES:LANES + 1])
    o = o * lax.rsqrt(jnp.mean(o * o, 0, keepdims=True) + EPS) * g_ref[...] * (1.0 - lam_init)
    o_ref[0] = o.T.astype(BF16)


def _diff_attn(p_rope, p_plain, lamv, g_col, lam_init, *, q_row0, n_q, k_row0, n_k, tq, tk):
    B = p_rope.shape[0]
    tq = min(tq, n_q)
    assert n_q % tq == 0 and q_row0 % tq == 0 and k_row0 % n_k == 0 and n_k % tk == 0
    qb, kb = q_row0 // tq, k_row0 // n_k
    return pl.pallas_call(
        functools.partial(_diff_attn_kernel, lam_init=lam_init, tk=tk),
        grid=(B, DA_HEADS, n_q // tq),
        in_specs=[pl.BlockSpec((4, HEAD_DIM), lambda b, h, i: (0, 0)),
                  pl.BlockSpec((LANES, 1), lambda b, h, i: (0, 0)),
                  pl.BlockSpec((1, tq, LANES), lambda b, h, i: (b, qb + i, R_AQ * 4 + h)),
                  pl.BlockSpec((1, n_k, LANES), lambda b, h, i: (b, kb, R_AK * 4 + h)),
                  pl.BlockSpec((1, n_k, LANES), lambda b, h, i: (b, kb, P_AV * 4 + h))],
        out_specs=pl.BlockSpec((1, tq, LANES), lambda b, h, i: (b, i, h)),
        out_shape=jax.ShapeDtypeStruct((B, n_q, DA_HEADS * LANES), BF16),
        scratch_shapes=[pltpu.VMEM((2, 2, tk, tq), F32),
                        pltpu.VMEM((n_k // tk, VT_ROWS, tk), BF16)],
        compiler_params=_cparams(("arbitrary",) * 3),
        name="diff_attn",
    )(lamv, g_col, p_rope, p_rope, p_plain)


def _swa_kernel(sink_ref, q_ref, kp_ref, kc_ref, kn_ref, vp_ref, vc_ref, vn_ref, kx_ref, vx_ref, o_ref):
    i = pl.program_id(1)
    nb = pl.num_programs(1)
    blk = SW_BLOCK
    kband = jnp.concatenate([kp_ref[0], kc_ref[0], kn_ref[0], kx_ref[0]], axis=0)
    vband = jnp.concatenate([vp_ref[0], vc_ref[0], vn_ref[0], vx_ref[0]], axis=0)
    nkeys = kband.shape[0]
    row = lax.broadcasted_iota(jnp.int32, (blk, nkeys), 0)
    col = lax.broadcasted_iota(jnp.int32, (blk, nkeys), 1)
    big = 4 * blk
    off_prev = jnp.where(i > 0, 0, big)
    off_next = jnp.where(i < nb - 1, 0, big)
    ok_prev = jnp.where(col >= row + off_prev, 0.0, NEG)
    ok_next = jnp.where(col - 2 * blk <= row - off_next, 0.0, NEG)
    bias = jnp.where(col < blk, ok_prev, jnp.where((col >= 2 * blk) & (col < 3 * blk), ok_next, 0.0))
    bias = jnp.concatenate([bias] * 4, axis=0)
    q = q_ref[0]
    outs = [None] * 4

    def scores(kv):
        qs = jnp.concatenate([jnp.where(_half_mask(kv), q[:, g * LANES:(g + 1) * LANES],
                                        jnp.zeros((blk, LANES), BF16)) for g in range(4)], axis=0)
        return _dot_nt(qs, kband) + bias

    s_all = [scores(kv) for kv in range(SW_KV)]
    for kv in range(SW_KV):
        s = s_all[kv]
        sink = jnp.concatenate([jnp.full((blk, 1), sink_ref[kv * 4 + g] * LOG2E, F32) for g in range(4)], axis=0)
        m = jnp.maximum(jnp.max(s, -1, keepdims=True), sink)
        p = jnp.exp2((s - m).astype(BF16))
        pv, l = _split_pv(_dot(p, _v_with_ones(vband, kv)), kv)
        o = pv / (l + jnp.exp2(sink - m))
        for g in range(4):
            part = o[g * blk:(g + 1) * blk]
            outs[g] = part if outs[g] is None else outs[g] + part
    o_ref[0] = jnp.concatenate(outs, axis=1).astype(BF16)


def _swa(p_rope, sink, N, L):
    B = p_rope.shape[0]
    nb = N // SW_BLOCK
    kcol, vcol = R_MISC * 4, R_MISC * 4 + 1
    kv_spec = lambda colblk, off: pl.BlockSpec(
        (1, SW_BLOCK, LANES), lambda b, i: (b, jnp.clip(i + off, 0, nb - 1), colblk))
    return pl.pallas_call(
        _swa_kernel,
        grid=(B, nb),
        in_specs=[pl.BlockSpec(memory_space=pltpu.SMEM),
                  pl.BlockSpec((1, SW_BLOCK, TN), lambda b, i: (b, i, R_SQ)),
                  kv_spec(kcol, -1), kv_spec(kcol, 0), kv_spec(kcol, 1),
                  kv_spec(vcol, -1), kv_spec(vcol, 0), kv_spec(vcol, 1),
                  pl.BlockSpec((1, L, LANES), lambda b, i: (b, N // L, kcol)),
                  pl.BlockSpec((1, L, LANES), lambda b, i: (b, N // L, vcol))],
        out_specs=pl.BlockSpec((1, SW_BLOCK, TN), lambda b, i: (b, i, 0)),
        out_shape=jax.ShapeDtypeStruct((B, N, TN), BF16),
        compiler_params=_cparams(("arbitrary", "arbitrary")),
        name="window_attn",
    )(sink, *([p_rope] * 9))


NA_QROWS = 2
NA_KBLKS = 5


def _na_kernel(bias_ref, q_ref, k0, k1, k2, k3, k4, v0, v1, v2, v3, v4, kx_ref, vx_ref, o_ref):
    kall = jnp.concatenate([k0[0], k1[0], k2[0], k3[0], k4[0], kx_ref[0]], axis=0)
    vall = jnp.concatenate([v0[0], v1[0], v2[0], v3[0], v4[0], vx_ref[0]], axis=0)
    q = q_ref[0]
    blocks = [slice(pb * LANES, (pb + 1) * LANES) for pb in range(4)]
    no_bias = jnp.zeros((q.shape[0], kx_ref.shape[1]), F32)
    scores = [[_dot_nt(jnp.where(_half_mask(hh), q[:, sl], jnp.zeros((q.shape[0], LANES), BF16)), kall[:, sl])
               + jnp.concatenate([bias_ref[0, pb * 2 + hh], no_bias], axis=1)
               for hh in range(2)] for pb, sl in enumerate(blocks)]
    outs = []
    for pb, sl in enumerate(blocks):
        vb = vall[:, sl]
        ob = None
        for hh in range(2):
            s = scores[pb][hh]
            p = jnp.exp2((s - jnp.max(s, -1, keepdims=True)).astype(BF16))
            pv, l = _split_pv(_dot(p, _v_with_ones(vb, hh)), hh)
            o = pv / l
            ob = o if ob is None else ob + o
        outs.append(ob)
    o_ref[0] = jnp.concatenate(outs, axis=1).astype(BF16)


def _col_bias_kernel(r_ref, sel_ref, neg_ref, o_ref):
    o_ref[...] = sum(_dot(part, sel_ref[...]) for part in reversed(_split3(r_ref[...]))) * LOG2E + neg_ref[...]


def _na_col_bias(rpb_all):
    W = GRID_W
    nco = 2 * NA_COLS - 1
    lead = rpb_all.shape[:3]
    rows = int(np.prod(lead))
    c = np.arange(W)
    cb = np.clip(c - NA_COLS // 2, 0, W - NA_COLS)
    dc = c[None, :] - cb[:, None]
    inside = (dc >= 0) & (dc < NA_COLS)
    co = c[None, :] - c[:, None] + NA_COLS - 1
    sel = ((co[None] == np.arange(32)[:, None, None]) & inside[None]).reshape(32, W * W)
    neg = np.where(inside, 0.0, NEG).reshape(1, W * W).astype(np.float32)
    table = jnp.pad(rpb_all.astype(F32).reshape(rows, nco), ((0, (-rows) % 8), (0, 32 - nco)))
    out = pl.pallas_call(
        _col_bias_kernel,
        out_shape=jax.ShapeDtypeStruct((table.shape[0], W * W), F32),
        name="na_col_bias",
    )(table, jnp.asarray(sel, BF16), jnp.asarray(neg))
    return out[:rows].reshape(lead + (W, W))


def _na_bias(t2, R, L):
    W = GRID_W
    depth, H = t2.shape[:2]
    clamp = lambda v, lo, hi: min(max(v, lo), hi)
    nblk = R // NA_QROWS
    slots = NA_KBLKS * NA_QROWS
    t2c = t2.transpose(0, 1, 3, 2, 4)
    per_cls = []
    for i in (0, 1, 2, nblk - 2, nblk - 1):
        base_blk = clamp(i - 2, 0, nblk - NA_KBLKS)
        per_t = []
        for t in range(NA_QROWS):
            rq = NA_QROWS * i + t
            base = clamp(rq - NA_ROWS // 2, 0, R - NA_ROWS)
            s_lo = base - NA_QROWS * base_blk
            assert 0 <= s_lo and s_lo + NA_ROWS <= slots
            ro = base - rq + NA_ROWS - 1
            per_t.append(jnp.pad(t2c[:, :, :, ro:ro + NA_ROWS],
                                 ((0, 0), (0, 0), (0, 0), (s_lo, slots - NA_ROWS - s_lo), (0, 0)),
                                 constant_values=NEG))
        per_cls.append(jnp.stack(per_t, axis=2))
    return jnp.stack(per_cls, axis=1).reshape(depth, 5, H, NA_QROWS * W, slots * W)


def _na(p_plain, bias, N, L):
    B = p_plain.shape[0]
    rows = NA_QROWS * GRID_W
    nblk = N // rows
    assert nblk >= NA_KBLKS + 2
    cls = lambda i: jnp.where(i < 2, i, jnp.where(i >= nblk - 2, i - (nblk - 5), 2))
    kv_spec = lambda tile, s: pl.BlockSpec(
        (1, rows, TN), lambda b, i: (b, jnp.clip(i - 2, 0, nblk - NA_KBLKS) + s, tile))
    return pl.pallas_call(
        _na_kernel,
        grid=(B, nblk),
        in_specs=[pl.BlockSpec((1, NA_HEADS, rows, NA_KBLKS * rows), lambda b, i: (cls(i), 0, 0, 0)),
                  pl.BlockSpec((1, rows, TN), lambda b, i: (b, i, P_NQ))]
                 + [kv_spec(P_NK, s) for s in range(NA_KBLKS)]
                 + [kv_spec(P_NV, s) for s in range(NA_KBLKS)]
                 + [pl.BlockSpec((1, L, TN), lambda b, i: (b, N // L, P_NK)),
                    pl.BlockSpec((1, L, TN), lambda b, i: (b, N // L, P_NV))],
        out_specs=pl.BlockSpec((1, rows, TN), lambda b, i: (b, i, 0)),
        out_shape=jax.ShapeDtypeStruct((B, N, TN), BF16),
        compiler_params=_cparams(("arbitrary", "arbitrary")),
        name="neighbourhood_attn",
    )(bias, *([p_plain] * (2 * NA_KBLKS + 3)))


def _ctx_attend_kernel(sink_ref, q_ref, k_ref, v_ref, o_ref, *, gqa):
    q = q_ref[0]
    k = k_ref[0]
    v = v_ref[0]
    outs = []
    for blk in range(4):
        sl = slice(blk * LANES, (blk + 1) * LANES)
        qb = q[:, sl]
        kb, vb = (k, v) if gqa else (k[:, sl], v[:, sl])
        ob = None
        for hh in range(2):
            head = hh * 4 + blk if gqa else blk * 2 + hh
            s = _dot_nt(jnp.where(_half_mask(hh), qb, jnp.zeros_like(qb)), kb)
            sink = sink_ref[head] * LOG2E
            m = jnp.maximum(jnp.max(s, -1, keepdims=True), sink)
            p = jnp.exp2((s - m).astype(BF16))
            pv, l = _split_pv(_dot(p, _v_with_ones(vb, hh)), hh)
            o = pv / (l + jnp.exp2(sink - m))
            ob = o if ob is None else ob + o
        outs.append(ob)
    o_ref[0] = jnp.concatenate(outs, axis=1).astype(BF16)


def _ctx_attend(p, sink, q_tile, k_blk, v_blk, N, L, *, gqa):
    B = p.shape[0]
    kw = LANES if gqa else TN
    rb = N // L
    return pl.pallas_call(
        functools.partial(_ctx_attend_kernel, gqa=gqa),
        grid=(B,),
        in_specs=[pl.BlockSpec(memory_space=pltpu.SMEM),
                  pl.BlockSpec((1, L, TN), lambda b: (b, rb, q_tile)),
                  pl.BlockSpec((1, L, kw), lambda b: (b, rb, k_blk)),
                  pl.BlockSpec((1, L, kw), lambda b: (b, rb, v_blk))],
        out_specs=pl.BlockSpec((1, L, TN), lambda b: (b, 0, 0)),
        out_shape=jax.ShapeDtypeStruct((B, L, TN), BF16),
        compiler_params=_cparams(("arbitrary",)),
        name="ctx_attend_gqa" if gqa else "ctx_attend",
    )(sink, p, p, p)


CONV_BLK = 256
HALO = 16


def _conv_kernel(prev_ref, cur_ref, next_ref, w_ref, b_ref, o_ref, *, lat_blocks):
    i = pl.program_id(1)
    last = pl.num_programs(1) - 1
    has_prev = (i != 0) & (i != lat_blocks)
    has_next = (i != lat_blocks - 1) & (i != last)
    prev = prev_ref[0].astype(F32) * has_prev.astype(F32)
    nxt = next_ref[0].astype(F32) * has_next.astype(F32)
    ext = jnp.concatenate([prev, cur_ref[0].astype(F32), nxt], axis=0)
    n = ext.shape[0]
    w = w_ref[...]
    acc = None
    for k in range(SSM_CONV):
        shift = (SSM_CONV // 2 - k) % n
        t = (ext if shift == 0 else pltpu.roll(ext, shift, 0))[HALO:HALO + CONV_BLK] * w[k:k + 1]
        acc = t if acc is None else acc + t
    acc = acc + b_ref[...]
    o_ref[0] = (acc * jax.nn.sigmoid(acc)).astype(BF16)


def _conv_silu(p_plain, conv_w, conv_b, N):
    B, T, _ = p_plain.shape
    C = SSM_XBC
    assert N % CONV_BLK == 0 and T % CONV_BLK == 0
    per = CONV_BLK // HALO
    nh = T // HALO
    w = jnp.zeros((8, C), F32).at[:SSM_CONV].set(conv_w.astype(F32))
    return pl.pallas_call(
        functools.partial(_conv_kernel, lat_blocks=N // CONV_BLK),
        grid=(B, T // CONV_BLK),
        in_specs=[pl.BlockSpec((1, HALO, C), lambda b, i: (b, jnp.maximum(i * per - 1, 0), 0)),
                  pl.BlockSpec((1, CONV_BLK, C), lambda b, i: (b, i, 0)),
                  pl.BlockSpec((1, HALO, C), lambda b, i: (b, jnp.minimum((i + 1) * per, nh - 1), 0)),
                  pl.BlockSpec((8, C), lambda b, i: (0, 0)),
                  pl.BlockSpec((1, C), lambda b, i: (0, 0))],
        out_specs=pl.BlockSpec((1, CONV_BLK, C), lambda b, i: (b, i, 0)),
        out_shape=jax.ShapeDtypeStruct((B, T, C), BF16),
        compiler_params=_cparams(("arbitrary", "arbitrary")),
        name="ssm_conv",
    )(p_plain, p_plain, p_plain, w, conv_b.reshape(1, C).astype(F32))


def _split3(t):
    hi = t.astype(BF16)
    r = t - hi.astype(F32)
    mid = r.astype(BF16)
    return hi, mid, (r - mid.astype(F32)).astype(BF16)


def _ssd_chain(u, dtr, dtb, alog, tri, expm, state, *, fwd):
    last = CHUNK - 1 if fwd else 0
    xs = u[:, :SSM_INNER].astype(F32)
    bm = u[:, SSM_INNER:SSM_INNER + SSM_GROUPS * SSM_STATE]
    cm = u[:, SSM_INNER + SSM_GROUPS * SSM_STATE:]
    dt = jax.nn.softplus(dtr + dtb)
    a = dt * (-jnp.exp(alog))
    acum = sum(_dot(tri, part) for part in reversed(_split3(a)))
    eac = jnp.exp(acum)
    dst = jnp.exp(acum[last:last + 1] - acum)
    e3 = sum(_dot(part, expm) for part in reversed(_split3(jnp.concatenate([dt, eac, dst], axis=0))[:2]))
    dt_e, eac_e, dst_e = e3[:CHUNK], e3[CHUNK:2 * CHUNK], e3[2 * CHUNK:]
    xg = xs * dt_e
    xg_b = xg.astype(BF16)
    xgd_b = (xg * dst_e).astype(BF16)
    acum_t = acum.T
    row = lax.broadcasted_iota(jnp.int32, (CHUNK, CHUNK), 0)
    col = lax.broadcasted_iota(jnp.int32, (CHUNK, CHUNK), 1)
    keep = (row >= col) if fwd else (row <= col)
    state_b = state.astype(BF16)
    gw = SSM_INNER // SSM_GROUPS
    ys, st_new = [], []
    for g in range(SSM_GROUPS):
        bmg = bm[:, g * SSM_STATE:(g + 1) * SSM_STATE]
        cmg = cm[:, g * SSM_STATE:(g + 1) * SSM_STATE]
        cb = _dot_nt(cmg, bmg)
        yoff = _dot(cmg, state_b[:, g * gw:(g + 1) * gw])
        st_new.append(_dot_tn(bmg, xgd_b[:, g * gw:(g + 1) * gw]))
        for pb in range(gw // LANES):
            blk = g * (gw // LANES) + pb
            sl = slice(blk * LANES, (blk + 1) * LANES)
            xgb = xg_b[:, sl]
            yd = None
            for hh in range(2):
                h = blk * 2 + hh
                seg = acum[:, h:h + 1] - acum_t[h:h + 1, :]
                ld = jnp.exp(jnp.where(keep, seg, -jnp.inf))
                t = _dot((cb * ld).astype(BF16), jnp.where(_half_mask(hh), xgb, jnp.zeros_like(xgb)))
                yd = t if yd is None else yd + t
            ys.append(yd + yoff[:, pb * LANES:(pb + 1) * LANES] * eac_e[:, sl])
    y = jnp.concatenate(ys, axis=1)
    return y, state * eac_e[last:last + 1] + jnp.concatenate(st_new, axis=1)


def _ssd_kernel(uf_ref, ub_ref, dtf_ref, dtb_ref, bias_ref, alog_ref, tri_ref, exp_ref, yf_ref, yb_ref, h_scr):
    @pl.when(pl.program_id(0) == 0)
    def _():
        h_scr[...] = jnp.zeros(h_scr.shape, F32)

    for b in range(uf_ref.shape[0]):
        for d, (u_ref, dt_ref, y_ref) in enumerate(((uf_ref, dtf_ref, yf_ref), (ub_ref, dtb_ref, yb_ref))):
            y, st = _ssd_chain(u_ref[b], dt_ref[b], bias_ref[d], alog_ref[d], tri_ref[d], exp_ref[...],
                               h_scr[b, d], fwd=(d == 0))
            y_ref[b] = y
            h_scr[b, d] = st


def _ssd(u, dtr, dt_bias, a_log, N):
    B, T, _ = u.shape
    nc, nlat = T // CHUNK, N // CHUNK
    ncx = nc - nlat
    chunk_f = lambda c: jnp.where(c < ncx, nlat + c, c - ncx)
    chunk_b = lambda c: jnp.where(c < ncx, nlat + ncx - 1 - c, nlat - 1 - (c - ncx))
    pad = lambda t: jnp.zeros((2, 1, LANES), F32).at[:, 0, :SSM_HEADS].set(t.astype(F32))
    li = np.arange(CHUNK)
    tri = np.stack([li[:, None] >= li[None, :], li[:, None] <= li[None, :]]).astype(np.float32)
    expm = np.zeros((LANES, SSM_INNER), np.float32)
    for h in range(SSM_HEADS):
        expm[h, h * 64:(h + 1) * 64] = 1.0
    full = lambda shape: pl.BlockSpec(shape, lambda c: (0,) * len(shape))
    y_shape = jax.ShapeDtypeStruct((B, T, SSM_INNER), F32)
    return pl.pallas_call(
        _ssd_kernel,
        grid=(nc,),
        in_specs=[pl.BlockSpec((B, CHUNK, SSM_XBC), lambda c: (0, chunk_f(c), 0)),
                  pl.BlockSpec((B, CHUNK, SSM_XBC), lambda c: (0, chunk_b(c), 0)),
                  pl.BlockSpec((B, CHUNK, LANES), lambda c: (0, chunk_f(c), 0)),
                  pl.BlockSpec((B, CHUNK, LANES), lambda c: (0, chunk_b(c), 1)),
                  full((2, 1, LANES)), full((2, 1, LANES)), full((2, CHUNK, CHUNK)), full((LANES, SSM_INNER))],
        out_specs=[pl.BlockSpec((B, CHUNK, SSM_INNER), lambda c: (0, chunk_f(c), 0)),
                   pl.BlockSpec((B, CHUNK, SSM_INNER), lambda c: (0, chunk_b(c), 0))],
        out_shape=[y_shape, y_shape],
        scratch_shapes=[pltpu.VMEM((B, 2, SSM_STATE, SSM_INNER), F32)],
        compiler_params=_cparams(("arbitrary",)),
        name="ssd_scan",
    )(u, u, dtr, dtr, pad(dt_bias), pad(a_log), jnp.asarray(tri, BF16), jnp.asarray(expm, BF16))


def _ssm_finish_kernel(yf_ref, yb_ref, u_ref, z_ref, dsk_ref, g_ref, o_ref):
    y = yf_ref[0] + yb_ref[0] + dsk_ref[...] * u_ref[0].astype(F32)
    z = z_ref[0].astype(F32)
    o_ref[0] = _rms(y * (z * jax.nn.sigmoid(z)), g_ref[...]).astype(BF16)


def _ssm_finish(y_f, y_b, u, p_plain, d_skip, g_ssm, *, tm=256):
    B, T, W = y_f.shape
    dsk = jnp.repeat(d_skip.astype(F32), W // SSM_HEADS).reshape(1, W)
    row_spec = pl.BlockSpec((1, tm, W), lambda b, i: (b, i, 0))
    return pl.pallas_call(
        _ssm_finish_kernel,
        grid=(B, T // tm),
        in_specs=[row_spec, row_spec, row_spec,
                  pl.BlockSpec((1, tm, W), lambda b, i: (b, i, P_BZ)),
                  pl.BlockSpec((1, W), lambda b, i: (0, 0)),
                  pl.BlockSpec((1, W), lambda b, i: (0, 0))],
        out_specs=row_spec,
        out_shape=jax.ShapeDtypeStruct((B, T, W), BF16),
        compiler_params=_cparams(("arbitrary", "arbitrary")),
        name="ssm_finish",
    )(y_f, y_b, u, p_plain, dsk, g_ssm.reshape(1, W).astype(F32))


def _merge_kernel(y0, y1, y2, y3, s0, s1, s2, s3, wb_ref, wo_ref, x_ref, gate_ref, g_ref, o_ref):
    m = None
    for k, (y, s) in enumerate(((y0, s0), (y1, s1), (y2, s2), (y3, s3))):
        t = _dot(y[0], wb_ref[k]) * s[0].astype(F32)
        m = t if m is None else m + t
    o = _dot(m.astype(BF16), wo_ref[...])
    o_ref[0] = x_ref[0] + gate_ref[0] * _rms(o, g_ref[...])


MERGE_TM = 256


def _merge(ya, yb_all, ys, yn, p_gate, w_branch, w_out, l, x, gate, g_post, *, row_off):
    B, Tx, D = x.shape
    tm = MERGE_TM
    assert Tx % tm == 0 and row_off % tm == 0
    off = row_off // tm
    y_spec = pl.BlockSpec((1, tm, BRANCH_W), lambda b, i: (b, i, 0))
    yb_spec = pl.BlockSpec((1, tm, BRANCH_W), lambda b, i: (b, off + i, 0))
    s_spec = lambda k: pl.BlockSpec((1, tm, D), lambda b, i: (b, off + i, k))
    resident = lambda shape: pl.BlockSpec((None,) + shape, lambda b, i: (l,) + (0,) * len(shape),
                                          pipeline_mode=pl.Buffered(1))
    return pl.pallas_call(
        _merge_kernel,
        grid=(B, Tx // tm),
        in_specs=[y_spec, yb_spec, y_spec, y_spec] + [s_spec(k) for k in range(N_BRANCH)]
                 + [resident((N_BRANCH, BRANCH_W, D)), resident((D, D)),
                    pl.BlockSpec((1, tm, D), lambda b, i: (b, i, 0)),
                    pl.BlockSpec((1, 1, D), lambda b, i: (b, 0, 0)),
                    pl.BlockSpec((1, D), lambda b, i: (0, 0))],
        out_specs=pl.BlockSpec((1, tm, D), lambda b, i: (b, i, 0)),
        out_shape=jax.ShapeDtypeStruct((B, Tx, D), F32),
        compiler_params=_cparams(("arbitrary", "arbitrary")),
        name="merge_out_proj",
    )(ya, yb_all, ys, yn, p_gate, p_gate, p_gate, p_gate, w_branch, w_out, x, gate, g_post)


def _ffn_kernel(x_ref, g1_ref, sh_ref, sc_ref, gate_ref, g2_ref, w1_ref, w2_ref, o_ref, h_scr, acc):
    f = pl.program_id(2)

    @pl.when(f == 0)
    def _():
        h = _rms(x_ref[0], g1_ref[...]) * (1.0 + sc_ref[0]) + sh_ref[0]
        h_scr[...] = h.astype(BF16)
        acc[...] = jnp.zeros(acc.shape, F32)

    a = jnp.square(jnp.maximum(_dot(h_scr[...], w1_ref[...]), 0.0)).astype(BF16)
    acc[...] += _dot(a, w2_ref[...])

    @pl.when(f == pl.num_programs(2) - 1)
    def _():
        o_ref[0] = x_ref[0] + gate_ref[0] * _rms(acc[...], g2_ref[...])


def _ffn(x, g_pre, shift, scale, gate, g_post, w1, w2, l, *, tm, tf=1024):
    B, T, D = x.shape
    F = w1.shape[2]
    tm = min(tm, T)
    mod_spec = pl.BlockSpec((1, 1, D), lambda b, i, f: (b, 0, 0))
    vec_spec = pl.BlockSpec((1, D), lambda b, i, f: (0, 0))
    return pl.pallas_call(
        _ffn_kernel,
        grid=(B, T // tm, F // tf),
        in_specs=[pl.BlockSpec((1, tm, D), lambda b, i, f: (b, i, 0)),
                  vec_spec, mod_spec, mod_spec, mod_spec, vec_spec,
                  pl.BlockSpec((None, D, tf), lambda b, i, f: (l, 0, f)),
                  pl.BlockSpec((None, tf, D), lambda b, i, f: (l, f, 0))],
        out_specs=pl.BlockSpec((1, tm, D), lambda b, i, f: (b, i, 0)),
        out_shape=jax.ShapeDtypeStruct((B, T, D), F32),
        scratch_shapes=[pltpu.VMEM((tm, D), BF16), pltpu.VMEM((tm, D), F32)],
        compiler_params=_cparams(("arbitrary", "arbitrary", "arbitrary")),
        name="mlp",
    )(x, g_pre, shift, scale, gate, g_post, w1, w2)


def _layout_w_in(w):
    depth, D, _ = w.shape
    bounds, o = {}, 0
    for name, width in (("aq", 512), ("ak", 512), ("av", 512), ("bz", 512), ("bx", SSM_XBC), ("bdt", 16),
                        ("sq", 512), ("sk", 128), ("sv", 128), ("nq", 512), ("nk", 512), ("nv", 512),
                        ("gate", N_BRANCH * D)):
        bounds[name] = (o, o + width)
        o += width
    assert o == w.shape[2]
    wb = w.astype(BF16)
    part = lambda name: wb[:, :, bounds[name][0]:bounds[name][1]]
    qpart = lambda name: w[:, :, bounds[name][0]:bounds[name][1]] * (QSCALE * LOG2E)
    sq = qpart("sq").reshape(depth, D, SW_KV, 4, HEAD_DIM).transpose(0, 1, 3, 2, 4).reshape(depth, D, 512)
    zpad = jnp.zeros((depth, D, LANES - SSM_HEADS), BF16)
    bdt = part("bdt")
    misc = jnp.concatenate([part("sk"), part("sv"), bdt[:, :, :SSM_HEADS], zpad, bdt[:, :, SSM_HEADS:], zpad], axis=2)
    w_rope = jnp.concatenate([qpart("aq").astype(BF16), part("ak"), sq.astype(BF16), misc], axis=2)
    w_plain = jnp.concatenate([part("bx"), qpart("nq").astype(BF16), part("nk"), part("nv"), part("av"),
                               part("bz")], axis=2)
    return w_rope, w_plain, part("gate")


def _rope_tables(n, n_ctx):
    t = np.arange(n)
    nf = HEAD_DIM // 4
    inv = ROPE_BASE ** (-np.arange(nf, dtype=np.float32) / nf)
    row = (t // GRID_W).astype(np.float32)
    col = (t % GRID_W).astype(np.float32)
    ang = jnp.asarray(np.concatenate([row[:, None] * inv, col[:, None] * inv], -1).astype(np.float32))
    cos, sin = jnp.cos(ang), jnp.sin(ang)
    cos = jnp.concatenate([cos, jnp.ones((n_ctx, cos.shape[1]), F32)], axis=0)
    sin = jnp.concatenate([sin, jnp.zeros((n_ctx, sin.shape[1]), F32)], axis=0)
    z = jnp.zeros_like(sin)
    rep = LANES // HEAD_DIM
    cos_t = jnp.tile(jnp.concatenate([cos, cos], -1), (1, rep))
    s1 = jnp.tile(jnp.concatenate([-sin, z], -1), (1, rep))
    s2 = jnp.tile(jnp.concatenate([z, sin], -1), (1, rep))
    return cos_t, s1, s2


def kernel(x, c, ctx, c_ctx, w_mod, b_mod, g_pre_mix, g_post_mix, g_pre_mlp, g_post_mlp, w_in, lam_q1, lam_k1, lam_q2, lam_k2, g_subln, conv_w, conv_b, dt_bias, a_log, d_skip, g_ssm, sink, rpb, w_branch, w_out, w_ff1, w_ff2):
    B, N, D = x.shape
    L = ctx.shape[1]
    T = N + L
    depth = w_mod.shape[0]
    assert B + 1 <= 8 and N % GRID_W == 0 and N % L == 0
    R = N // GRID_W

    cvec = jnp.zeros((8, D), F32).at[:B].set(c).at[B].set(c_ctx)
    mods = _modulation(cvec, w_mod, b_mod)
    rope_tabs = _rope_tables(N, L)
    na_bias = _na_bias(_na_col_bias(rpb), R, L)
    no_sink = jnp.full((NA_HEADS,), NEG, F32)
    row = lambda v: v.reshape(1, -1).astype(F32)
    tk_lat = _largest_divisor(T, (768, 512, 256, 128))
    w_rope, w_plain, w_gate = _layout_w_in(w_in)
    wb = w_branch.at[:, 2].set(w_branch[:, 2].reshape(depth, SW_KV, 4, HEAD_DIM, D).transpose(0, 2, 1, 3, 4)
                               .reshape(depth, BRANCH_W, D)).astype(BF16)
    wo, w1, w2 = w_out.astype(BF16), w_ff1.astype(BF16), w_ff2.astype(BF16)

    cx = ctx
    for l in range(depth):
        need_ctx = l < depth - 1
        lam_init = 0.8 - 0.6 * math.exp(-0.3 * l)
        mod = mods[l, :B].reshape(B, 6, 1, D)
        modc = jnp.broadcast_to(mods[l, B].reshape(1, 6, 1, D), (B, 6, 1, D))
        both = lambda k: jnp.stack([mod[:, k], modc[:, k]], axis=1).reshape(B * 2, 1, D)

        h = _prenorm(x, cx, row(g_pre_mix[l]), both(0), both(1))
        p_rope, dtr = _proj_rope(h, rope_tabs, w_rope, l)
        p_plain = _proj(h, w_plain, l, gate=False, tn=TN)
        p_gate = _proj(h, w_gate, l, gate=True, tn=2048)

        lamv = jnp.stack([lam_q1[l], lam_k1[l], lam_q2[l], lam_k2[l]]).astype(F32)
        g_col = g_subln[l].reshape(LANES, 1).astype(F32)
        ya = _diff_attn(p_rope, p_plain, lamv, g_col, lam_init, q_row0=0, n_q=N, k_row0=0, n_k=T, tq=512, tk=tk_lat)

        u = _conv_silu(p_plain, conv_w[l], conv_b[l], N)
        y_f, y_b = _ssd(u, dtr, dt_bias[l], a_log[l], N)
        yb_all = _ssm_finish(y_f, y_b, u, p_plain, d_skip[l], g_ssm[l])

        ys = _swa(p_rope, sink[l].astype(F32), N, L)
        yn = _na(p_plain, na_bias[l], N, L)

        x_new = _merge(ya, yb_all, ys, yn, p_gate, wb, wo, l, x, mod[:, 2], row(g_post_mix[l]), row_off=0)
        x_new = _ffn(x_new, row(g_pre_mlp[l]), mod[:, 3], mod[:, 4], mod[:, 5], row(g_post_mlp[l]), w1, w2, l,
                     tm=512)
        if need_ctx:
            ya_c = _diff_attn(p_rope, p_plain, lamv, g_col, lam_init, q_row0=N, n_q=L, k_row0=N, n_k=L, tq=L, tk=L)
            ys_c = _ctx_attend(p_rope, sink[l].astype(F32), R_SQ, R_MISC * 4, R_MISC * 4 + 1, N, L, gqa=True)
            yn_c = _ctx_attend(p_plain, no_sink, P_NQ, P_NK, P_NV, N, L, gqa=False)
            cx = _merge(ya_c, yb_all, ys_c, yn_c, p_gate, wb, wo, l, cx, modc[:, 2], row(g_post_mix[l]), row_off=N)
            cx = _ffn(cx, row(g_pre_mlp[l]), modc[:, 3], modc[:, 4], modc[:, 5], row(g_post_mlp[l]), w1, w2, l,
                      tm=L)
        x = x_new
    return x
```

```python
import functools
import math

import numpy as np
import jax
import jax.numpy as jnp
from jax import lax
from jax.experimental import pallas as pl
from jax.experimental.pallas import tpu as pltpu

F32 = jnp.float32
BF16 = jnp.bfloat16

LANES = 128
VMEM_LIMIT = 56 * 1024 * 1024
NEG = -1e30
EPS = 1e-6
GRID_W = 64
ROPE_BASE = 10000.0
HEAD_DIM = 64
QSCALE = HEAD_DIM ** -0.5
LOG2E = math.log2(math.e)
DA_HEADS = 4
SSM_HEADS = 8
SSM_INNER = 512
SSM_GROUPS = 2
SSM_STATE = 128
SSM_XBC = SSM_INNER + 2 * SSM_GROUPS * SSM_STATE
SSM_CONV = 5
CHUNK = 128
SW_KV = 2
SW_BLOCK = 128
NA_HEADS = 8
NA_ROWS = 8
NA_COLS = 16
N_BRANCH = 4
BRANCH_W = 512

TN = 512
R_AQ, R_AK, R_SQ, R_MISC = 0, 1, 2, 3
P_NQ, P_NK, P_NV, P_AV, P_BZ = 2, 3, 4, 5, 6


def _cparams(sem):
    return pltpu.CompilerParams(dimension_semantics=sem, vmem_limit_bytes=VMEM_LIMIT)


def _rms(t, g):
    return t * lax.rsqrt(jnp.mean(t * t, -1, keepdims=True) + EPS) * g


def _dot(a, b):
    return jnp.dot(a, b, preferred_element_type=F32)


def _dot_nt(a, b):
    return lax.dot_general(a, b, (((1,), (1,)), ((), ())), preferred_element_type=F32)


def _dot_tn(a, b):
    return lax.dot_general(a, b, (((0,), (0,)), ((), ())), preferred_element_type=F32)


def _half_mask(hh):
    lane = lax.broadcasted_iota(jnp.int32, (1, LANES), 1)
    return (lane // HEAD_DIM) == hh


def _v_with_ones(v, hh):
    lane = lax.broadcasted_iota(jnp.int32, (1, LANES), 1)
    ones_lane = jnp.where(lane == (1 - hh) * HEAD_DIM, 1.0, 0.0).astype(v.dtype)
    return jnp.where(_half_mask(hh), v, jnp.zeros_like(v)) + ones_lane


def _split_pv(o, hh):
    spare = (1 - hh) * HEAD_DIM
    return jnp.where(_half_mask(hh), o, 0.0), o[:, spare:spare + 1]


def _largest_divisor(n, cands):
    for c in cands:
        if n % c == 0:
            return c
    raise ValueError(f"no tile in {cands} divides {n}")


MOD_KB = 256


def _mod_kernel(c_ref, wa_ref, wb_ref, b_ref, o_ref):
    k = pl.program_id(1)
    c = c_ref[...]
    s = (c * jax.nn.sigmoid(c)).astype(BF16)
    half = MOD_KB // 2
    part = _dot(s[:, :half], wa_ref[0].astype(BF16)) + _dot(s[:, half:], wb_ref[0].astype(BF16))

    @pl.when(k == 0)
    def _():
        o_ref[0] = part + b_ref[0]

    @pl.when(k > 0)
    def _():
        o_ref[0] += part


def _modulation(cvec, w_mod, b_mod):
    depth, D, C = w_mod.shape
    return pl.pallas_call(
        _mod_kernel,
        grid=(depth, D // MOD_KB),
        in_specs=[pl.BlockSpec((8, MOD_KB), lambda l, k: (0, k)),
                  pl.BlockSpec((1, MOD_KB // 2, C), lambda l, k: (l, 2 * k, 0)),
                  pl.BlockSpec((1, MOD_KB // 2, C), lambda l, k: (l, 2 * k + 1, 0)),
                  pl.BlockSpec((1, 1, C), lambda l, k: (l, 0, 0))],
        out_specs=pl.BlockSpec((1, 8, C), lambda l, k: (l, 0, 0)),
        out_shape=jax.ShapeDtypeStruct((depth, 8, C), F32),
        compiler_params=_cparams(("arbitrary", "arbitrary")),
        name="modulation",
    )(cvec, w_mod, w_mod, b_mod.reshape(depth, 1, C))


NORM_BLK = 256


def _prenorm_kernel(x_ref, cx_ref, g_ref, sh_ref, sc_ref, o_ref, *, nlat):
    i = pl.program_id(1)

    def emit(t):
        o_ref[0] = (_rms(t, g_ref[...]) * (1.0 + sc_ref[0]) + sh_ref[0]).astype(BF16)

    @pl.when(i < nlat)
    def _():
        emit(x_ref[0])

    @pl.when(i >= nlat)
    def _():
        emit(cx_ref[0])


def _prenorm(x, cx, g, shift2, scale2):
    B, N, D = x.shape
    L = cx.shape[1]
    assert N % NORM_BLK == 0 and L % NORM_BLK == 0
    nlat, nctx = N // NORM_BLK, L // NORM_BLK
    mod_spec = pl.BlockSpec((1, 1, D), lambda b, i: (b * 2 + (i >= nlat).astype(jnp.int32), 0, 0))
    return pl.pallas_call(
        functools.partial(_prenorm_kernel, nlat=nlat),
        grid=(B, nlat + nctx),
        in_specs=[pl.BlockSpec((1, NORM_BLK, D), lambda b, i: (b, jnp.minimum(i, nlat - 1), 0)),
                  pl.BlockSpec((1, NORM_BLK, D), lambda b, i: (b, jnp.maximum(i - nlat, 0), 0)),
                  pl.BlockSpec((1, D), lambda b, i: (0, 0)),
                  mod_spec, mod_spec],
        out_specs=pl.BlockSpec((1, NORM_BLK, D), lambda b, i: (b, i, 0)),
        out_shape=jax.ShapeDtypeStruct((B, N + L, D), BF16),
        compiler_params=_cparams(("arbitrary", "arbitrary")),
        name="prenorm",
    )(x, cx, g, shift2, scale2)


def _rope(t, cos, s1, s2):
    w = t.shape[1]
    reps = w // LANES
    tile = lambda a: a if reps == 1 else jnp.concatenate([a] * reps, axis=1)
    return (t * tile(cos) + pltpu.roll(t, w - HEAD_DIM // 2, 1) * tile(s1)
            + pltpu.roll(t, HEAD_DIM // 2, 1) * tile(s2))


def _proj_rope_kernel(h_ref, cos_ref, s1_ref, s2_ref, w_ref, o_ref, odt_ref):
    j = pl.program_id(2)
    acc = _dot_nt(h_ref[0], w_ref[...])
    rot = lambda t: _rope(t, cos_ref[...], s1_ref[...], s2_ref[...])

    @pl.when(j != R_MISC)
    def _():
        o_ref[0] = rot(acc).astype(BF16)

    @pl.when(j == R_MISC)
    def _():
        o_ref[0] = jnp.concatenate([rot(acc[:, :LANES]), acc[:, LANES:]], axis=1).astype(BF16)
        odt_ref[0] = acc[:, 2 * LANES:]


def _row_tile(T):
    return _largest_divisor(T, (1056, 1024, 768, 640, 512, 256, 128))


def _proj_rope(h, rope_tabs, w, l):
    B, T, D = h.shape
    C = w.shape[1]
    tm = _row_tile(T)
    cos, s1, s2 = rope_tabs
    tab_spec = pl.BlockSpec((tm, LANES), lambda b, i, j: (i, 0))
    return pl.pallas_call(
        _proj_rope_kernel,
        grid=(B, T // tm, C // TN),
        in_specs=[pl.BlockSpec((1, tm, D), lambda b, i, j: (b, i, 0)),
                  tab_spec, tab_spec, tab_spec,
                  pl.BlockSpec((None, TN, D), lambda b, i, j: (l, j, 0))],
        out_specs=[pl.BlockSpec((1, tm, TN), lambda b, i, j: (b, i, j)),
                   pl.BlockSpec((1, tm, 2 * LANES), lambda b, i, j: (b, i, 0))],
        out_shape=[jax.ShapeDtypeStruct((B, T, C), BF16),
                   jax.ShapeDtypeStruct((B, T, 2 * LANES), F32)],
        compiler_params=_cparams(("arbitrary", "arbitrary", "arbitrary")),
        name="proj_rope",
    )(h, cos, s1, s2, w)


def _proj_kernel(h_ref, w_ref, o_ref, *, gate):
    acc = _dot_nt(h_ref[0], w_ref[...])
    o_ref[0] = (jax.nn.sigmoid(acc) if gate else acc).astype(BF16)


def _proj(h, w, l, *, gate, tn):
    B, T, D = h.shape
    C = w.shape[1]
    tm = _row_tile(T)
    return pl.pallas_call(
        functools.partial(_proj_kernel, gate=gate),
        grid=(B, T // tm, C // tn),
        in_specs=[pl.BlockSpec((1, tm, D), lambda b, i, j: (b, i, 0)),
                  pl.BlockSpec((None, tn, D), lambda b, i, j: (l, j, 0))],
        out_specs=pl.BlockSpec((1, tm, tn), lambda b, i, j: (b, i, j)),
        out_shape=jax.ShapeDtypeStruct((B, T, C), BF16),
        compiler_params=_cparams(("arbitrary", "arbitrary", "arbitrary")),
        name="proj_gate" if gate else "proj_plain",
    )(h, w)


VT_ROWS = LANES + 16


def _diff_attn_kernel(lam_ref, g_ref, q_ref, k_ref, v_ref, o_ref, s_scr, vt_scr, *, lam_init, tk):
    q = q_ref[0]
    tq = q.shape[0]
    nk = k_ref.shape[1] // tk
    q_m = [jnp.where(_half_mask(m), q, jnp.zeros_like(q)) for m in range(2)]

    @pl.when(pl.program_id(2) == 0)
    def _():
        ones_row = (lax.broadcasted_iota(jnp.int32, (VT_ROWS - LANES, tk), 0) == 0).astype(BF16)
        for c in range(nk):
            v_t = v_ref[0, c * tk:(c + 1) * tk, :].astype(F32).T.astype(BF16)
            vt_scr[c] = jnp.concatenate([v_t, ones_row], axis=0)

    def scores(c):
        kc = k_ref[0, c * tk:(c + 1) * tk, :]
        for m in range(2):
            s_scr[c % 2, m] = _dot_nt(kc, q_m[m])

    carry = [(jnp.full((1, tq), -jnp.inf, F32), jnp.zeros((VT_ROWS, tq), F32)) for _ in range(2)]
    scores(0)
    for c in range(nk):
        if c + 1 < nk:
            scores(c + 1)
        vc = vt_scr[c]
        for m in range(2):
            mx, acc = carry[m]
            m_new = jnp.maximum(mx, jnp.max(s_scr[c % 2, m], 0, keepdims=True))
            p = jnp.exp2((s_scr[c % 2, m] - m_new).astype(BF16))
            acc = jnp.exp2(mx - m_new) * acc + _dot(vc, p)
            carry[m] = (m_new, acc)
    (_, a0), (_, a1) = carry
    lv = lam_ref[...]
    lam = (jnp.exp(jnp.sum(lv[0:1] * lv[1:2], -1, keepdims=True))
           - jnp.exp(jnp.sum(lv[2:3] * lv[3:4], -1, keepdims=True)) + lam_init)
    o = a0[:LANES] / a0[LANES:LANES + 1] - lam * (a1[:LANES] / a1[LANES:LANES + 1])
    o = o * lax.rsqrt(jnp.mean(o * o, 0, keepdims=True) + EPS) * g_ref[...] * (1.0 - lam_init)
    o_ref[0] = o.T.astype(BF16)


def _diff_attn(p_rope, p_plain, lamv, g_col, lam_init, *, q_row0, n_q, k_row0, n_k, tq, tk):
    B = p_rope.shape[0]
    tq = min(tq, n_q)
    assert n_q % tq == 0 and q_row0 % tq == 0 and k_row0 % n_k == 0 and n_k % tk == 0
    qb, kb = q_row0 // tq, k_row0 // n_k
    return pl.pallas_call(
        functools.partial(_diff_attn_kernel, lam_init=lam_init, tk=tk),
        grid=(B, DA_HEADS, n_q // tq),
        in_specs=[pl.BlockSpec((4, HEAD_DIM), lambda b, h, i: (0, 0)),
                  pl.BlockSpec((LANES, 1), lambda b, h, i: (0, 0)),
                  pl.BlockSpec((1, tq, LANES), lambda b, h, i: (b, qb + i, R_AQ * 4 + h)),
                  pl.BlockSpec((1, n_k, LANES), lambda b, h, i: (b, kb, R_AK * 4 + h)),
                  pl.BlockSpec((1, n_k, LANES), lambda b, h, i: (b, kb, P_AV * 4 + h))],
        out_specs=pl.BlockSpec((1, tq, LANES), lambda b, h, i: (b, i, h)),
        out_shape=jax.ShapeDtypeStruct((B, n_q, DA_HEADS * LANES), BF16),
        scratch_shapes=[pltpu.VMEM((2, 2, tk, tq), F32),
                        pltpu.VMEM((n_k // tk, VT_ROWS, tk), BF16)],
        compiler_params=_cparams(("arbitrary",) * 3),
        name="diff_attn",
    )(lamv, g_col, p_rope, p_rope, p_plain)


def _swa_kernel(sink_ref, q_ref, kp_ref, kc_ref, kn_ref, vp_ref, vc_ref, vn_ref, kx_ref, vx_ref, o_ref):
    i = pl.program_id(1)
    nb = pl.num_programs(1)
    blk = SW_BLOCK
    kband = jnp.concatenate([kp_ref[0], kc_ref[0], kn_ref[0], kx_ref[0]], axis=0)
    vband = jnp.concatenate([vp_ref[0], vc_ref[0], vn_ref[0], vx_ref[0]], axis=0)
    nkeys = kband.shape[0]
    row = lax.broadcasted_iota(jnp.int32, (blk, nkeys), 0)
    col = lax.broadcasted_iota(jnp.int32, (blk, nkeys), 1)
    big = 4 * blk
    off_prev = jnp.where(i > 0, 0, big)
    off_next = jnp.where(i < nb - 1, 0, big)
    ok_prev = jnp.where(col >= row + off_prev, 0.0, NEG)
    ok_next = jnp.where(col - 2 * blk <= row - off_next, 0.0, NEG)
    bias = jnp.where(col < blk, ok_prev, jnp.where((col >= 2 * blk) & (col < 3 * blk), ok_next, 0.0))
    bias = jnp.concatenate([bias] * 4, axis=0)
    q = q_ref[0]
    outs = [None] * 4

    def scores(kv):
        qs = jnp.concatenate([jnp.where(_half_mask(kv), q[:, g * LANES:(g + 1) * LANES],
                                        jnp.zeros((blk, LANES), BF16)) for g in range(4)], axis=0)
        return _dot_nt(qs, kband) + bias

    s_all = [scores(kv) for kv in range(SW_KV)]
    for kv in range(SW_KV):
        s = s_all[kv]
        sink = jnp.concatenate([jnp.full((blk, 1), sink_ref[kv * 4 + g] * LOG2E, F32) for g in range(4)], axis=0)
        m = jnp.maximum(jnp.max(s, -1, keepdims=True), sink)
        p = jnp.exp2((s - m).astype(BF16))
        pv, l = _split_pv(_dot(p, _v_with_ones(vband, kv)), kv)
        o = pv / (l + jnp.exp2(sink - m))
        for g in range(4):
            part = o[g * blk:(g + 1) * blk]
            outs[g] = part if outs[g] is None else outs[g] + part
    o_ref[0] = jnp.concatenate(outs, axis=1).astype(BF16)


def _swa(p_rope, sink, N, L):
    B = p_rope.shape[0]
    nb = N // SW_BLOCK
    kcol, vcol = R_MISC * 4, R_MISC * 4 + 1
    kv_spec = lambda colblk, off: pl.BlockSpec(
        (1, SW_BLOCK, LANES), lambda b, i: (b, jnp.clip(i + off, 0, nb - 1), colblk))
    return pl.pallas_call(
        _swa_kernel,
        grid=(B, nb),
        in_specs=[pl.BlockSpec(memory_space=pltpu.SMEM),
                  pl.BlockSpec((1, SW_BLOCK, TN), lambda b, i: (b, i, R_SQ)),
                  kv_spec(kcol, -1), kv_spec(kcol, 0), kv_spec(kcol, 1),
                  kv_spec(vcol, -1), kv_spec(vcol, 0), kv_spec(vcol, 1),
                  pl.BlockSpec((1, L, LANES), lambda b, i: (b, N // L, kcol)),
                  pl.BlockSpec((1, L, LANES), lambda b, i: (b, N // L, vcol))],
        out_specs=pl.BlockSpec((1, SW_BLOCK, TN), lambda b, i: (b, i, 0)),
        out_shape=jax.ShapeDtypeStruct((B, N, TN), BF16),
        compiler_params=_cparams(("arbitrary", "arbitrary")),
        name="window_attn",
    )(sink, *([p_rope] * 9))


NA_QROWS = 2
NA_KBLKS = 5


def _na_kernel(bias_ref, q_ref, k0, k1, k2, k3, k4, v0, v1, v2, v3, v4, kx_ref, vx_ref, o_ref):
    kall = jnp.concatenate([k0[0], k1[0], k2[0], k3[0], k4[0], kx_ref[0]], axis=0)
    vall = jnp.concatenate([v0[0], v1[0], v2[0], v3[0], v4[0], vx_ref[0]], axis=0)
    q = q_ref[0]
    blocks = [slice(pb * LANES, (pb + 1) * LANES) for pb in range(4)]
    no_bias = jnp.zeros((q.shape[0], kx_ref.shape[1]), F32)
    scores = [[_dot_nt(jnp.where(_half_mask(hh), q[:, sl], jnp.zeros((q.shape[0], LANES), BF16)), kall[:, sl])
               + jnp.concatenate([bias_ref[0, pb * 2 + hh], no_bias], axis=1)
               for hh in range(2)] for pb, sl in enumerate(blocks)]
    outs = []
    for pb, sl in enumerate(blocks):
        vb = vall[:, sl]
        ob = None
        for hh in range(2):
            s = scores[pb][hh]
            p = jnp.exp2((s - jnp.max(s, -1, keepdims=True)).astype(BF16))
            pv, l = _split_pv(_dot(p, _v_with_ones(vb, hh)), hh)
            o = pv / l
            ob = o if ob is None else ob + o
        outs.append(ob)
    o_ref[0] = jnp.concatenate(outs, axis=1).astype(BF16)


def _col_bias_kernel(r_ref, sel_ref, neg_ref, o_ref):
    o_ref[...] = sum(_dot(part, sel_ref[...]) for part in reversed(_split3(r_ref[...]))) * LOG2E + neg_ref[...]


def _na_col_bias(rpb_all):
    W = GRID_W
    nco = 2 * NA_COLS - 1
    lead = rpb_all.shape[:3]
    rows = int(np.prod(lead))
    c = np.arange(W)
    cb = np.clip(c - NA_COLS // 2, 0, W - NA_COLS)
    dc = c[None, :] - cb[:, None]
    inside = (dc >= 0) & (dc < NA_COLS)
    co = c[None, :] - c[:, None] + NA_COLS - 1
    sel = ((co[None] == np.arange(32)[:, None, None]) & inside[None]).reshape(32, W * W)
    neg = np.where(inside, 0.0, NEG).reshape(1, W * W).astype(np.float32)
    table = jnp.pad(rpb_all.astype(F32).reshape(rows, nco), ((0, (-rows) % 8), (0, 32 - nco)))
    out = pl.pallas_call(
        _col_bias_kernel,
        out_shape=jax.ShapeDtypeStruct((table.shape[0], W * W), F32),
        name="na_col_bias",
    )(table, jnp.asarray(sel, BF16), jnp.asarray(neg))
    return out[:rows].reshape(lead + (W, W))


def _na_bias(t2, R, L):
    W = GRID_W
    depth, H = t2.shape[:2]
    clamp = lambda v, lo, hi: min(max(v, lo), hi)
    nblk = R // NA_QROWS
    slots = NA_KBLKS * NA_QROWS
    per = []
    for i in (0, 1, 2, nblk - 2, nblk - 1):
        base_blk = clamp(i - 2, 0, nblk - NA_KBLKS)
        for t in range(NA_QROWS):
            rq = NA_QROWS * i + t
            base = clamp(rq - NA_ROWS // 2, 0, R - NA_ROWS)
            s_lo = base - NA_QROWS * base_blk
            assert 0 <= s_lo and s_lo + NA_ROWS <= slots
            ro = base - rq + NA_ROWS - 1
            per.append(jnp.pad(t2[:, :, ro:ro + NA_ROWS],
                               ((0, 0), (0, 0), (s_lo, slots - NA_ROWS - s_lo), (0, 0), (0, 0)),
                               constant_values=NEG))
    b = jnp.stack(per, axis=2).reshape(depth, H, 5, NA_QROWS, slots, W, W)
    return b.transpose(0, 2, 1, 3, 5, 4, 6).reshape(depth, 5, H, NA_QROWS * W, slots * W)


def _na(p_plain, bias, N, L):
    B = p_plain.shape[0]
    rows = NA_QROWS * GRID_W
    nblk = N // rows
    assert nblk >= NA_KBLKS + 2
    cls = lambda i: jnp.where(i < 2, i, jnp.where(i >= nblk - 2, i - (nblk - 5), 2))
    kv_spec = lambda tile, s: pl.BlockSpec(
        (1, rows, TN), lambda b, i: (b, jnp.clip(i - 2, 0, nblk - NA_KBLKS) + s, tile))
    return pl.pallas_call(
        _na_kernel,
        grid=(B, nblk),
        in_specs=[pl.BlockSpec((1, NA_HEADS, rows, NA_KBLKS * rows), lambda b, i: (cls(i), 0, 0, 0)),
                  pl.BlockSpec((1, rows, TN), lambda b, i: (b, i, P_NQ))]
                 + [kv_spec(P_NK, s) for s in range(NA_KBLKS)]
                 + [kv_spec(P_NV, s) for s in range(NA_KBLKS)]
                 + [pl.BlockSpec((1, L, TN), lambda b, i: (b, N // L, P_NK)),
                    pl.BlockSpec((1, L, TN), lambda b, i: (b, N // L, P_NV))],
        out_specs=pl.BlockSpec((1, rows, TN), lambda b, i: (b, i, 0)),
        out_shape=jax.ShapeDtypeStruct((B, N, TN), BF16),
        compiler_params=_cparams(("arbitrary", "arbitrary")),
        name="neighbourhood_attn",
    )(bias, *([p_plain] * (2 * NA_KBLKS + 3)))


def _ctx_attend_kernel(sink_ref, q_ref, k_ref, v_ref, o_ref, *, gqa):
    q = q_ref[0]
    k = k_ref[0]
    v = v_ref[0]
    outs = []
    for blk in range(4):
        sl = slice(blk * LANES, (blk + 1) * LANES)
        qb = q[:, sl]
        kb, vb = (k, v) if gqa else (k[:, sl], v[:, sl])
        ob = None
        for hh in range(2):
            head = hh * 4 + blk if gqa else blk * 2 + hh
            s = _dot_nt(jnp.where(_half_mask(hh), qb, jnp.zeros_like(qb)), kb)
            sink = sink_ref[head] * LOG2E
            m = jnp.maximum(jnp.max(s, -1, keepdims=True), sink)
            p = jnp.exp2((s - m).astype(BF16))
            pv, l = _split_pv(_dot(p, _v_with_ones(vb, hh)), hh)
            o = pv / (l + jnp.exp2(sink - m))
            ob = o if ob is None else ob + o
        outs.append(ob)
    o_ref[0] = jnp.concatenate(outs, axis=1).astype(BF16)


def _ctx_attend(p, sink, q_tile, k_blk, v_blk, N, L, *, gqa):
    B = p.shape[0]
    kw = LANES if gqa else TN
    rb = N // L
    return pl.pallas_call(
        functools.partial(_ctx_attend_kernel, gqa=gqa),
        grid=(B,),
        in_specs=[pl.BlockSpec(memory_space=pltpu.SMEM),
                  pl.BlockSpec((1, L, TN), lambda b: (b, rb, q_tile)),
                  pl.BlockSpec((1, L, kw), lambda b: (b, rb, k_blk)),
                  pl.BlockSpec((1, L, kw), lambda b: (b, rb, v_blk))],
        out_specs=pl.BlockSpec((1, L, TN), lambda b: (b, 0, 0)),
        out_shape=jax.ShapeDtypeStruct((B, L, TN), BF16),
        compiler_params=_cparams(("arbitrary",)),
        name="ctx_attend_gqa" if gqa else "ctx_attend",
    )(sink, p, p, p)


CONV_BLK = 256
HALO = 16


def _conv_kernel(prev_ref, cur_ref, next_ref, w_ref, b_ref, o_ref, *, lat_blocks):
    i = pl.program_id(1)
    last = pl.num_programs(1) - 1
    has_prev = (i != 0) & (i != lat_blocks)
    has_next = (i != lat_blocks - 1) & (i != last)
    prev = prev_ref[0].astype(F32) * has_prev.astype(F32)
    nxt = next_ref[0].astype(F32) * has_next.astype(F32)
    ext = jnp.concatenate([prev, cur_ref[0].astype(F32), nxt], axis=0)
    n = ext.shape[0]
    w = w_ref[...]
    acc = None
    for k in range(SSM_CONV):
        shift = (SSM_CONV // 2 - k) % n
        t = (ext if shift == 0 else pltpu.roll(ext, shift, 0))[HALO:HALO + CONV_BLK] * w[k:k + 1]
        acc = t if acc is None else acc + t
    acc = acc + b_ref[...]
    o_ref[0] = (acc * jax.nn.sigmoid(acc)).astype(BF16)


def _conv_silu(p_plain, conv_w, conv_b, N):
    B, T, _ = p_plain.shape
    C = SSM_XBC
    assert N % CONV_BLK == 0 and T % CONV_BLK == 0
    per = CONV_BLK // HALO
    nh = T // HALO
    w = jnp.zeros((8, C), F32).at[:SSM_CONV].set(conv_w.astype(F32))
    return pl.pallas_call(
        functools.partial(_conv_kernel, lat_blocks=N // CONV_BLK),
        grid=(B, T // CONV_BLK),
        in_specs=[pl.BlockSpec((1, HALO, C), lambda b, i: (b, jnp.maximum(i * per - 1, 0), 0)),
                  pl.BlockSpec((1, CONV_BLK, C), lambda b, i: (b, i, 0)),
                  pl.BlockSpec((1, HALO, C), lambda b, i: (b, jnp.minimum((i + 1) * per, nh - 1), 0)),
                  pl.BlockSpec((8, C), lambda b, i: (0, 0)),
                  pl.BlockSpec((1, C), lambda b, i: (0, 0))],
        out_specs=pl.BlockSpec((1, CONV_BLK, C), lambda b, i: (b, i, 0)),
        out_shape=jax.ShapeDtypeStruct((B, T, C), BF16),
        compiler_params=_cparams(("arbitrary", "arbitrary")),
        name="ssm_conv",
    )(p_plain, p_plain, p_plain, w, conv_b.reshape(1, C).astype(F32))


def _split3(t):
    hi = t.astype(BF16)
    r = t - hi.astype(F32)
    mid = r.astype(BF16)
    return hi, mid, (r - mid.astype(F32)).astype(BF16)


def _ssd_chain(u, dtr, dtb, alog, tri, expm, state, *, fwd):
    last = CHUNK - 1 if fwd else 0
    xs = u[:, :SSM_INNER].astype(F32)
    bm = u[:, SSM_INNER:SSM_INNER + SSM_GROUPS * SSM_STATE]
    cm = u[:, SSM_INNER + SSM_GROUPS * SSM_STATE:]
    dt = jax.nn.softplus(dtr + dtb)
    a = dt * (-jnp.exp(alog))
    acum = sum(_dot(tri, part) for part in reversed(_split3(a)))
    eac = jnp.exp(acum)
    dst = jnp.exp(acum[last:last + 1] - acum)
    e3 = sum(_dot(part, expm) for part in reversed(_split3(jnp.concatenate([dt, eac, dst], axis=0))[:2]))
    dt_e, eac_e, dst_e = e3[:CHUNK], e3[CHUNK:2 * CHUNK], e3[2 * CHUNK:]
    xg = xs * dt_e
    xg_b = xg.astype(BF16)
    xgd_b = (xg * dst_e).astype(BF16)
    acum_t = acum.T
    row = lax.broadcasted_iota(jnp.int32, (CHUNK, CHUNK), 0)
    col = lax.broadcasted_iota(jnp.int32, (CHUNK, CHUNK), 1)
    keep = (row >= col) if fwd else (row <= col)
    state_b = state.astype(BF16)
    gw = SSM_INNER // SSM_GROUPS
    ys, st_new = [], []
    for g in range(SSM_GROUPS):
        bmg = bm[:, g * SSM_STATE:(g + 1) * SSM_STATE]
        cmg = cm[:, g * SSM_STATE:(g + 1) * SSM_STATE]
        cb = _dot_nt(cmg, bmg)
        yoff = _dot(cmg, state_b[:, g * gw:(g + 1) * gw])
        st_new.append(_dot_tn(bmg, xgd_b[:, g * gw:(g + 1) * gw]))
        for pb in range(gw // LANES):
            blk = g * (gw // LANES) + pb
            sl = slice(blk * LANES, (blk + 1) * LANES)
            xgb = xg_b[:, sl]
            yd = None
            for hh in range(2):
                h = blk * 2 + hh
                seg = acum[:, h:h + 1] - acum_t[h:h + 1, :]
                ld = jnp.exp(jnp.where(keep, seg, -jnp.inf))
                t = _dot((cb * ld).astype(BF16), jnp.where(_half_mask(hh), xgb, jnp.zeros_like(xgb)))
                yd = t if yd is None else yd + t
            ys.append(yd + yoff[:, pb * LANES:(pb + 1) * LANES] * eac_e[:, sl])
    y = jnp.concatenate(ys, axis=1)
    return y, state * eac_e[last:last + 1] + jnp.concatenate(st_new, axis=1)


def _ssd_kernel(uf_ref, ub_ref, dtf_ref, dtb_ref, bias_ref, alog_ref, tri_ref, exp_ref, yf_ref, yb_ref, h_scr):
    @pl.when(pl.program_id(0) == 0)
    def _():
        h_scr[...] = jnp.zeros(h_scr.shape, F32)

    for b in range(uf_ref.shape[0]):
        for d, (u_ref, dt_ref, y_ref) in enumerate(((uf_ref, dtf_ref, yf_ref), (ub_ref, dtb_ref, yb_ref))):
            y, st = _ssd_chain(u_ref[b], dt_ref[b], bias_ref[d], alog_ref[d], tri_ref[d], exp_ref[...],
                               h_scr[b, d], fwd=(d == 0))
            y_ref[b] = y
            h_scr[b, d] = st


def _ssd(u, dtr, dt_bias, a_log, N):
    B, T, _ = u.shape
    nc, nlat = T // CHUNK, N // CHUNK
    ncx = nc - nlat
    chunk_f = lambda c: jnp.where(c < ncx, nlat + c, c - ncx)
    chunk_b = lambda c: jnp.where(c < ncx, nlat + ncx - 1 - c, nlat - 1 - (c - ncx))
    pad = lambda t: jnp.zeros((2, 1, LANES), F32).at[:, 0, :SSM_HEADS].set(t.astype(F32))
    li = np.arange(CHUNK)
    tri = np.stack([li[:, None] >= li[None, :], li[:, None] <= li[None, :]]).astype(np.float32)
    expm = np.zeros((LANES, SSM_INNER), np.float32)
    for h in range(SSM_HEADS):
        expm[h, h * 64:(h + 1) * 64] = 1.0
    full = lambda shape: pl.BlockSpec(shape, lambda c: (0,) * len(shape))
    y_shape = jax.ShapeDtypeStruct((B, T, SSM_INNER), F32)
    return pl.pallas_call(
        _ssd_kernel,
        grid=(nc,),
        in_specs=[pl.BlockSpec((B, CHUNK, SSM_XBC), lambda c: (0, chunk_f(c), 0)),
                  pl.BlockSpec((B, CHUNK, SSM_XBC), lambda c: (0, chunk_b(c), 0)),
                  pl.BlockSpec((B, CHUNK, LANES), lambda c: (0, chunk_f(c), 0)),
                  pl.BlockSpec((B, CHUNK, LANES), lambda c: (0, chunk_b(c), 1)),
                  full((2, 1, LANES)), full((2, 1, LANES)), full((2, CHUNK, CHUNK)), full((LANES, SSM_INNER))],
        out_specs=[pl.BlockSpec((B, CHUNK, SSM_INNER), lambda c: (0, chunk_f(c), 0)),
                   pl.BlockSpec((B, CHUNK, SSM_INNER), lambda c: (0, chunk_b(c), 0))],
        out_shape=[y_shape, y_shape],
        scratch_shapes=[pltpu.VMEM((B, 2, SSM_STATE, SSM_INNER), F32)],
        compiler_params=_cparams(("arbitrary",)),
        name="ssd_scan",
    )(u, u, dtr, dtr, pad(dt_bias), pad(a_log), jnp.asarray(tri, BF16), jnp.asarray(expm, BF16))


def _ssm_finish_kernel(yf_ref, yb_ref, u_ref, z_ref, dsk_ref, g_ref, o_ref):
    y = yf_ref[0] + yb_ref[0] + dsk_ref[...] * u_ref[0].astype(F32)
    z = z_ref[0].astype(F32)
    o_ref[0] = _rms(y * (z * jax.nn.sigmoid(z)), g_ref[...]).astype(BF16)


def _ssm_finish(y_f, y_b, u, p_plain, d_skip, g_ssm, *, tm=256):
    B, T, W = y_f.shape
    dsk = jnp.repeat(d_skip.astype(F32), W // SSM_HEADS).reshape(1, W)
    row_spec = pl.BlockSpec((1, tm, W), lambda b, i: (b, i, 0))
    return pl.pallas_call(
        _ssm_finish_kernel,
        grid=(B, T // tm),
        in_specs=[row_spec, row_spec, row_spec,
                  pl.BlockSpec((1, tm, W), lambda b, i: (b, i, P_BZ)),
                  pl.BlockSpec((1, W), lambda b, i: (0, 0)),
                  pl.BlockSpec((1, W), lambda b, i: (0, 0))],
        out_specs=row_spec,
        out_shape=jax.ShapeDtypeStruct((B, T, W), BF16),
        compiler_params=_cparams(("arbitrary", "arbitrary")),
        name="ssm_finish",
    )(y_f, y_b, u, p_plain, dsk, g_ssm.reshape(1, W).astype(F32))


def _merge_kernel(y0, y1, y2, y3, s0, s1, s2, s3, wb_ref, wo_ref, x_ref, gate_ref, g_ref, o_ref):
    m = None
    for k, (y, s) in enumerate(((y0, s0), (y1, s1), (y2, s2), (y3, s3))):
        t = _dot(y[0], wb_ref[k]) * s[0].astype(F32)
        m = t if m is None else m + t
    o = _dot(m.astype(BF16), wo_ref[...])
    o_ref[0] = x_ref[0] + gate_ref[0] * _rms(o, g_ref[...])


MERGE_TM = 256


def _merge(ya, yb_all, ys, yn, p_gate, w_branch, w_out, l, x, gate, g_post, *, row_off):
    B, Tx, D = x.shape
    tm = MERGE_TM
    assert Tx % tm == 0 and row_off % tm == 0
    off = row_off // tm
    y_spec = pl.BlockSpec((1, tm, BRANCH_W), lambda b, i: (b, i, 0))
    yb_spec = pl.BlockSpec((1, tm, BRANCH_W), lambda b, i: (b, off + i, 0))
    s_spec = lambda k: pl.BlockSpec((1, tm, D), lambda b, i: (b, off + i, k))
    resident = lambda shape: pl.BlockSpec((None,) + shape, lambda b, i: (l,) + (0,) * len(shape),
                                          pipeline_mode=pl.Buffered(1))
    return pl.pallas_call(
        _merge_kernel,
        grid=(B, Tx // tm),
        in_specs=[y_spec, yb_spec, y_spec, y_spec] + [s_spec(k) for k in range(N_BRANCH)]
                 + [resident((N_BRANCH, BRANCH_W, D)), resident((D, D)),
                    pl.BlockSpec((1, tm, D), lambda b, i: (b, i, 0)),
                    pl.BlockSpec((1, 1, D), lambda b, i: (b, 0, 0)),
                    pl.BlockSpec((1, D), lambda b, i: (0, 0))],
        out_specs=pl.BlockSpec((1, tm, D), lambda b, i: (b, i, 0)),
        out_shape=jax.ShapeDtypeStruct((B, Tx, D), F32),
        compiler_params=_cparams(("arbitrary", "arbitrary")),
        name="merge_out_proj",
    )(ya, yb_all, ys, yn, p_gate, p_gate, p_gate, p_gate, w_branch, w_out, x, gate, g_post)


def _ffn_kernel(x_ref, g1_ref, sh_ref, sc_ref, gate_ref, g2_ref, w1_ref, w2_ref, o_ref, h_scr, acc):
    f = pl.program_id(2)

    @pl.when(f == 0)
    def _():
        h = _rms(x_ref[0], g1_ref[...]) * (1.0 + sc_ref[0]) + sh_ref[0]
        h_scr[...] = h.astype(BF16)
        acc[...] = jnp.zeros(acc.shape, F32)

    a = jnp.square(jnp.maximum(_dot(h_scr[...], w1_ref[...]), 0.0)).astype(BF16)
    acc[...] += _dot(a, w2_ref[...])

    @pl.when(f == pl.num_programs(2) - 1)
    def _():
        o_ref[0] = x_ref[0] + gate_ref[0] * _rms(acc[...], g2_ref[...])


def _ffn(x, g_pre, shift, scale, gate, g_post, w1, w2, l, *, tm, tf=1024):
    B, T, D = x.shape
    F = w1.shape[2]
    tm = min(tm, T)
    mod_spec = pl.BlockSpec((1, 1, D), lambda b, i, f: (b, 0, 0))
    vec_spec = pl.BlockSpec((1, D), lambda b, i, f: (0, 0))
    return pl.pallas_call(
        _ffn_kernel,
        grid=(B, T // tm, F // tf),
        in_specs=[pl.BlockSpec((1, tm, D), lambda b, i, f: (b, i, 0)),
                  vec_spec, mod_spec, mod_spec, mod_spec, vec_spec,
                  pl.BlockSpec((None, D, tf), lambda b, i, f: (l, 0, f)),
                  pl.BlockSpec((None, tf, D), lambda b, i, f: (l, f, 0))],
        out_specs=pl.BlockSpec((1, tm, D), lambda b, i, f: (b, i, 0)),
        out_shape=jax.ShapeDtypeStruct((B, T, D), F32),
        scratch_shapes=[pltpu.VMEM((tm, D), BF16), pltpu.VMEM((tm, D), F32)],
        compiler_params=_cparams(("arbitrary", "arbitrary", "arbitrary")),
        name="mlp",
    )(x, g_pre, shift, scale, gate, g_post, w1, w2)


def _layout_w_in(w):
    depth, D, _ = w.shape
    bounds, o = {}, 0
    for name, width in (("aq", 512), ("ak", 512), ("av", 512), ("bz", 512), ("bx", SSM_XBC), ("bdt", 16),
                        ("sq", 512), ("sk", 128), ("sv", 128), ("nq", 512), ("nk", 512), ("nv", 512),
                        ("gate", N_BRANCH * D)):
        bounds[name] = (o, o + width)
        o += width
    assert o == w.shape[2]
    w_t = jnp.swapaxes(w, 1, 2)
    part = lambda name: w_t[:, bounds[name][0]:bounds[name][1]]
    qpart = lambda name: part(name) * (QSCALE * LOG2E)
    sq = qpart("sq").reshape(depth, SW_KV, 4, HEAD_DIM, D).transpose(0, 2, 1, 3, 4).reshape(depth, 512, D)
    zpad = jnp.zeros((depth, LANES - SSM_HEADS, D), w.dtype)
    bdt = part("bdt")
    misc = jnp.concatenate([part("sk"), part("sv"), bdt[:, :SSM_HEADS], zpad, bdt[:, SSM_HEADS:], zpad], axis=1)
    w_rope = jnp.concatenate([qpart("aq"), part("ak"), sq, misc], axis=1)
    w_plain = jnp.concatenate([part("bx"), qpart("nq"), part("nk"), part("nv"), part("av"), part("bz")], axis=1)
    return w_rope.astype(BF16), w_plain.astype(BF16), part("gate").astype(BF16)


def _rope_tables(n, n_ctx):
    t = np.arange(n)
    nf = HEAD_DIM // 4
    inv = ROPE_BASE ** (-np.arange(nf, dtype=np.float32) / nf)
    row = (t // GRID_W).astype(np.float32)
    col = (t % GRID_W).astype(np.float32)
    ang = jnp.asarray(np.concatenate([row[:, None] * inv, col[:, None] * inv], -1).astype(np.float32))
    cos, sin = jnp.cos(ang), jnp.sin(ang)
    cos = jnp.concatenate([cos, jnp.ones((n_ctx, cos.shape[1]), F32)], axis=0)
    sin = jnp.concatenate([sin, jnp.zeros((n_ctx, sin.shape[1]), F32)], axis=0)
    z = jnp.zeros_like(sin)
    rep = LANES // HEAD_DIM
    cos_t = jnp.tile(jnp.concatenate([cos, cos], -1), (1, rep))
    s1 = jnp.tile(jnp.concatenate([-sin, z], -1), (1, rep))
    s2 = jnp.tile(jnp.concatenate([z, sin], -1), (1, rep))
    return cos_t, s1, s2


def kernel(x, c, ctx, c_ctx, w_mod, b_mod, g_pre_mix, g_post_mix, g_pre_mlp, g_post_mlp, w_in, lam_q1, lam_k1, lam_q2, lam_k2, g_subln, conv_w, conv_b, dt_bias, a_log, d_skip, g_ssm, sink, rpb, w_branch, w_out, w_ff1, w_ff2):
    B, N, D = x.shape
    L = ctx.shape[1]
    T = N + L
    depth = w_mod.shape[0]
    assert B + 1 <= 8 and N % GRID_W == 0 and N % L == 0
    R = N // GRID_W

    cvec = jnp.zeros((8, D), F32).at[:B].set(c).at[B].set(c_ctx)
    mods = _modulation(cvec, w_mod, b_mod)
    rope_tabs = _rope_tables(N, L)
    na_bias = _na_bias(_na_col_bias(rpb), R, L)
    no_sink = jnp.full((NA_HEADS,), NEG, F32)
    row = lambda v: v.reshape(1, -1).astype(F32)
    tk_lat = _largest_divisor(T, (768, 512, 256, 128))
    w_rope, w_plain, w_gate = _layout_w_in(w_in)
    wb = w_branch.at[:, 2].set(w_branch[:, 2].reshape(depth, SW_KV, 4, HEAD_DIM, D).transpose(0, 2, 1, 3, 4)
                               .reshape(depth, BRANCH_W, D)).astype(BF16)
    wo, w1, w2 = w_out.astype(BF16), w_ff1.astype(BF16), w_ff2.astype(BF16)

    cx = ctx
    for l in range(depth):
        need_ctx = l < depth - 1
        lam_init = 0.8 - 0.6 * math.exp(-0.3 * l)
        mod = mods[l, :B].reshape(B, 6, 1, D)
        modc = jnp.broadcast_to(mods[l, B].reshape(1, 6, 1, D), (B, 6, 1, D))
        both = lambda k: jnp.stack([mod[:, k], modc[:, k]], axis=1).reshape(B * 2, 1, D)

        h = _prenorm(x, cx, row(g_pre_mix[l]), both(0), both(1))
        p_rope, dtr = _proj_rope(h, rope_tabs, w_rope, l)
        p_plain = _proj(h, w_plain, l, gate=False, tn=TN)
        p_gate = _proj(h, w_gate, l, gate=True, tn=2048)

        lamv = jnp.stack([lam_q1[l], lam_k1[l], lam_q2[l], lam_k2[l]]).astype(F32)
        g_col = g_subln[l].reshape(LANES, 1).astype(F32)
        ya = _diff_attn(p_rope, p_plain, lamv, g_col, lam_init, q_row0=0, n_q=N, k_row0=0, n_k=T, tq=512, tk=tk_lat)

        u = _conv_silu(p_plain, conv_w[l], conv_b[l], N)
        y_f, y_b = _ssd(u, dtr, dt_bias[l], a_log[l], N)
        yb_all = _ssm_finish(y_f, y_b, u, p_plain, d_skip[l], g_ssm[l])

        ys = _swa(p_rope, sink[l].astype(F32), N, L)
        yn = _na(p_plain, na_bias[l], N, L)

        x_new = _merge(ya, yb_all, ys, yn, p_gate, wb, wo, l, x, mod[:, 2], row(g_post_mix[l]), row_off=0)
        x_new = _ffn(x_new, row(g_pre_mlp[l]), mod[:, 3], mod[:, 4], mod[:, 5], row(g_post_mlp[l]), w1, w2, l,
                     tm=512)
        if need_ctx:
            ya_c = _diff_attn(p_rope, p_plain, lamv, g_col, lam_init, q_row0=N, n_q=L, k_row0=N, n_k=L, tq=L, tk=L)
            ys_c = _ctx_attend(p_rope, sink[l].astype(F32), R_SQ, R_MISC * 4, R_MISC * 4 + 1, N, L, gqa=True)
            yn_c = _ctx_attend(p_plain, no_sink, P_NQ, P_NK, P_NV, N, L, gqa=False)
            cx = _merge(ya_c, yb_all, ys_c, yn_c, p_gate, wb, wo, l, cx, modc[:, 2], row(g_post_mix[l]), row_off=N)
            cx = _ffn(cx, row(g_pre_mlp[l]), modc[:, 3], modc[:, 4], modc[:, 5], row(g_post_mlp[l]), w1, w2, l,
                      tm=L)
        x = x_new
    return x
```

```python
import functools
import math

import numpy as np
import jax
import jax.numpy as jnp
from jax import lax
from jax.experimental import pallas as pl
from jax.experimental.pallas import tpu as pltpu

F32 = jnp.float32
BF16 = jnp.bfloat16

LANES = 128
VMEM_LIMIT = 56 * 1024 * 1024
NEG = -1e30
EPS = 1e-6
GRID_W = 64
ROPE_BASE = 10000.0
HEAD_DIM = 64
QSCALE = HEAD_DIM ** -0.5
LOG2E = math.log2(math.e)
DA_HEADS = 4
SSM_HEADS = 8
SSM_INNER = 512
SSM_GROUPS = 2
SSM_STATE = 128
SSM_XBC = SSM_INNER + 2 * SSM_GROUPS * SSM_STATE
SSM_CONV = 5
CHUNK = 128
SW_KV = 2
SW_BLOCK = 128
NA_HEADS = 8
NA_ROWS = 8
NA_COLS = 16
N_BRANCH = 4
BRANCH_W = 512

TN = 512
R_AQ, R_AK, R_SQ, R_MISC = 0, 1, 2, 3
P_NQ, P_NK, P_NV, P_AV, P_BZ = 2, 3, 4, 5, 6


def _cparams(sem):
    return pltpu.CompilerParams(dimension_semantics=sem, vmem_limit_bytes=VMEM_LIMIT)


def _rms(t, g):
    return t * lax.rsqrt(jnp.mean(t * t, -1, keepdims=True) + EPS) * g


def _dot(a, b):
    return jnp.dot(a, b, preferred_element_type=F32)


def _dot_nt(a, b):
    return lax.dot_general(a, b, (((1,), (1,)), ((), ())), preferred_element_type=F32)


def _dot_tn(a, b):
    return lax.dot_general(a, b, (((0,), (0,)), ((), ())), preferred_element_type=F32)


def _half_mask(hh):
    lane = lax.broadcasted_iota(jnp.int32, (1, LANES), 1)
    return (lane // HEAD_DIM) == hh


def _v_with_ones(v, hh):
    lane = lax.broadcasted_iota(jnp.int32, (1, LANES), 1)
    ones_lane = jnp.where(lane == (1 - hh) * HEAD_DIM, 1.0, 0.0).astype(v.dtype)
    return jnp.where(_half_mask(hh), v, jnp.zeros_like(v)) + ones_lane


def _split_pv(o, hh):
    spare = (1 - hh) * HEAD_DIM
    return jnp.where(_half_mask(hh), o, 0.0), o[:, spare:spare + 1]


def _largest_divisor(n, cands):
    for c in cands:
        if n % c == 0:
            return c
    raise ValueError(f"no tile in {cands} divides {n}")


MOD_KB = 256


def _mod_kernel(c_ref, wa_ref, wb_ref, b_ref, o_ref):
    k = pl.program_id(1)
    c = c_ref[...]
    s = (c * jax.nn.sigmoid(c)).astype(BF16)
    half = MOD_KB // 2
    part = _dot(s[:, :half], wa_ref[0].astype(BF16)) + _dot(s[:, half:], wb_ref[0].astype(BF16))

    @pl.when(k == 0)
    def _():
        o_ref[0] = part + b_ref[0]

    @pl.when(k > 0)
    def _():
        o_ref[0] += part


def _modulation(cvec, w_mod, b_mod):
    depth, D, C = w_mod.shape
    return pl.pallas_call(
        _mod_kernel,
        grid=(depth, D // MOD_KB),
        in_specs=[pl.BlockSpec((8, MOD_KB), lambda l, k: (0, k)),
                  pl.BlockSpec((1, MOD_KB // 2, C), lambda l, k: (l, 2 * k, 0)),
                  pl.BlockSpec((1, MOD_KB // 2, C), lambda l, k: (l, 2 * k + 1, 0)),
                  pl.BlockSpec((1, 1, C), lambda l, k: (l, 0, 0))],
        out_specs=pl.BlockSpec((1, 8, C), lambda l, k: (l, 0, 0)),
        out_shape=jax.ShapeDtypeStruct((depth, 8, C), F32),
        compiler_params=_cparams(("arbitrary", "arbitrary")),
        name="modulation",
    )(cvec, w_mod, w_mod, b_mod.reshape(depth, 1, C))


NORM_BLK = 256


def _prenorm_kernel(x_ref, cx_ref, g_ref, sh_ref, sc_ref, o_ref, *, nlat):
    i = pl.program_id(1)

    def emit(t):
        o_ref[0] = (_rms(t, g_ref[...]) * (1.0 + sc_ref[0]) + sh_ref[0]).astype(BF16)

    @pl.when(i < nlat)
    def _():
        emit(x_ref[0])

    @pl.when(i >= nlat)
    def _():
        emit(cx_ref[0])


def _prenorm(x, cx, g, shift2, scale2):
    B, N, D = x.shape
    L = cx.shape[1]
    assert N % NORM_BLK == 0 and L % NORM_BLK == 0
    nlat, nctx = N // NORM_BLK, L // NORM_BLK
    mod_spec = pl.BlockSpec((1, 1, D), lambda b, i: (b * 2 + (i >= nlat).astype(jnp.int32), 0, 0))
    return pl.pallas_call(
        functools.partial(_prenorm_kernel, nlat=nlat),
        grid=(B, nlat + nctx),
        in_specs=[pl.BlockSpec((1, NORM_BLK, D), lambda b, i: (b, jnp.minimum(i, nlat - 1), 0)),
                  pl.BlockSpec((1, NORM_BLK, D), lambda b, i: (b, jnp.maximum(i - nlat, 0), 0)),
                  pl.BlockSpec((1, D), lambda b, i: (0, 0)),
                  mod_spec, mod_spec],
        out_specs=pl.BlockSpec((1, NORM_BLK, D), lambda b, i: (b, i, 0)),
        out_shape=jax.ShapeDtypeStruct((B, N + L, D), BF16),
        compiler_params=_cparams(("arbitrary", "arbitrary")),
        name="prenorm",
    )(x, cx, g, shift2, scale2)


def _rope(t, cos, s1, s2):
    w = t.shape[1]
    reps = w // LANES
    tile = lambda a: a if reps == 1 else jnp.concatenate([a] * reps, axis=1)
    return (t * tile(cos) + pltpu.roll(t, w - HEAD_DIM // 2, 1) * tile(s1)
            + pltpu.roll(t, HEAD_DIM // 2, 1) * tile(s2))


def _proj_rope_kernel(h_ref, cos_ref, s1_ref, s2_ref, w_ref, o_ref, odt_ref):
    j = pl.program_id(2)
    acc = _dot_nt(h_ref[0], w_ref[...])
    rot = lambda t: _rope(t, cos_ref[...], s1_ref[...], s2_ref[...])

    @pl.when(j != R_MISC)
    def _():
        o_ref[0] = rot(acc).astype(BF16)

    @pl.when(j == R_MISC)
    def _():
        o_ref[0] = jnp.concatenate([rot(acc[:, :LANES]), acc[:, LANES:]], axis=1).astype(BF16)
        odt_ref[0] = acc[:, 2 * LANES:]


def _row_tile(T):
    return _largest_divisor(T, (1056, 1024, 768, 640, 512, 256, 128))


def _proj_rope(h, rope_tabs, w, l):
    B, T, D = h.shape
    C = w.shape[1]
    tm = _row_tile(T)
    cos, s1, s2 = rope_tabs
    tab_spec = pl.BlockSpec((tm, LANES), lambda b, i, j: (i, 0))
    return pl.pallas_call(
        _proj_rope_kernel,
        grid=(B, T // tm, C // TN),
        in_specs=[pl.BlockSpec((1, tm, D), lambda b, i, j: (b, i, 0)),
                  tab_spec, tab_spec, tab_spec,
                  pl.BlockSpec((None, TN, D), lambda b, i, j: (l, j, 0))],
        out_specs=[pl.BlockSpec((1, tm, TN), lambda b, i, j: (b, i, j)),
                   pl.BlockSpec((1, tm, 2 * LANES), lambda b, i, j: (b, i, 0))],
        out_shape=[jax.ShapeDtypeStruct((B, T, C), BF16),
                   jax.ShapeDtypeStruct((B, T, 2 * LANES), F32)],
        compiler_params=_cparams(("arbitrary", "arbitrary", "arbitrary")),
        name="proj_rope",
    )(h, cos, s1, s2, w)


def _proj_kernel(h_ref, w_ref, o_ref, *, gate):
    acc = _dot_nt(h_ref[0], w_ref[...])
    o_ref[0] = (jax.nn.sigmoid(acc) if gate else acc).astype(BF16)


def _proj(h, w, l, *, gate, tn, rows=None, w_row0=None, n_cols=None):
    B, _, D = h.shape
    T = h.shape[1] if rows is None else rows
    C = w.shape[1] if w_row0 is None else n_cols
    tm = _row_tile(T)
    if w_row0 is None:
        w_spec = pl.BlockSpec((None, tn, D), lambda b, i, j: (l, j, 0))
    else:
        assert w_row0 % 16 == 0 and tn % 16 == 0
        w_spec = pl.BlockSpec((None, pl.Element(tn), pl.Element(D)),
                              lambda b, i, j: (l, pl.multiple_of(w_row0 + j * tn, 16), 0))
    return pl.pallas_call(
        functools.partial(_proj_kernel, gate=gate),
        grid=(B, T // tm, C // tn),
        in_specs=[pl.BlockSpec((1, tm, D), lambda b, i, j: (b, i, 0)), w_spec],
        out_specs=pl.BlockSpec((1, tm, tn), lambda b, i, j: (b, i, j)),
        out_shape=jax.ShapeDtypeStruct((B, T, C), BF16),
        compiler_params=_cparams(("arbitrary", "arbitrary", "arbitrary")),
        name="proj_gate" if gate else "proj_plain",
    )(h, w)


VT_ROWS = LANES + 16


def _diff_attn_kernel(lam_ref, g_ref, q_ref, k_ref, v_ref, o_ref, s_scr, vt_scr, *, lam_init, tk):
    q = q_ref[0]
    tq = q.shape[0]
    nk = k_ref.shape[1] // tk
    q_m = [jnp.where(_half_mask(m), q, jnp.zeros_like(q)) for m in range(2)]

    @pl.when(pl.program_id(2) == 0)
    def _():
        ones_row = (lax.broadcasted_iota(jnp.int32, (VT_ROWS - LANES, tk), 0) == 0).astype(BF16)
        for c in range(nk):
            v_t = v_ref[0, c * tk:(c + 1) * tk, :].astype(F32).T.astype(BF16)
            vt_scr[c] = jnp.concatenate([v_t, ones_row], axis=0)

    def scores(c):
        kc = k_ref[0, c * tk:(c + 1) * tk, :]
        for m in range(2):
            s_scr[c % 2, m] = _dot_nt(kc, q_m[m])

    carry = [(jnp.full((1, tq), -jnp.inf, F32), jnp.zeros((VT_ROWS, tq), F32)) for _ in range(2)]
    scores(0)
    for c in range(nk):
        if c + 1 < nk:
            scores(c + 1)
        vc = vt_scr[c]
        for m in range(2):
            mx, acc = carry[m]
            m_new = jnp.maximum(mx, jnp.max(s_scr[c % 2, m], 0, keepdims=True))
            p = jnp.exp2((s_scr[c % 2, m] - m_new).astype(BF16))
            acc = jnp.exp2(mx - m_new) * acc + _dot(vc, p)
            carry[m] = (m_new, acc)
    (_, a0), (_, a1) = carry
    lv = lam_ref[...]
    lam = (jnp.exp(jnp.sum(lv[0:1] * lv[1:2], -1, keepdims=True))
           - jnp.exp(jnp.sum(lv[2:3] * lv[3:4], -1, keepdims=True)) + lam_init)
    o = a0[:LANES] / a0[LANES:LANES + 1] - lam * (a1[:LANES] / a1[LANES:LANES + 1])
    o = o * lax.rsqrt(jnp.mean(o * o, 0, keepdims=True) + EPS) * g_ref[...] * (1.0 - lam_init)
    o_ref[0] = o.T.astype(BF16)


def _diff_attn(p_rope, p_plain, lamv, g_col, lam_init, *, q_row0, n_q, k_row0, n_k, tq, tk):
    B = p_rope.shape[0]
    tq = min(tq, n_q)
    assert n_q % tq == 0 and q_row0 % tq == 0 and k_row0 % n_k == 0 and n_k % tk == 0
    qb, kb = q_row0 // tq, k_row0 // n_k
    return pl.pallas_call(
        functools.partial(_diff_attn_kernel, lam_init=lam_init, tk=tk),
        grid=(B, DA_HEADS, n_q // tq),
        in_specs=[pl.BlockSpec((4, HEAD_DIM), lambda b, h, i: (0, 0)),
                  pl.BlockSpec((LANES, 1), lambda b, h, i: (0, 0)),
                  pl.BlockSpec((1, tq, LANES), lambda b, h, i: (b, qb + i, R_AQ * 4 + h)),
                  pl.BlockSpec((1, n_k, LANES), lambda b, h, i: (b, kb, R_AK * 4 + h)),
                  pl.BlockSpec((1, n_k, LANES), lambda b, h, i: (b, kb, P_AV * 4 + h))],
        out_specs=pl.BlockSpec((1, tq, LANES), lambda b, h, i: (b, i, h)),
        out_shape=jax.ShapeDtypeStruct((B, n_q, DA_HEADS * LANES), BF16),
        scratch_shapes=[pltpu.VMEM((2, 2, tk, tq), F32),
                        pltpu.VMEM((n_k // tk, VT_ROWS, tk), BF16)],
        compiler_params=_cparams(("arbitrary",) * 3),
        name="diff_attn",
    )(lamv, g_col, p_rope, p_rope, p_plain)


def _swa_kernel(sink_ref, q_ref, kp_ref, kc_ref, kn_ref, vp_ref, vc_ref, vn_ref, kx_ref, vx_ref, o_ref):
    i = pl.program_id(1)
    nb = pl.num_programs(1)
    blk = SW_BLOCK
    kband = jnp.concatenate([kp_ref[0], kc_ref[0], kn_ref[0], kx_ref[0]], axis=0)
    vband = jnp.concatenate([vp_ref[0], vc_ref[0], vn_ref[0], vx_ref[0]], axis=0)
    nkeys = kband.shape[0]
    row = lax.broadcasted_iota(jnp.int32, (blk, nkeys), 0)
    col = lax.broadcasted_iota(jnp.int32, (blk, nkeys), 1)
    big = 4 * blk
    off_prev = jnp.where(i > 0, 0, big)
    off_next = jnp.where(i < nb - 1, 0, big)
    ok_prev = jnp.where(col >= row + off_prev, 0.0, NEG)
    ok_next = jnp.where(col - 2 * blk <= row - off_next, 0.0, NEG)
    bias = jnp.where(col < blk, ok_prev, jnp.where((col >= 2 * blk) & (col < 3 * blk), ok_next, 0.0))
    bias = jnp.concatenate([bias] * 4, axis=0)
    q = q_ref[0]
    outs = [None] * 4

    def scores(kv):
        qs = jnp.concatenate([jnp.where(_half_mask(kv), q[:, g * LANES:(g + 1) * LANES],
                                        jnp.zeros((blk, LANES), BF16)) for g in range(4)], axis=0)
        return _dot_nt(qs, kband) + bias

    s_all = [scores(kv) for kv in range(SW_KV)]
    for kv in range(SW_KV):
        s = s_all[kv]
        sink = jnp.concatenate([jnp.full((blk, 1), sink_ref[kv * 4 + g] * LOG2E, F32) for g in range(4)], axis=0)
        m = jnp.maximum(jnp.max(s, -1, keepdims=True), sink)
        p = jnp.exp2((s - m).astype(BF16))
        pv, l = _split_pv(_dot(p, _v_with_ones(vband, kv)), kv)
        o = pv / (l + jnp.exp2(sink - m))
        for g in range(4):
            part = o[g * blk:(g + 1) * blk]
            outs[g] = part if outs[g] is None else outs[g] + part
    o_ref[0] = jnp.concatenate(outs, axis=1).astype(BF16)


def _swa(p_rope, sink, N, L):
    B = p_rope.shape[0]
    nb = N // SW_BLOCK
    kcol, vcol = R_MISC * 4, R_MISC * 4 + 1
    kv_spec = lambda colblk, off: pl.BlockSpec(
        (1, SW_BLOCK, LANES), lambda b, i: (b, jnp.clip(i + off, 0, nb - 1), colblk))
    return pl.pallas_call(
        _swa_kernel,
        grid=(B, nb),
        in_specs=[pl.BlockSpec(memory_space=pltpu.SMEM),
                  pl.BlockSpec((1, SW_BLOCK, TN), lambda b, i: (b, i, R_SQ)),
                  kv_spec(kcol, -1), kv_spec(kcol, 0), kv_spec(kcol, 1),
                  kv_spec(vcol, -1), kv_spec(vcol, 0), kv_spec(vcol, 1),
                  pl.BlockSpec((1, L, LANES), lambda b, i: (b, N // L, kcol)),
                  pl.BlockSpec((1, L, LANES), lambda b, i: (b, N // L, vcol))],
        out_specs=pl.BlockSpec((1, SW_BLOCK, TN), lambda b, i: (b, i, 0)),
        out_shape=jax.ShapeDtypeStruct((B, N, TN), BF16),
        compiler_params=_cparams(("arbitrary", "arbitrary")),
        name="window_attn",
    )(sink, *([p_rope] * 9))


NA_QROWS = 2
NA_KBLKS = 5


def _na_kernel(bias_ref, q_ref, k0, k1, k2, k3, k4, v0, v1, v2, v3, v4, kx_ref, vx_ref, o_ref):
    kall = jnp.concatenate([k0[0], k1[0], k2[0], k3[0], k4[0], kx_ref[0]], axis=0)
    vall = jnp.concatenate([v0[0], v1[0], v2[0], v3[0], v4[0], vx_ref[0]], axis=0)
    q = q_ref[0]
    blocks = [slice(pb * LANES, (pb + 1) * LANES) for pb in range(4)]
    no_bias = jnp.zeros((q.shape[0], kx_ref.shape[1]), F32)
    scores = [[_dot_nt(jnp.where(_half_mask(hh), q[:, sl], jnp.zeros((q.shape[0], LANES), BF16)), kall[:, sl])
               + jnp.concatenate([bias_ref[0, pb * 2 + hh], no_bias], axis=1)
               for hh in range(2)] for pb, sl in enumerate(blocks)]
    outs = []
    for pb, sl in enumerate(blocks):
        vb = vall[:, sl]
        ob = None
        for hh in range(2):
            s = scores[pb][hh]
            p = jnp.exp2((s - jnp.max(s, -1, keepdims=True)).astype(BF16))
            pv, l = _split_pv(_dot(p, _v_with_ones(vb, hh)), hh)
            o = pv / l
            ob = o if ob is None else ob + o
        outs.append(ob)
    o_ref[0] = jnp.concatenate(outs, axis=1).astype(BF16)


def _col_bias_kernel(r_ref, sel_ref, neg_ref, o_ref):
    o_ref[...] = sum(_dot(part, sel_ref[...]) for part in reversed(_split3(r_ref[...]))) * LOG2E + neg_ref[...]


def _na_col_bias(rpb_all):
    W = GRID_W
    nco = 2 * NA_COLS - 1
    lead = rpb_all.shape[:3]
    rows = int(np.prod(lead))
    c = np.arange(W)
    cb = np.clip(c - NA_COLS // 2, 0, W - NA_COLS)
    dc = c[None, :] - cb[:, None]
    inside = (dc >= 0) & (dc < NA_COLS)
    co = c[None, :] - c[:, None] + NA_COLS - 1
    sel = ((co[None] == np.arange(32)[:, None, None]) & inside[None]).reshape(32, W * W)
    neg = np.where(inside, 0.0, NEG).reshape(1, W * W).astype(np.float32)
    table = jnp.pad(rpb_all.astype(F32).reshape(rows, nco), ((0, (-rows) % 8), (0, 32 - nco)))
    out = pl.pallas_call(
        _col_bias_kernel,
        out_shape=jax.ShapeDtypeStruct((table.shape[0], W * W), F32),
        name="na_col_bias",
    )(table, jnp.asarray(sel, BF16), jnp.asarray(neg))
    return out[:rows].reshape(lead + (W, W))


def _na_bias(t2, R, L):
    W = GRID_W
    depth, H = t2.shape[:2]
    clamp = lambda v, lo, hi: min(max(v, lo), hi)
    nblk = R // NA_QROWS
    slots = NA_KBLKS * NA_QROWS
    per = []
    for i in (0, 1, 2, nblk - 2, nblk - 1):
        base_blk = clamp(i - 2, 0, nblk - NA_KBLKS)
        for t in range(NA_QROWS):
            rq = NA_QROWS * i + t
            base = clamp(rq - NA_ROWS // 2, 0, R - NA_ROWS)
            s_lo = base - NA_QROWS * base_blk
            assert 0 <= s_lo and s_lo + NA_ROWS <= slots
            ro = base - rq + NA_ROWS - 1
            per.append(jnp.pad(t2[:, :, ro:ro + NA_ROWS],
                               ((0, 0), (0, 0), (s_lo, slots - NA_ROWS - s_lo), (0, 0), (0, 0)),
                               constant_values=NEG))
    b = jnp.stack(per, axis=2).reshape(depth, H, 5, NA_QROWS, slots, W, W)
    return b.transpose(0, 2, 1, 3, 5, 4, 6).reshape(depth, 5, H, NA_QROWS * W, slots * W)


def _na(p_plain, bias, N, L):
    B = p_plain.shape[0]
    rows = NA_QROWS * GRID_W
    nblk = N // rows
    assert nblk >= NA_KBLKS + 2
    cls = lambda i: jnp.where(i < 2, i, jnp.where(i >= nblk - 2, i - (nblk - 5), 2))
    kv_spec = lambda tile, s: pl.BlockSpec(
        (1, rows, TN), lambda b, i: (b, jnp.clip(i - 2, 0, nblk - NA_KBLKS) + s, tile))
    return pl.pallas_call(
        _na_kernel,
        grid=(B, nblk),
        in_specs=[pl.BlockSpec((1, NA_HEADS, rows, NA_KBLKS * rows), lambda b, i: (cls(i), 0, 0, 0)),
                  pl.BlockSpec((1, rows, TN), lambda b, i: (b, i, P_NQ))]
                 + [kv_spec(P_NK, s) for s in range(NA_KBLKS)]
                 + [kv_spec(P_NV, s) for s in range(NA_KBLKS)]
                 + [pl.BlockSpec((1, L, TN), lambda b, i: (b, N // L, P_NK)),
                    pl.BlockSpec((1, L, TN), lambda b, i: (b, N // L, P_NV))],
        out_specs=pl.BlockSpec((1, rows, TN), lambda b, i: (b, i, 0)),
        out_shape=jax.ShapeDtypeStruct((B, N, TN), BF16),
        compiler_params=_cparams(("arbitrary", "arbitrary")),
        name="neighbourhood_attn",
    )(bias, *([p_plain] * (2 * NA_KBLKS + 3)))


def _ctx_attend_kernel(sink_ref, q_ref, k_ref, v_ref, o_ref, *, gqa):
    q = q_ref[0]
    k = k_ref[0]
    v = v_ref[0]
    outs = []
    for blk in range(4):
        sl = slice(blk * LANES, (blk + 1) * LANES)
        qb = q[:, sl]
        kb, vb = (k, v) if gqa else (k[:, sl], v[:, sl])
        ob = None
        for hh in range(2):
            head = hh * 4 + blk if gqa else blk * 2 + hh
            s = _dot_nt(jnp.where(_half_mask(hh), qb, jnp.zeros_like(qb)), kb)
            sink = sink_ref[head] * LOG2E
            m = jnp.maximum(jnp.max(s, -1, keepdims=True), sink)
            p = jnp.exp2((s - m).astype(BF16))
            pv, l = _split_pv(_dot(p, _v_with_ones(vb, hh)), hh)
            o = pv / (l + jnp.exp2(sink - m))
            ob = o if ob is None else ob + o
        outs.append(ob)
    o_ref[0] = jnp.concatenate(outs, axis=1).astype(BF16)


def _ctx_attend(p, sink, q_tile, k_blk, v_blk, N, L, *, gqa):
    B = p.shape[0]
    kw = LANES if gqa else TN
    rb = N // L
    return pl.pallas_call(
        functools.partial(_ctx_attend_kernel, gqa=gqa),
        grid=(B,),
        in_specs=[pl.BlockSpec(memory_space=pltpu.SMEM),
                  pl.BlockSpec((1, L, TN), lambda b: (b, rb, q_tile)),
                  pl.BlockSpec((1, L, kw), lambda b: (b, rb, k_blk)),
                  pl.BlockSpec((1, L, kw), lambda b: (b, rb, v_blk))],
        out_specs=pl.BlockSpec((1, L, TN), lambda b: (b, 0, 0)),
        out_shape=jax.ShapeDtypeStruct((B, L, TN), BF16),
        compiler_params=_cparams(("arbitrary",)),
        name="ctx_attend_gqa" if gqa else "ctx_attend",
    )(sink, p, p, p)


CONV_BLK = 256
HALO = 16


def _conv_kernel(prev_ref, cur_ref, next_ref, w_ref, b_ref, o_ref, *, lat_blocks):
    i = pl.program_id(1)
    last = pl.num_programs(1) - 1
    has_prev = (i != 0) & (i != lat_blocks)
    has_next = (i != lat_blocks - 1) & (i != last)
    prev = prev_ref[0].astype(F32) * has_prev.astype(F32)
    nxt = next_ref[0].astype(F32) * has_next.astype(F32)
    ext = jnp.concatenate([prev, cur_ref[0].astype(F32), nxt], axis=0)
    n = ext.shape[0]
    w = w_ref[...]
    acc = None
    for k in range(SSM_CONV):
        shift = (SSM_CONV // 2 - k) % n
        t = (ext if shift == 0 else pltpu.roll(ext, shift, 0))[HALO:HALO + CONV_BLK] * w[k:k + 1]
        acc = t if acc is None else acc + t
    acc = acc + b_ref[...]
    o_ref[0] = (acc * jax.nn.sigmoid(acc)).astype(BF16)


def _conv_silu(p_plain, conv_w, conv_b, N):
    B, T, _ = p_plain.shape
    C = SSM_XBC
    assert N % CONV_BLK == 0 and T % CONV_BLK == 0
    per = CONV_BLK // HALO
    nh = T // HALO
    w = jnp.zeros((8, C), F32).at[:SSM_CONV].set(conv_w.astype(F32))
    return pl.pallas_call(
        functools.partial(_conv_kernel, lat_blocks=N // CONV_BLK),
        grid=(B, T // CONV_BLK),
        in_specs=[pl.BlockSpec((1, HALO, C), lambda b, i: (b, jnp.maximum(i * per - 1, 0), 0)),
                  pl.BlockSpec((1, CONV_BLK, C), lambda b, i: (b, i, 0)),
                  pl.BlockSpec((1, HALO, C), lambda b, i: (b, jnp.minimum((i + 1) * per, nh - 1), 0)),
                  pl.BlockSpec((8, C), lambda b, i: (0, 0)),
                  pl.BlockSpec((1, C), lambda b, i: (0, 0))],
        out_specs=pl.BlockSpec((1, CONV_BLK, C), lambda b, i: (b, i, 0)),
        out_shape=jax.ShapeDtypeStruct((B, T, C), BF16),
        compiler_params=_cparams(("arbitrary", "arbitrary")),
        name="ssm_conv",
    )(p_plain, p_plain, p_plain, w, conv_b.reshape(1, C).astype(F32))


def _split3(t):
    hi = t.astype(BF16)
    r = t - hi.astype(F32)
    mid = r.astype(BF16)
    return hi, mid, (r - mid.astype(F32)).astype(BF16)


def _ssd_chain(u, dtr, dtb, alog, tri, expm, state, *, fwd):
    last = CHUNK - 1 if fwd else 0
    xs = u[:, :SSM_INNER].astype(F32)
    bm = u[:, SSM_INNER:SSM_INNER + SSM_GROUPS * SSM_STATE]
    cm = u[:, SSM_INNER + SSM_GROUPS * SSM_STATE:]
    dt = jax.nn.softplus(dtr + dtb)
    a = dt * (-jnp.exp(alog))
    acum = sum(_dot(tri, part) for part in reversed(_split3(a)))
    eac = jnp.exp(acum)
    dst = jnp.exp(acum[last:last + 1] - acum)
    e3 = sum(_dot(part, expm) for part in reversed(_split3(jnp.concatenate([dt, eac, dst], axis=0))[:2]))
    dt_e, eac_e, dst_e = e3[:CHUNK], e3[CHUNK:2 * CHUNK], e3[2 * CHUNK:]
    xg = xs * dt_e
    xg_b = xg.astype(BF16)
    xgd_b = (xg * dst_e).astype(BF16)
    acum_t = acum.T
    row = lax.broadcasted_iota(jnp.int32, (CHUNK, CHUNK), 0)
    col = lax.broadcasted_iota(jnp.int32, (CHUNK, CHUNK), 1)
    keep = (row >= col) if fwd else (row <= col)
    state_b = state.astype(BF16)
    gw = SSM_INNER // SSM_GROUPS
    ys, st_new = [], []
    for g in range(SSM_GROUPS):
        bmg = bm[:, g * SSM_STATE:(g + 1) * SSM_STATE]
        cmg = cm[:, g * SSM_STATE:(g + 1) * SSM_STATE]
        cb = _dot_nt(cmg, bmg)
        yoff = _dot(cmg, state_b[:, g * gw:(g + 1) * gw])
        st_new.append(_dot_tn(bmg, xgd_b[:, g * gw:(g + 1) * gw]))
        for pb in range(gw // LANES):
            blk = g * (gw // LANES) + pb
            sl = slice(blk * LANES, (blk + 1) * LANES)
            xgb = xg_b[:, sl]
            yd = None
            for hh in range(2):
                h = blk * 2 + hh
                seg = acum[:, h:h + 1] - acum_t[h:h + 1, :]
                ld = jnp.exp(jnp.where(keep, seg, -jnp.inf))
                t = _dot((cb * ld).astype(BF16), jnp.where(_half_mask(hh), xgb, jnp.zeros_like(xgb)))
                yd = t if yd is None else yd + t
            ys.append(yd + yoff[:, pb * LANES:(pb + 1) * LANES] * eac_e[:, sl])
    y = jnp.concatenate(ys, axis=1)
    return y, state * eac_e[last:last + 1] + jnp.concatenate(st_new, axis=1)


def _ssd_kernel(uf_ref, ub_ref, dtf_ref, dtb_ref, bias_ref, alog_ref, tri_ref, exp_ref, yf_ref, yb_ref, h_scr):
    @pl.when(pl.program_id(0) == 0)
    def _():
        h_scr[...] = jnp.zeros(h_scr.shape, F32)

    for b in range(uf_ref.shape[0]):
        for d, (u_ref, dt_ref, y_ref) in enumerate(((uf_ref, dtf_ref, yf_ref), (ub_ref, dtb_ref, yb_ref))):
            y, st = _ssd_chain(u_ref[b], dt_ref[b], bias_ref[d], alog_ref[d], tri_ref[d], exp_ref[...],
                               h_scr[b, d], fwd=(d == 0))
            y_ref[b] = y.astype(BF16)
            h_scr[b, d] = st


def _ssd(u, dtr, dt_bias, a_log, N):
    B, T, _ = u.shape
    nc, nlat = T // CHUNK, N // CHUNK
    ncx = nc - nlat
    chunk_f = lambda c: jnp.where(c < ncx, nlat + c, c - ncx)
    chunk_b = lambda c: jnp.where(c < ncx, nlat + ncx - 1 - c, nlat - 1 - (c - ncx))
    pad = lambda t: jnp.zeros((2, 1, LANES), F32).at[:, 0, :SSM_HEADS].set(t.astype(F32))
    li = np.arange(CHUNK)
    tri = np.stack([li[:, None] >= li[None, :], li[:, None] <= li[None, :]]).astype(np.float32)
    expm = np.zeros((LANES, SSM_INNER), np.float32)
    for h in range(SSM_HEADS):
        expm[h, h * 64:(h + 1) * 64] = 1.0
    full = lambda shape: pl.BlockSpec(shape, lambda c: (0,) * len(shape))
    y_shape = jax.ShapeDtypeStruct((B, T, SSM_INNER), BF16)
    return pl.pallas_call(
        _ssd_kernel,
        grid=(nc,),
        in_specs=[pl.BlockSpec((B, CHUNK, SSM_XBC), lambda c: (0, chunk_f(c), 0)),
                  pl.BlockSpec((B, CHUNK, SSM_XBC), lambda c: (0, chunk_b(c), 0)),
                  pl.BlockSpec((B, CHUNK, LANES), lambda c: (0, chunk_f(c), 0)),
                  pl.BlockSpec((B, CHUNK, LANES), lambda c: (0, chunk_b(c), 1)),
                  full((2, 1, LANES)), full((2, 1, LANES)), full((2, CHUNK, CHUNK)), full((LANES, SSM_INNER))],
        out_specs=[pl.BlockSpec((B, CHUNK, SSM_INNER), lambda c: (0, chunk_f(c), 0)),
                   pl.BlockSpec((B, CHUNK, SSM_INNER), lambda c: (0, chunk_b(c), 0))],
        out_shape=[y_shape, y_shape],
        scratch_shapes=[pltpu.VMEM((B, 2, SSM_STATE, SSM_INNER), F32)],
        compiler_params=_cparams(("arbitrary",)),
        name="ssd_scan",
    )(u, u, dtr, dtr, pad(dt_bias), pad(a_log), jnp.asarray(tri, BF16), jnp.asarray(expm, BF16))


def _ssm_finish_kernel(yf_ref, yb_ref, u_ref, z_ref, dsk_ref, g_ref, o_ref):
    y = yf_ref[0].astype(F32) + yb_ref[0].astype(F32) + dsk_ref[...] * u_ref[0].astype(F32)
    z = z_ref[0].astype(F32)
    o_ref[0] = _rms(y * (z * jax.nn.sigmoid(z)), g_ref[...]).astype(BF16)


def _ssm_finish(y_f, y_b, u, p_plain, d_skip, g_ssm, *, tm=256):
    B, T, W = y_f.shape
    dsk = jnp.repeat(d_skip.astype(F32), W // SSM_HEADS).reshape(1, W)
    row_spec = pl.BlockSpec((1, tm, W), lambda b, i: (b, i, 0))
    return pl.pallas_call(
        _ssm_finish_kernel,
        grid=(B, T // tm),
        in_specs=[row_spec, row_spec, row_spec,
                  pl.BlockSpec((1, tm, W), lambda b, i: (b, i, P_BZ)),
                  pl.BlockSpec((1, W), lambda b, i: (0, 0)),
                  pl.BlockSpec((1, W), lambda b, i: (0, 0))],
        out_specs=row_spec,
        out_shape=jax.ShapeDtypeStruct((B, T, W), BF16),
        compiler_params=_cparams(("arbitrary", "arbitrary")),
        name="ssm_finish",
    )(y_f, y_b, u, p_plain, dsk, g_ssm.reshape(1, W).astype(F32))


def _merge_kernel(y0, y1, y2, y3, s0, s1, s2, s3, wb_ref, wo_ref, x_ref, gate_ref, g_ref, o_ref):
    m = None
    for k, (y, s) in enumerate(((y0, s0), (y1, s1), (y2, s2), (y3, s3))):
        t = _dot(y[0], wb_ref[k]) * s[0].astype(F32)
        m = t if m is None else m + t
    o = _dot(m.astype(BF16), wo_ref[...])
    o_ref[0] = x_ref[0] + gate_ref[0] * _rms(o, g_ref[...])


MERGE_TM = 256


def _merge(ya, yb_all, ys, yn, p_gate, w_branch, w_out, l, x, gate, g_post, *, row_off):
    B, Tx, D = x.shape
    tm = MERGE_TM
    assert Tx % tm == 0 and row_off % tm == 0
    off = row_off // tm
    y_spec = pl.BlockSpec((1, tm, BRANCH_W), lambda b, i: (b, i, 0))
    yb_spec = pl.BlockSpec((1, tm, BRANCH_W), lambda b, i: (b, off + i, 0))
    s_spec = lambda k: pl.BlockSpec((1, tm, D), lambda b, i: (b, off + i, k))
    resident = lambda shape: pl.BlockSpec((None,) + shape, lambda b, i: (l,) + (0,) * len(shape),
                                          pipeline_mode=pl.Buffered(1))
    return pl.pallas_call(
        _merge_kernel,
        grid=(B, Tx // tm),
        in_specs=[y_spec, yb_spec, y_spec, y_spec] + [s_spec(k) for k in range(N_BRANCH)]
                 + [resident((N_BRANCH, BRANCH_W, D)), resident((D, D)),
                    pl.BlockSpec((1, tm, D), lambda b, i: (b, i, 0)),
                    pl.BlockSpec((1, 1, D), lambda b, i: (b, 0, 0)),
                    pl.BlockSpec((1, D), lambda b, i: (0, 0))],
        out_specs=pl.BlockSpec((1, tm, D), lambda b, i: (b, i, 0)),
        out_shape=jax.ShapeDtypeStruct((B, Tx, D), F32),
        compiler_params=_cparams(("arbitrary", "arbitrary")),
        name="merge_out_proj",
    )(ya, yb_all, ys, yn, p_gate, p_gate, p_gate, p_gate, w_branch, w_out, x, gate, g_post)


def _ffn_kernel(x_ref, g1_ref, sh_ref, sc_ref, gate_ref, g2_ref, w1_ref, w2_ref, o_ref, h_scr, acc):
    f = pl.program_id(2)

    @pl.when(f == 0)
    def _():
        h = _rms(x_ref[0], g1_ref[...]) * (1.0 + sc_ref[0]) + sh_ref[0]
        h_scr[...] = h.astype(BF16)
        acc[...] = jnp.zeros(acc.shape, F32)

    a = jnp.square(jnp.maximum(_dot(h_scr[...], w1_ref[...]), 0.0)).astype(BF16)
    acc[...] += _dot(a, w2_ref[...])

    @pl.when(f == pl.num_programs(2) - 1)
    def _():
        o_ref[0] = x_ref[0] + gate_ref[0] * _rms(acc[...], g2_ref[...])


def _ffn(x, g_pre, shift, scale, gate, g_post, w1, w2, l, *, tm, tf=1024):
    B, T, D = x.shape
    F = w1.shape[2]
    tm = min(tm, T)
    mod_spec = pl.BlockSpec((1, 1, D), lambda b, i, f: (b, 0, 0))
    vec_spec = pl.BlockSpec((1, D), lambda b, i, f: (0, 0))
    return pl.pallas_call(
        _ffn_kernel,
        grid=(B, T // tm, F // tf),
        in_specs=[pl.BlockSpec((1, tm, D), lambda b, i, f: (b, i, 0)),
                  vec_spec, mod_spec, mod_spec, mod_spec, vec_spec,
                  pl.BlockSpec((None, D, tf), lambda b, i, f: (l, 0, f)),
                  pl.BlockSpec((None, tf, D), lambda b, i, f: (l, f, 0))],
        out_specs=pl.BlockSpec((1, tm, D), lambda b, i, f: (b, i, 0)),
        out_shape=jax.ShapeDtypeStruct((B, T, D), F32),
        scratch_shapes=[pltpu.VMEM((tm, D), BF16), pltpu.VMEM((tm, D), F32)],
        compiler_params=_cparams(("arbitrary", "arbitrary", "arbitrary")),
        name="mlp",
    )(x, g_pre, shift, scale, gate, g_post, w1, w2)


def _layout_w_in(w):
    depth, D, _ = w.shape
    bounds, o = {}, 0
    for name, width in (("aq", 512), ("ak", 512), ("av", 512), ("bz", 512), ("bx", SSM_XBC), ("bdt", 16),
                        ("sq", 512), ("sk", 128), ("sv", 128), ("nq", 512), ("nk", 512), ("nv", 512),
                        ("gate", N_BRANCH * D)):
        bounds[name] = (o, o + width)
        o += width
    assert o == w.shape[2]
    w_t = jnp.swapaxes(w, 1, 2)
    part = lambda name: w_t[:, bounds[name][0]:bounds[name][1]]
    qpart = lambda name: part(name) * (QSCALE * LOG2E)
    sq = qpart("sq").reshape(depth, SW_KV, 4, HEAD_DIM, D).transpose(0, 2, 1, 3, 4).reshape(depth, 512, D)
    zpad = jnp.zeros((depth, LANES - SSM_HEADS, D), w.dtype)
    bdt = part("bdt")
    misc = jnp.concatenate([part("sk"), part("sv"), bdt[:, :SSM_HEADS], zpad, bdt[:, SSM_HEADS:], zpad], axis=1)
    w_rope = jnp.concatenate([qpart("aq"), part("ak"), sq, misc], axis=1)
    w_plain = jnp.concatenate([part("bx"), qpart("nq"), part("nk"), part("nv"), part("av"), part("bz")], axis=1)
    return w_rope.astype(BF16), w_plain.astype(BF16), w_t.astype(BF16), bounds["gate"][0]


def _rope_tables(n, n_ctx):
    t = np.arange(n)
    nf = HEAD_DIM // 4
    inv = ROPE_BASE ** (-np.arange(nf, dtype=np.float32) / nf)
    row = (t // GRID_W).astype(np.float32)
    col = (t % GRID_W).astype(np.float32)
    ang = jnp.asarray(np.concatenate([row[:, None] * inv, col[:, None] * inv], -1).astype(np.float32))
    cos, sin = jnp.cos(ang), jnp.sin(ang)
    cos = jnp.concatenate([cos, jnp.ones((n_ctx, cos.shape[1]), F32)], axis=0)
    sin = jnp.concatenate([sin, jnp.zeros((n_ctx, sin.shape[1]), F32)], axis=0)
    z = jnp.zeros_like(sin)
    rep = LANES // HEAD_DIM
    cos_t = jnp.tile(jnp.concatenate([cos, cos], -1), (1, rep))
    s1 = jnp.tile(jnp.concatenate([-sin, z], -1), (1, rep))
    s2 = jnp.tile(jnp.concatenate([z, sin], -1), (1, rep))
    return cos_t, s1, s2


def kernel(x, c, ctx, c_ctx, w_mod, b_mod, g_pre_mix, g_post_mix, g_pre_mlp, g_post_mlp, w_in, lam_q1, lam_k1, lam_q2, lam_k2, g_subln, conv_w, conv_b, dt_bias, a_log, d_skip, g_ssm, sink, rpb, w_branch, w_out, w_ff1, w_ff2):
    B, N, D = x.shape
    L = ctx.shape[1]
    T = N + L
    depth = w_mod.shape[0]
    assert B + 1 <= 8 and N % GRID_W == 0 and N % L == 0
    R = N // GRID_W

    cvec = jnp.zeros((8, D), F32).at[:B].set(c).at[B].set(c_ctx)
    mods = _modulation(cvec, w_mod, b_mod)
    rope_tabs = _rope_tables(N, L)
    na_bias = _na_bias(_na_col_bias(rpb), R, L)
    no_sink = jnp.full((NA_HEADS,), NEG, F32)
    row = lambda v: v.reshape(1, -1).astype(F32)
    tk_lat = _largest_divisor(T, (768, 512, 256, 128))
    w_rope, w_plain, w_all, gate_row0 = _layout_w_in(w_in)
    wb = w_branch.at[:, 2].set(w_branch[:, 2].reshape(depth, SW_KV, 4, HEAD_DIM, D).transpose(0, 2, 1, 3, 4)
                               .reshape(depth, BRANCH_W, D)).astype(BF16)
    wo, w1, w2 = w_out.astype(BF16), w_ff1.astype(BF16), w_ff2.astype(BF16)

    cx = ctx
    for l in range(depth):
        need_ctx = l < depth - 1
        lam_init = 0.8 - 0.6 * math.exp(-0.3 * l)
        mod = mods[l, :B].reshape(B, 6, 1, D)
        modc = jnp.broadcast_to(mods[l, B].reshape(1, 6, 1, D), (B, 6, 1, D))
        both = lambda k: jnp.stack([mod[:, k], modc[:, k]], axis=1).reshape(B * 2, 1, D)

        h = _prenorm(x, cx, row(g_pre_mix[l]), both(0), both(1))
        p_rope, dtr = _proj_rope(h, rope_tabs, w_rope, l)
        p_plain = _proj(h, w_plain, l, gate=False, tn=TN)
        p_gate = _proj(h, w_all, l, gate=True, tn=2048, rows=T if need_ctx else N, w_row0=gate_row0,
                       n_cols=N_BRANCH * D)

        lamv = jnp.stack([lam_q1[l], lam_k1[l], lam_q2[l], lam_k2[l]]).astype(F32)
        g_col = g_subln[l].reshape(LANES, 1).astype(F32)
        ya = _diff_attn(p_rope, p_plain, lamv, g_col, lam_init, q_row0=0, n_q=N, k_row0=0, n_k=T, tq=512, tk=tk_lat)

        u = _conv_silu(p_plain, conv_w[l], conv_b[l], N)
        y_f, y_b = _ssd(u, dtr, dt_bias[l], a_log[l], N)
        yb_all = _ssm_finish(y_f, y_b, u, p_plain, d_skip[l], g_ssm[l])

        ys = _swa(p_rope, sink[l].astype(F32), N, L)
        yn = _na(p_plain, na_bias[l], N, L)

        x_new = _merge(ya, yb_all, ys, yn, p_gate, wb, wo, l, x, mod[:, 2], row(g_post_mix[l]), row_off=0)
        x_new = _ffn(x_new, row(g_pre_mlp[l]), mod[:, 3], mod[:, 4], mod[:, 5], row(g_post_mlp[l]), w1, w2, l,
                     tm=512)
        if need_ctx:
            ya_c = _diff_attn(p_rope, p_plain, lamv, g_col, lam_init, q_row0=N, n_q=L, k_row0=N, n_k=L, tq=L, tk=L)
            ys_c = _ctx_attend(p_rope, sink[l].astype(F32), R_SQ, R_MISC * 4, R_MISC * 4 + 1, N, L, gqa=True)
            yn_c = _ctx_attend(p_plain, no_sink, P_NQ, P_NK, P_NV, N, L, gqa=False)
            cx = _merge(ya_c, yb_all, ys_c, yn_c, p_gate, wb, wo, l, cx, modc[:, 2], row(g_post_mix[l]), row_off=N)
            cx = _ffn(cx, row(g_pre_mlp[l]), modc[:, 3], modc[:, 4], modc[:, 5], row(g_post_mlp[l]), w1, w2, l,
                      tm=L)
        x = x_new
    return x
```
